```python
import jax
import jax.numpy as jnp
from jax import lax
import numpy as np

D_MODEL = 1024
BATCH = 4
SEQ = 4096
DEPTH = 4
DEC_BATCH = 2
DEC_SEQ = 16384
PAST_LEN = 128

N_MIXERS = 3
EPS = 1e-6
CONV_WIDTH = 3
MLA_HEADS = 8
Q_LORA = 384
KV_LORA = 256
QK_NOPE = 128
QK_ROPE = 64
V_HEAD = 128
ROPE_THETA = 10000.0
Q_BLOCK = 128
ML_HEADS = 8
ML_QK = D_MODEL // (2 * ML_HEADS)
ML_V = D_MODEL // ML_HEADS
ML_CHUNK = 64
N_EXPERTS = 16
EC_CAPACITY = 2
D_EXPERT = 1024

kernel_name = 'hybrid_bidir_encoder_conv_mla_mlstm_ec'


def rmsnorm(x, g):
    xf = x.astype(jnp.float32)
    y = xf * lax.rsqrt(jnp.mean(xf * xf, axis=-1, keepdims=True) + EPS)
    return (y * g.astype(jnp.float32)).astype(x.dtype)


def short_conv_mixer(h, w_in, w_dw, w_out):
    S = h.shape[1]
    gb, gc, xv = jnp.split(h @ w_in, 3, axis=-1)
    u = gc * xv
    half = CONV_WIDTH // 2
    up = jnp.pad(u, ((0, 0), (half, half), (0, 0)))
    conv = sum(up[:, t:t + S] * w_dw[t] for t in range(CONV_WIDTH))
    return (gb * conv) @ w_out


def rope_tables(S):
    pos = jnp.arange(S, dtype=jnp.float32)
    inv = ROPE_THETA ** (-jnp.arange(0, QK_ROPE, 2, dtype=jnp.float32) / QK_ROPE)
    ang = pos[:, None] * inv[None, :]
    return jnp.cos(ang), jnp.sin(ang)


def apply_rope(x, cos, sin):
    x1, x2 = jnp.split(x.astype(jnp.float32), 2, axis=-1)
    c = cos[:, None, :]
    s = sin[:, None, :]
    return jnp.concatenate([x1 * c - x2 * s, x1 * s + x2 * c], axis=-1).astype(x.dtype)


def mla_mixer(h, w_in, q_norm, w_qb, kv_norm, w_kvb, w_out):
    B, S, _ = h.shape
    H = MLA_HEADS
    lat = h @ w_in
    q_lat, kv_lat, k_pe = jnp.split(lat, [Q_LORA, Q_LORA + KV_LORA], axis=-1)
    q = (rmsnorm(q_lat, q_norm) @ w_qb).reshape(B, S, H, QK_NOPE + QK_ROPE)
    kv = (rmsnorm(kv_lat, kv_norm) @ w_kvb).reshape(B, S, H, QK_NOPE + V_HEAD)
    q_nope, q_pe = jnp.split(q, [QK_NOPE], axis=-1)
    k_nope, v = jnp.split(kv, [QK_NOPE], axis=-1)
    cos, sin = rope_tables(S)
    q_pe = apply_rope(q_pe, cos, sin)
    k_pe = apply_rope(k_pe[:, :, None, :], cos, sin)[:, :, 0, :]
    scale = (QK_NOPE + QK_ROPE) ** -0.5
    nb = S // Q_BLOCK

    def to_blocks(t):
        return t.reshape(B, nb, Q_BLOCK, H, t.shape[-1]).swapaxes(0, 1)

    def attend(blk):
        qn, qp = blk
        s = (jnp.einsum('bqhd,bkhd->bhqk', qn, k_nope, preferred_element_type=jnp.float32)
             + jnp.einsum('bqhr,bkr->bhqk', qp, k_pe, preferred_element_type=jnp.float32))
        p = jax.nn.softmax(s * scale, axis=-1).astype(v.dtype)
        return jnp.einsum('bhqk,bkhd->bqhd', p, v)

    o = lax.map(attend, (to_blocks(q_nope), to_blocks(q_pe)))
    o = o.swapaxes(0, 1).reshape(B, S, H * V_HEAD)
    return o @ w_out


def mlstm_chunkwise(q, k, v, li, lf):
    B, H, S, dk = q.shape
    dv = v.shape[-1]
    L = ML_CHUNK
    nc = S // L
    q = q.reshape(B, H, nc, L, dk)
    k = k.reshape(B, H, nc, L, dk)
    v = v.reshape(B, H, nc, L, dv)
    li = li.reshape(B, H, nc, L)
    lf = lf.reshape(B, H, nc, L)
    b = jnp.cumsum(lf, axis=-1)
    a = b[..., -1]
    w_end = a[..., None] - b + li
    g = jnp.max(w_end, axis=-1)
    e = jnp.exp(w_end - g[..., None])
    kv_c = jnp.einsum('bhcl,bhcld,bhcle->bhcde', e, k, v)
    n_c = jnp.einsum('bhcl,bhcld->bhcd', e, k)

    def step(carry, xs):
        C, n, m = carry
        a_c, g_c, kv_i, n_i = xs
        m_new = jnp.maximum(a_c + m, g_c)
        fdec = jnp.exp(a_c + m - m_new)
        iin = jnp.exp(g_c - m_new)
        C_new = fdec[..., None, None] * C + iin[..., None, None] * kv_i
        n_new = fdec[..., None] * n + iin[..., None] * n_i
        return (C_new, n_new, m_new), (C, n, m)

    init = (jnp.zeros((B, H, dk, dv), jnp.float32), jnp.zeros((B, H, dk), jnp.float32),
            jnp.zeros((B, H), jnp.float32))
    xs = (jnp.moveaxis(a, 2, 0), jnp.moveaxis(g, 2, 0), jnp.moveaxis(kv_c, 2, 0), jnp.moveaxis(n_c, 2, 0))
    _, (C_in, n_in, m_in) = lax.scan(step, init, xs)
    C_in = jnp.moveaxis(C_in, 0, 2)
    n_in = jnp.moveaxis(n_in, 0, 2)
    m_in = jnp.moveaxis(m_in, 0, 2)

    mask = jnp.tril(jnp.ones((L, L), dtype=bool))
    D = jnp.where(mask, b[..., :, None] - b[..., None, :] + li[..., None, :], -jnp.inf)
    inter_log = b + m_in[..., None]
    m = jnp.maximum(inter_log, jnp.max(D, axis=-1))
    inter = jnp.exp(inter_log - m)
    qk = jnp.einsum('bhcjd,bhcsd->bhcjs', q, k) * jnp.exp(D - m[..., None])
    num = (jnp.einsum('bhcjs,bhcse->bhcje', qk, v)
           + inter[..., None] * jnp.einsum('bhcjd,bhcde->bhcje', q, C_in))
    den = jnp.sum(qk, axis=-1) + inter * jnp.einsum('bhcjd,bhcd->bhcj', q, n_in)
    hout = num / jnp.maximum(jnp.abs(den), jnp.exp(-m))[..., None]
    return hout.reshape(B, H, S, dv)


def mlstm_mixer(h, w_in, b_gates, head_norm, w_out):
    B, S, _ = h.shape
    H = ML_HEADS
    qd = H * ML_QK
    vd = H * ML_V
    z = h @ w_in
    q, k, v, o, gates = jnp.split(z, [qd, 2 * qd, 2 * qd + vd, 2 * qd + 2 * vd], axis=-1)
    gates = gates.astype(jnp.float32) + b_gates.astype(jnp.float32)
    i_f, f_f, i_b, f_b = gates.reshape(B, S, 4, H).transpose(2, 0, 3, 1)

    def heads(t, d):
        return t.reshape(B, S, H, d).transpose(0, 2, 1, 3).astype(jnp.float32)

    qh = heads(q, ML_QK) * (ML_QK ** -0.5)
    kh = heads(k, ML_QK)
    vh = heads(v, ML_V)
    fwd = mlstm_chunkwise(qh, kh, vh, i_f, jax.nn.log_sigmoid(f_f))

    def flip(t):
        return jnp.flip(t, axis=2)

    bwd = flip(mlstm_chunkwise(flip(qh), flip(kh), flip(vh), flip(i_b), flip(jax.nn.log_sigmoid(f_b))))
    hs = fwd + bwd
    hs = hs * lax.rsqrt(jnp.mean(hs * hs, axis=-1, keepdims=True) + EPS)
    hs = hs * head_norm.astype(jnp.float32).reshape(H, 1, ML_V)
    hs = hs.transpose(0, 2, 1, 3).reshape(B, S, vd) * jax.nn.sigmoid(o.astype(jnp.float32))
    return hs.astype(h.dtype) @ w_out


def expert_choice_ffn(h, w_router, w_gate, w_up, w_down):
    B, S, D = h.shape
    N = B * S
    cap = EC_CAPACITY * N // N_EXPERTS
    xt = h.reshape(N, D)
    aff = jax.nn.softmax((xt @ w_router).astype(jnp.float32), axis=-1)
    gate, idx = lax.top_k(aff.T, cap)
    xe = xt[idx]
    hid = (jax.nn.silu(jnp.einsum('ecd,edf->ecf', xe, w_gate))
           * jnp.einsum('ecd,edf->ecf', xe, w_up))
    ye = jnp.einsum('ecf,efd->ecd', hid, w_down) * gate[..., None].astype(h.dtype)
    out = jnp.zeros_like(xt).at[idx.reshape(-1)].add(ye.reshape(-1, D))
    return out.reshape(B, S, D)


def trunk(x, p):
    h = x
    for i in range(DEPTH):
        j = i // N_MIXERS
        kind = i % N_MIXERS
        hn = rmsnorm(h, p['norm_mix'][i])
        if kind == 0:
            mix = short_conv_mixer(hn, p['conv_w_in'][j], p['conv_w_dw'][j], p['conv_w_out'][j])
        elif kind == 1:
            mix = mla_mixer(hn, p['mla_w_in'][j], p['mla_q_norm'][j], p['mla_w_qb'][j],
                            p['mla_kv_norm'][j], p['mla_w_kvb'][j], p['mla_w_out'][j])
        else:
            mix = mlstm_mixer(hn, p['ml_w_in'][j], p['ml_b_gates'][j], p['ml_head_norm'][j], p['ml_w_out'][j])
        h = h + mix
        h = h + expert_choice_ffn(rmsnorm(h, p['norm_ffn'][i]), p['router_w'][i],
                                  p['exp_w_gate'][i], p['exp_w_up'][i], p['exp_w_down'][i])
    return rmsnorm(h, p['norm_final'])


def setup_inputs(seed: int = 0) -> dict:
    key = jax.random.key(seed)
    k = jax.random.split(key, 24)
    f32 = jnp.float32
    n_conv = (DEPTH + 2) // 3
    n_mla = (DEPTH + 1) // 3
    n_ml = DEPTH // 3

    def w(kk, shape, fan_in):
        return jax.random.normal(kk, shape, f32) * (fan_in ** -0.5)

    def gain(kk, shape):
        return 1.0 + 0.02 * jax.random.normal(kk, shape, f32)

    ml_in_width = 2 * ML_HEADS * ML_QK + 2 * ML_HEADS * ML_V + 4 * ML_HEADS
    i_bias = 0.1 * jax.random.normal(k[12], (n_ml, 2, ML_HEADS), f32)
    f_bias = jnp.linspace(3.0, 6.0, ML_HEADS, dtype=f32) + 0.1 * jax.random.normal(k[13], (n_ml, 2, ML_HEADS), f32)
    b_gates = jnp.stack([i_bias[:, 0], f_bias[:, 0], i_bias[:, 1], f_bias[:, 1]], axis=1).reshape(n_ml, 4 * ML_HEADS)
    return {
        'x_prompt': jax.random.normal(k[0], (BATCH, SEQ, D_MODEL), f32),
        'x_sample': jax.random.normal(k[1], (DEC_BATCH, DEC_SEQ, D_MODEL), f32),
        'conv_w_in': w(k[2], (n_conv, D_MODEL, 3 * D_MODEL), D_MODEL),
        'conv_w_dw': w(k[3], (n_conv, CONV_WIDTH, D_MODEL), CONV_WIDTH),
        'conv_w_out': w(k[4], (n_conv, D_MODEL, D_MODEL), D_MODEL),
        'mla_w_in': w(k[5], (n_mla, D_MODEL, Q_LORA + KV_LORA + QK_ROPE), D_MODEL),
        'mla_q_norm': gain(k[6], (n_mla, Q_LORA)),
        'mla_w_qb': w(k[7], (n_mla, Q_LORA, MLA_HEADS * (QK_NOPE + QK_ROPE)), Q_LORA),
        'mla_kv_norm': gain(k[8], (n_mla, KV_LORA)),
        'mla_w_kvb': w(k[9], (n_mla, KV_LORA, MLA_HEADS * (QK_NOPE + V_HEAD)), KV_LORA),
        'mla_w_out': w(k[10], (n_mla, MLA_HEADS * V_HEAD, D_MODEL), MLA_HEADS * V_HEAD),
        'ml_w_in': w(k[11], (n_ml, D_MODEL, ml_in_width), D_MODEL),
        'ml_b_gates': b_gates,
        'ml_head_norm': gain(k[14], (n_ml, ML_HEADS * ML_V)),
        'ml_w_out': w(k[15], (n_ml, ML_HEADS * ML_V, D_MODEL), ML_HEADS * ML_V),
        'norm_mix': gain(k[16], (DEPTH, D_MODEL)),
        'norm_ffn': gain(k[17], (DEPTH, D_MODEL)),
        'router_w': w(k[18], (DEPTH, D_MODEL, N_EXPERTS), D_MODEL),
        'exp_w_gate': w(k[19], (DEPTH, N_EXPERTS, D_MODEL, D_EXPERT), D_MODEL),
        'exp_w_up': w(k[20], (DEPTH, N_EXPERTS, D_MODEL, D_EXPERT), D_MODEL),
        'exp_w_down': w(k[21], (DEPTH, N_EXPERTS, D_EXPERT, D_MODEL), D_EXPERT),
        'norm_final': gain(k[22], (D_MODEL,)),
    }


def reference(x_prompt, x_sample, conv_w_in, conv_w_dw, conv_w_out, mla_w_in, mla_q_norm, mla_w_qb,
              mla_kv_norm, mla_w_kvb, mla_w_out, ml_w_in, ml_b_gates, ml_head_norm, ml_w_out,
              norm_mix, norm_ffn, router_w, exp_w_gate, exp_w_up, exp_w_down, norm_final):
    p = dict(conv_w_in=conv_w_in, conv_w_dw=conv_w_dw, conv_w_out=conv_w_out,
             mla_w_in=mla_w_in, mla_q_norm=mla_q_norm, mla_w_qb=mla_w_qb,
             mla_kv_norm=mla_kv_norm, mla_w_kvb=mla_w_kvb, mla_w_out=mla_w_out,
             ml_w_in=ml_w_in, ml_b_gates=ml_b_gates, ml_head_norm=ml_head_norm, ml_w_out=ml_w_out,
             norm_mix=norm_mix, norm_ffn=norm_ffn, router_w=router_w,
             exp_w_gate=exp_w_gate, exp_w_up=exp_w_up, exp_w_down=exp_w_down, norm_final=norm_final)
    y_prompt = trunk(x_prompt, p)
    y_sample = trunk(x_sample, p)
    return (y_prompt, y_sample)
```

```python
import functools

import jax
import jax.numpy as jnp
from jax import lax
from jax.experimental import pallas as pl
from jax.experimental.pallas import tpu as pltpu

F32 = jnp.float32
BF16 = jnp.bfloat16

D_MODEL = 1024
DEPTH = 4
N_MIXERS = 3
EPS = 1e-6
CONV_WIDTH = 3
MLA_HEADS = 8
Q_LORA = 384
KV_LORA = 256
QK_NOPE = 128
QK_ROPE = 64
V_HEAD = 128
ROPE_THETA = 10000.0
ML_HEADS = 8
ML_QK = 64
ML_V = 128
N_EXPERTS = 16
EC_CAPACITY = 2
D_EXPERT = 1024

LANES_V7X = 128
BF16_SUBLANES_V7X = 16
VMEM_BYTES_V7X = 64 * 1024 * 1024
VMEM_LIMIT_BYTES = VMEM_BYTES_V7X - 8 * 1024 * 1024

TOKEN_TILE = 512
ATTN_Q_TILE = 512
ATTN_KV_TILE = 512
ML_CHUNK_TILE = 256

MLA_HEAD_PAD = 2 * LANES_V7X
ML_QK_PAD = LANES_V7X


def _params(*sem):
    return pltpu.CompilerParams(dimension_semantics=sem, vmem_limit_bytes=VMEM_LIMIT_BYTES)


def _rms(x, g):
    ms = jnp.mean(x * x, axis=-1, keepdims=True)
    return x * lax.rsqrt(ms + EPS) * g


def _tile(n, pref):
    t = min(n, pref)
    assert n % t == 0, (n, t)
    return t


def _conv_in_kernel(x_ref, g_ref, w_ref, u_ref, gb_ref):
    d = D_MODEL
    xn = _rms(x_ref[...], g_ref[...]).astype(BF16)
    gb = jnp.dot(xn, w_ref[:, 0:d], preferred_element_type=F32)
    gc = jnp.dot(xn, w_ref[:, d:2 * d], preferred_element_type=F32)
    xv = jnp.dot(xn, w_ref[:, 2 * d:3 * d], preferred_element_type=F32)
    gb_ref[...] = gb.astype(BF16)
    u_ref[...] = (gc * xv).astype(BF16)


def _conv_in(x, g, w):
    n, d = x.shape
    tm = _tile(n, TOKEN_TILE)
    return pl.pallas_call(
        _conv_in_kernel,
        grid=(n // tm,),
        in_specs=[pl.BlockSpec((tm, d), lambda i: (i, 0)),
                  pl.BlockSpec((1, d), lambda i: (0, 0)),
                  pl.BlockSpec((d, 3 * d), lambda i: (0, 0))],
        out_specs=[pl.BlockSpec((tm, d), lambda i: (i, 0)),
                   pl.BlockSpec((tm, d), lambda i: (i, 0))],
        out_shape=[jax.ShapeDtypeStruct((n, d), BF16), jax.ShapeDtypeStruct((n, d), BF16)],
        compiler_params=_params("parallel"),
        name="conv_in",
    )(x, g, w)


def _conv_out_kernel(u_ref, up_ref, un_ref, gb_ref, h_ref, wdw_ref, w_ref, o_ref, *, tiles_per_seq):
    tm = u_ref.shape[0]
    pos = pl.program_id(0) % tiles_per_seq
    u = u_ref[...].astype(F32)
    halo = BF16_SUBLANES_V7X
    prev_row = jnp.where(pos == 0, 0.0, up_ref[halo - 1:halo, :].astype(F32))
    next_row = jnp.where(pos == tiles_per_seq - 1, 0.0, un_ref[0:1, :].astype(F32))
    row = lax.broadcasted_iota(jnp.int32, (tm, 1), 0)
    u_up = jnp.where(row == 0, prev_row, pltpu.roll(u, 1, axis=0))
    u_dn = jnp.where(row == tm - 1, next_row, pltpu.roll(u, tm - 1, axis=0))
    conv = u_up * wdw_ref[0:1, :] + u * wdw_ref[1:2, :] + u_dn * wdw_ref[2:3, :]
    g = (gb_ref[...].astype(F32) * conv).astype(BF16)
    o_ref[...] = h_ref[...] + jnp.dot(g, w_ref[...], preferred_element_type=F32)


def _conv_out(u, gb, h, w_dw, w_out, seq):
    n, d = u.shape
    tm = _tile(seq, TOKEN_TILE)
    halo = BF16_SUBLANES_V7X
    r = tm // halo
    nblk = n // halo
    return pl.pallas_call(
        functools.partial(_conv_out_kernel, tiles_per_seq=seq // tm),
        grid=(n // tm,),
        in_specs=[pl.BlockSpec((tm, d), lambda i: (i, 0)),
                  pl.BlockSpec((halo, d), lambda i: (jnp.maximum(i * r - 1, 0), 0)),
                  pl.BlockSpec((halo, d), lambda i: (jnp.minimum((i + 1) * r, nblk - 1), 0)),
                  pl.BlockSpec((tm, d), lambda i: (i, 0)),
                  pl.BlockSpec((tm, d), lambda i: (i, 0)),
                  pl.BlockSpec((CONV_WIDTH, d), lambda i: (0, 0)),
                  pl.BlockSpec((d, d), lambda i: (0, 0))],
        out_specs=pl.BlockSpec((tm, d), lambda i: (i, 0)),
        out_shape=jax.ShapeDtypeStruct((n, d), F32),
        compiler_params=_params("parallel"),
        name="conv_out",
    )(u, u, u, gb, h, w_dw, w_out)


def _rope_group(x, c, s):
    return x * c + pltpu.roll(x, LANES_V7X // 2, axis=1) * s


def _mla_in_kernel(x_ref, g_ref, win_ref, qn_ref, kvn_ref, wqb_ref, wkvb_ref, cos_ref, sin_ref,
                   q_ref, k_ref, v_ref):
    hp = MLA_HEAD_PAD
    scale = (QK_NOPE + QK_ROPE) ** -0.5
    xn = _rms(x_ref[...], g_ref[...]).astype(BF16)
    lat = jnp.dot(xn, win_ref[...], preferred_element_type=F32)
    qn = _rms(lat[:, 0:Q_LORA], qn_ref[...]).astype(BF16)
    kvn = _rms(lat[:, Q_LORA:Q_LORA + KV_LORA], kvn_ref[...]).astype(BF16)
    c = cos_ref[...]
    s = sin_ref[...]
    k_pe = _rope_group(lat[:, Q_LORA + KV_LORA:], c, s).astype(BF16)
    for h in range(MLA_HEADS):
        qh = jnp.dot(qn, wqb_ref[:, h * hp:(h + 1) * hp], preferred_element_type=F32)
        q_ref[:, h * hp:h * hp + QK_NOPE] = (qh[:, 0:QK_NOPE] * scale).astype(BF16)
        q_ref[:, h * hp + QK_NOPE:(h + 1) * hp] = (_rope_group(qh[:, QK_NOPE:], c, s) * scale).astype(BF16)
        kvh = jnp.dot(kvn, wkvb_ref[:, h * hp:(h + 1) * hp], preferred_element_type=F32)
        k_ref[:, h * hp:h * hp + QK_NOPE] = kvh[:, 0:QK_NOPE].astype(BF16)
        k_ref[:, h * hp + QK_NOPE:(h + 1) * hp] = k_pe
        v_ref[:, h * V_HEAD:(h + 1) * V_HEAD] = kvh[:, QK_NOPE:].astype(BF16)


def _mla_in(x, g, w_in, q_norm, w_qb, kv_norm, w_kvb, cos, sin, seq):
    n, d = x.shape
    tm = _tile(seq, TOKEN_TILE)
    tps = seq // tm
    hq = MLA_HEADS * MLA_HEAD_PAD
    const = lambda i: (0, 0)
    return pl.pallas_call(
        _mla_in_kernel,
        grid=(n // tm,),
        in_specs=[pl.BlockSpec((tm, d), lambda i: (i, 0)),
                  pl.BlockSpec((1, d), const),
                  pl.BlockSpec(w_in.shape, const),
                  pl.BlockSpec((1, Q_LORA), const),
                  pl.BlockSpec((1, KV_LORA), const),
                  pl.BlockSpec(w_qb.shape, const),
                  pl.BlockSpec(w_kvb.shape, const),
                  pl.BlockSpec((tm, LANES_V7X), lambda i: (i % tps, 0)),
                  pl.BlockSpec((tm, LANES_V7X), lambda i: (i % tps, 0))],
        out_specs=[pl.BlockSpec((tm, hq), lambda i: (i, 0)),
                   pl.BlockSpec((tm, hq), lambda i: (i, 0)),
                   pl.BlockSpec((tm, MLA_HEADS * V_HEAD), lambda i: (i, 0))],
        out_shape=[jax.ShapeDtypeStruct((n, hq), BF16),
                   jax.ShapeDtypeStruct((n, hq), BF16),
                   jax.ShapeDtypeStruct((n, MLA_HEADS * V_HEAD), BF16)],
        compiler_params=_params("parallel"),
        name="mla_in",
    )(x, g, w_in, q_norm, kv_norm, w_qb, w_kvb, cos, sin)


def _attn_kernel(q_ref, k_ref, v_ref, o_ref, *, tk):
    tq = q_ref.shape[0]
    nk = k_ref.shape[0] // tk
    q = q_ref[...]

    def body(j, carry):
        m, l, acc = carry
        start = pl.multiple_of(j * tk, tk)
        k = k_ref[pl.ds(start, tk), :]
        v = v_ref[pl.ds(start, tk), :]
        s = lax.dot_general(q, k, (((1,), (1,)), ((), ())), preferred_element_type=F32)
        m_new = jnp.maximum(m, jnp.max(s, axis=-1, keepdims=True))
        alpha = jnp.exp(m - m_new)
        p = jnp.exp(s - m_new)
        l = alpha * l + jnp.sum(p, axis=-1, keepdims=True)
        acc = alpha * acc + jnp.dot(p.astype(BF16), v, preferred_element_type=F32)
        return m_new, l, acc

    init = (jnp.full((tq, 1), -jnp.inf, F32), jnp.zeros((tq, 1), F32), jnp.zeros((tq, V_HEAD), F32))
    _, l, acc = lax.fori_loop(0, nk, body, init)
    o_ref[...] = (acc / l).astype(BF16)


def _attention(q, k, v, batch, seq):
    n = q.shape[0]
    tq = _tile(seq, ATTN_Q_TILE)
    tk = _tile(seq, ATTN_KV_TILE)
    nq = seq // tq
    hp = MLA_HEAD_PAD
    return pl.pallas_call(
        functools.partial(_attn_kernel, tk=tk),
        grid=(batch, MLA_HEADS, nq),
        in_specs=[pl.BlockSpec((tq, hp), lambda b, h, i: (b * nq + i, h)),
                  pl.BlockSpec((seq, hp), lambda b, h, i: (b, h)),
                  pl.BlockSpec((seq, V_HEAD), lambda b, h, i: (b, h))],
        out_specs=pl.BlockSpec((tq, V_HEAD), lambda b, h, i: (b * nq + i, h)),
        out_shape=jax.ShapeDtypeStruct((n, MLA_HEADS * V_HEAD), BF16),
        compiler_params=_params("parallel", "parallel", "arbitrary"),
        name="mla_attention",
    )(q, k, v)


def _proj_residual_kernel(a_ref, h_ref, w_ref, o_ref):
    o_ref[...] = h_ref[...] + jnp.dot(a_ref[...], w_ref[...], preferred_element_type=F32)


def _proj_residual(a, h, w):
    n, d = h.shape
    tm = _tile(n, TOKEN_TILE)
    return pl.pallas_call(
        _proj_residual_kernel,
        grid=(n // tm,),
        in_specs=[pl.BlockSpec((tm, a.shape[1]), lambda i: (i, 0)),
                  pl.BlockSpec((tm, d), lambda i: (i, 0)),
                  pl.BlockSpec(w.shape, lambda i: (0, 0))],
        out_specs=pl.BlockSpec((tm, d), lambda i: (i, 0)),
        out_shape=jax.ShapeDtypeStruct((n, d), F32),
        compiler_params=_params("parallel"),
        name="proj_residual",
    )(a, h, w)


def _ml_in_kernel(x_ref, g_ref, w_ref, b_ref, q_ref, k_ref, v_ref, o_ref, gt_ref):
    qw = ML_HEADS * ML_QK_PAD
    vw = ML_HEADS * ML_V
    xn = _rms(x_ref[...], g_ref[...]).astype(BF16)
    q = jnp.dot(xn, w_ref[:, 0:qw], preferred_element_type=F32)
    q_ref[...] = (q * (ML_QK ** -0.5)).astype(BF16)
    k_ref[...] = jnp.dot(xn, w_ref[:, qw:2 * qw], preferred_element_type=F32).astype(BF16)
    v_ref[...] = jnp.dot(xn, w_ref[:, 2 * qw:2 * qw + vw], preferred_element_type=F32).astype(BF16)
    o_ref[...] = jnp.dot(xn, w_ref[:, 2 * qw + vw:2 * qw + 2 * vw], preferred_element_type=F32).astype(BF16)
    gt_ref[...] = jnp.dot(xn, w_ref[:, 2 * qw + 2 * vw:], preferred_element_type=F32) + b_ref[...]


def _ml_in(x, g, w, b_gates):
    n, d = x.shape
    tm = _tile(n, TOKEN_TILE)
    qw = ML_HEADS * ML_QK_PAD
    vw = ML_HEADS * ML_V
    ng = 4 * ML_HEADS
    row = lambda i: (i, 0)
    const = lambda i: (0, 0)
    return pl.pallas_call(
        _ml_in_kernel,
        grid=(n // tm,),
        in_specs=[pl.BlockSpec((tm, d), row), pl.BlockSpec((1, d), const),
                  pl.BlockSpec(w.shape, const), pl.BlockSpec((1, ng), const)],
        out_specs=[pl.BlockSpec((tm, qw), row), pl.BlockSpec((tm, qw), row),
                   pl.BlockSpec((tm, vw), row), pl.BlockSpec((tm, vw), row),
                   pl.BlockSpec((tm, ng), row)],
        out_shape=[jax.ShapeDtypeStruct((n, qw), BF16), jax.ShapeDtypeStruct((n, qw), BF16),
                   jax.ShapeDtypeStruct((n, vw), BF16), jax.ShapeDtypeStruct((n, vw), BF16),
                   jax.ShapeDtypeStruct((n, ng), F32)],
        compiler_params=_params("parallel"),
        name="mlstm_in",
    )(x, g, w, b_gates)


def _split3(x):
    hi = x.astype(BF16)
    r1 = x - hi.astype(F32)
    mid = r1.astype(BF16)
    lo = (r1 - mid.astype(F32)).astype(BF16)
    return hi, mid, lo


def _mlstm_kernel(q_ref, kt_ref, v_ref, gc_ref, gr_ref, o_ref, c_scr, m_scr):
    lc = q_ref.shape[0]
    nh = ML_HEADS
    kp = ML_QK_PAD
    dv = ML_V
    fwd = pl.program_id(0) == 0

    @pl.when(pl.program_id(2) == 0)
    def _():
        c_scr[...] = jnp.zeros_like(c_scr)
        m_scr[...] = jnp.zeros_like(m_scr)

    gcol = gc_ref[0]
    grow = gr_ref[0]
    lf_col = jax.nn.log_sigmoid(gcol[:, nh:2 * nh])
    li_row = grow[0:nh, :]
    lf_row = jax.nn.log_sigmoid(grow[nh:2 * nh, :])

    r = lax.broadcasted_iota(jnp.int32, (lc, lc), 0)
    cidx = lax.broadcasted_iota(jnp.int32, (lc, lc), 1)
    allowed = (r - cidx) * jnp.where(fwd, 1, -1) >= 0
    tri = jnp.where(allowed, 1.0, 0.0).astype(BF16)

    b_col = jnp.zeros((lc, nh), F32)
    for piece in _split3(lf_col):
        b_col = b_col + jnp.dot(tri, piece, preferred_element_type=F32)
    b_row = jnp.zeros((nh, lc), F32)
    for piece in _split3(lf_row):
        b_row = b_row + lax.dot_general(piece, tri, (((1,), (1,)), ((), ())), preferred_element_type=F32)
    a_all = jnp.sum(lf_row, axis=-1, keepdims=True)

    lane = lax.broadcasted_iota(jnp.int32, (lc, LANES_V7X), 1)
    ones_col = jnp.where(lane == 0, 1.0, 0.0).astype(BF16)

    for h in range(nh):
        q = q_ref[:, h * kp:(h + 1) * kp]
        kt = kt_ref[h * kp:(h + 1) * kp, :]
        v_ext = jnp.concatenate([v_ref[:, h * dv:(h + 1) * dv], ones_col], axis=1)
        b_c = b_col[:, h:h + 1]
        b_r = b_row[h:h + 1, :]
        li_r = li_row[h:h + 1, :]
        a = a_all[h:h + 1, :]
        m_in = m_scr[h][0:1, 0:1]
        c_ext = c_scr[h]

        w_end = a - b_r + li_r
        g = jnp.max(w_end, axis=-1, keepdims=True)
        e_r = jnp.exp(w_end - g)

        dmat = jnp.where(allowed, b_c - b_r + li_r, -jnp.inf)
        inter_log = b_c + m_in
        m_j = jnp.maximum(inter_log, jnp.max(dmat, axis=-1, keepdims=True))
        inter = jnp.exp(inter_log - m_j)
        p = jnp.exp(dmat - m_j)
        s = jnp.dot(q, kt, preferred_element_type=F32)
        qk = (s * p).astype(BF16)
        nd = (jnp.dot(qk, v_ext, preferred_element_type=F32)
              + inter * jnp.dot(q, c_ext.astype(BF16), preferred_element_type=F32))
        den = nd[:, dv:dv + 1]
        o_ref[0, :, h * dv:(h + 1) * dv] = nd[:, 0:dv] / jnp.maximum(jnp.abs(den), jnp.exp(-m_j))

        m_new = jnp.maximum(a + m_in, g)
        fdec = jnp.exp(a + m_in - m_new)
        iin = jnp.exp(g - m_new)
        kte = (kt.astype(F32) * e_r).astype(BF16)
        c_scr[h] = fdec * c_ext + iin * jnp.dot(kte, v_ext, preferred_element_type=F32)
        m_scr[h] = jnp.broadcast_to(m_new, m_scr.shape[1:])


def _mlstm(q, kt, v, gcol, grow, batch, seq):
    n = q.shape[0]
    lc = _tile(seq, ML_CHUNK_TILE)
    nc = seq // lc
    qw = ML_HEADS * ML_QK_PAD
    vw = ML_HEADS * ML_V

    def chunk(d, b, c):
        return b * nc + c + d * (nc - 1 - 2 * c)

    return pl.pallas_call(
        _mlstm_kernel,
        grid=(2, batch, nc),
        in_specs=[pl.BlockSpec((lc, qw), lambda d, b, c: (chunk(d, b, c), 0)),
                  pl.BlockSpec((qw, lc), lambda d, b, c: (0, chunk(d, b, c))),
                  pl.BlockSpec((lc, vw), lambda d, b, c: (chunk(d, b, c), 0)),
                  pl.BlockSpec((1, lc, 2 * ML_HEADS), lambda d, b, c: (d, chunk(d, b, c), 0)),
                  pl.BlockSpec((1, 2 * ML_HEADS, lc), lambda d, b, c: (d, 0, chunk(d, b, c)))],
        out_specs=pl.BlockSpec((1, lc, vw), lambda d, b, c: (d, chunk(d, b, c), 0)),
        out_shape=jax.ShapeDtypeStruct((2, n, vw), F32),
        scratch_shapes=[pltpu.VMEM((ML_HEADS, ML_QK_PAD, 2 * ML_V), F32),
                        pltpu.VMEM((ML_HEADS, 8, LANES_V7X), F32)],
        compiler_params=_params("parallel", "parallel", "arbitrary"),
        name="mlstm_scan",
    )(q, kt, v, gcol, grow)


def _ml_out_kernel(hs_ref, og_ref, hn_ref, h_ref, w_ref, o_ref):
    dv = ML_V
    hs = hs_ref[0] + hs_ref[1]
    parts = []
    for hd in range(ML_HEADS):
        x = hs[:, hd * dv:(hd + 1) * dv]
        parts.append(x * lax.rsqrt(jnp.mean(x * x, axis=-1, keepdims=True) + EPS))
    y = jnp.concatenate(parts, axis=1) * hn_ref[...] * jax.nn.sigmoid(og_ref[...].astype(F32))
    o_ref[...] = h_ref[...] + jnp.dot(y.astype(BF16), w_ref[...], preferred_element_type=F32)


def _ml_out(hs, og, head_norm, h, w):
    n, d = h.shape
    vw = ML_HEADS * ML_V
    tm = _tile(n, TOKEN_TILE)
    row = lambda i: (i, 0)
    const = lambda i: (0, 0)
    return pl.pallas_call(
        _ml_out_kernel,
        grid=(n // tm,),
        in_specs=[pl.BlockSpec((2, tm, vw), lambda i: (0, i, 0)),
                  pl.BlockSpec((tm, vw), row), pl.BlockSpec((1, vw), const),
                  pl.BlockSpec((tm, d), row), pl.BlockSpec(w.shape, const)],
        out_specs=pl.BlockSpec((tm, d), row),
        out_shape=jax.ShapeDtypeStruct((n, d), F32),
        compiler_params=_params("parallel"),
        name="mlstm_out",
    )(hs, og, head_norm, h, w)


def _router_kernel(x_ref, g_ref, w_ref, wlo_ref, xn_ref, aff_ref):
    xf = _rms(x_ref[...], g_ref[...])
    xn = xf.astype(BF16)
    xn_ref[...] = xn
    xlo = (xf - xn.astype(F32)).astype(BF16)
    logits = (jnp.dot(xn, w_ref[...], preferred_element_type=F32)
              + jnp.dot(xlo, w_ref[...], preferred_element_type=F32)
              + jnp.dot(xn, wlo_ref[...], preferred_element_type=F32))
    z = jnp.exp(logits - jnp.max(logits, axis=-1, keepdims=True))
    aff_ref[...] = z / jnp.sum(z, axis=-1, keepdims=True)


def _router(x, g, w_f32):
    n, d = x.shape
    tm = _tile(n, TOKEN_TILE)
    row = lambda i: (i, 0)
    const = lambda i: (0, 0)
    w = w_f32.astype(BF16)
    w_lo = (w_f32 - w.astype(F32)).astype(BF16)
    return pl.pallas_call(
        _router_kernel,
        grid=(n // tm,),
        in_specs=[pl.BlockSpec((tm, d), row), pl.BlockSpec((1, d), const),
                  pl.BlockSpec(w.shape, const), pl.BlockSpec(w.shape, const)],
        out_specs=[pl.BlockSpec((tm, d), row), pl.BlockSpec((tm, N_EXPERTS), row)],
        out_shape=[jax.ShapeDtypeStruct((n, d), BF16), jax.ShapeDtypeStruct((n, N_EXPERTS), F32)],
        compiler_params=_params("parallel"),
        name="moe_router",
    )(x, g, w, w_lo)


def _ffn_kernel(x_ref, gate_ref, wg_ref, wu_ref, wd_ref, o_ref):
    x = x_ref[0]
    g = jnp.dot(x, wg_ref[0], preferred_element_type=F32)
    u = jnp.dot(x, wu_ref[0], preferred_element_type=F32)
    hid = (g * jax.nn.sigmoid(g) * u).astype(BF16)
    y = jnp.dot(hid, wd_ref[0], preferred_element_type=F32)
    o_ref[0] = y * gate_ref[0]


def _expert_ffn(xe, gate, wg, wu, wd):
    e, cap, d = xe.shape
    f = wg.shape[-1]
    tm = _tile(cap, TOKEN_TILE)
    tok = lambda i, j: (i, j, 0)
    wmap = lambda i, j: (i, 0, 0)
    return pl.pallas_call(
        _ffn_kernel,
        grid=(e, cap // tm),
        in_specs=[pl.BlockSpec((1, tm, d), tok), pl.BlockSpec((1, tm, 1), tok),
                  pl.BlockSpec((1, d, f), wmap), pl.BlockSpec((1, d, f), wmap),
                  pl.BlockSpec((1, f, d), wmap)],
        out_specs=pl.BlockSpec((1, tm, d), tok),
        out_shape=jax.ShapeDtypeStruct((e, cap, d), F32),
        compiler_params=_params("parallel", "arbitrary"),
        name="moe_ffn",
    )(xe, gate, wg, wu, wd)


def _moe(h, g, w_router, wg, wu, wd):
    n, d = h.shape
    cap = EC_CAPACITY * n // N_EXPERTS
    xn, aff = _router(h, g, w_router)
    gate, idx = lax.top_k(aff.T, cap)
    xe = xn[idx]
    ye = _expert_ffn(xe, gate[..., None], wg, wu, wd)
    return h.at[idx.reshape(-1)].add(ye.reshape(-1, d))


def _final_norm_kernel(x_ref, g_ref, o_ref):
    o_ref[...] = _rms(x_ref[...], g_ref[...])


def _final_norm(x, g):
    n, d = x.shape
    tm = _tile(n, TOKEN_TILE)
    return pl.pallas_call(
        _final_norm_kernel,
        grid=(n // tm,),
        in_specs=[pl.BlockSpec((tm, d), lambda i: (i, 0)), pl.BlockSpec((1, d), lambda i: (0, 0))],
        out_specs=pl.BlockSpec((tm, d), lambda i: (i, 0)),
        out_shape=jax.ShapeDtypeStruct((n, d), F32),
        compiler_params=_params("parallel"),
        name="final_norm",
    )(x, g)


def _rope_group_cols(w_pe):
    half = QK_ROPE // 2
    z = jnp.zeros((w_pe.shape[0], LANES_V7X // 2 - half), w_pe.dtype)
    return jnp.concatenate([w_pe[:, :half], z, w_pe[:, half:], z], axis=1)


def _rope_tables(seq):
    half = QK_ROPE // 2
    pos = jnp.arange(seq, dtype=F32)
    inv = ROPE_THETA ** (-jnp.arange(0, QK_ROPE, 2, dtype=F32) / QK_ROPE)
    ang = pos[:, None] * inv[None, :]
    c, s = jnp.cos(ang), jnp.sin(ang)
    z = jnp.zeros((seq, LANES_V7X // 2 - half), F32)
    return jnp.concatenate([c, z, c, z], axis=1), jnp.concatenate([-s, z, s, z], axis=1)


def _prep_mla(w_in, w_qb, w_kvb):
    lat = Q_LORA + KV_LORA
    w_in_p = jnp.concatenate([w_in[:, :lat], _rope_group_cols(w_in[:, lat:])], axis=1).astype(BF16)
    hd = QK_NOPE + QK_ROPE
    cols = []
    for h in range(MLA_HEADS):
        cols.append(w_qb[:, h * hd:h * hd + QK_NOPE])
        cols.append(_rope_group_cols(w_qb[:, h * hd + QK_NOPE:(h + 1) * hd]))
    return w_in_p, jnp.concatenate(cols, axis=1).astype(BF16), w_kvb.astype(BF16)


def _prep_ml_in(w):
    qd = ML_HEADS * ML_QK
    cols = []
    for base in (0, qd):
        for h in range(ML_HEADS):
            cols.append(w[:, base + h * ML_QK:base + (h + 1) * ML_QK])
            cols.append(jnp.zeros((w.shape[0], ML_QK_PAD - ML_QK), w.dtype))
    cols.append(w[:, 2 * qd:])
    return jnp.concatenate(cols, axis=1).astype(BF16)


def _trunk(x, p):
    batch, seq, d = x.shape
    n = batch * seq
    h = x.reshape(n, d)
    for i in range(DEPTH):
        j = i // N_MIXERS
        kind = i % N_MIXERS
        g_mix = p["norm_mix"][i][None, :]
        if kind == 0:
            u, gb = _conv_in(h, g_mix, p["conv_w_in"][j])
            h = _conv_out(u, gb, h, p["conv_w_dw"][j], p["conv_w_out"][j], seq)
        elif kind == 1:
            cos, sin = _rope_tables(seq)
            q, k, v = _mla_in(h, g_mix, p["mla_w_in"][j], p["mla_q_norm"][j][None, :], p["mla_w_qb"][j],
                              p["mla_kv_norm"][j][None, :], p["mla_w_kvb"][j], cos, sin, seq)
            o = _attention(q, k, v, batch, seq)
            h = _proj_residual(o, h, p["mla_w_out"][j])
        else:
            q, k, v, og, gates = _ml_in(h, g_mix, p["ml_w_in"][j], p["ml_b_gates"][j][None, :])
            gcol = gates.reshape(n, 2, 2 * ML_HEADS).transpose(1, 0, 2)
            grow = gcol.transpose(0, 2, 1)
            hs = _mlstm(q, k.T, v, gcol, grow, batch, seq)
            h = _ml_out(hs, og, p["ml_head_norm"][j][None, :], h, p["ml_w_out"][j])
        h = _moe(h, p["norm_ffn"][i][None, :], p["router_w"][i],
                 p["exp_w_gate"][i], p["exp_w_up"][i], p["exp_w_down"][i])
    return _final_norm(h, p["norm_final"][None, :]).reshape(batch, seq, d)


def kernel(x_prompt, x_sample, conv_w_in, conv_w_dw, conv_w_out, mla_w_in, mla_q_norm, mla_w_qb, mla_kv_norm, mla_w_kvb, mla_w_out, ml_w_in, ml_b_gates, ml_head_norm, ml_w_out, norm_mix, norm_ffn, router_w, exp_w_gate, exp_w_up, exp_w_down, norm_final):
    mla = [_prep_mla(mla_w_in[j], mla_w_qb[j], mla_w_kvb[j]) for j in range(mla_w_in.shape[0])]
    p = dict(
        conv_w_in=conv_w_in.astype(BF16), conv_w_dw=conv_w_dw, conv_w_out=conv_w_out.astype(BF16),
        mla_w_in=[m[0] for m in mla], mla_q_norm=mla_q_norm, mla_w_qb=[m[1] for m in mla],
        mla_kv_norm=mla_kv_norm, mla_w_kvb=[m[2] for m in mla], mla_w_out=mla_w_out.astype(BF16),
        ml_w_in=[_prep_ml_in(ml_w_in[j]) for j in range(ml_w_in.shape[0])], ml_b_gates=ml_b_gates,
        ml_head_norm=ml_head_norm, ml_w_out=ml_w_out.astype(BF16),
        norm_mix=norm_mix, norm_ffn=norm_ffn, router_w=router_w,
        exp_w_gate=exp_w_gate.astype(BF16), exp_w_up=exp_w_up.astype(BF16),
        exp_w_down=exp_w_down.astype(BF16), norm_final=norm_final)
    return (_trunk(x_prompt, p), _trunk(x_sample, p))
```

```python
import functools

import jax
import jax.numpy as jnp
from jax import lax
from jax.experimental import pallas as pl
from jax.experimental.pallas import tpu as pltpu

F32 = jnp.float32
BF16 = jnp.bfloat16

D_MODEL = 1024
DEPTH = 4
N_MIXERS = 3
EPS = 1e-6
CONV_WIDTH = 3
MLA_HEADS = 8
Q_LORA = 384
KV_LORA = 256
QK_NOPE = 128
QK_ROPE = 64
V_HEAD = 128
ROPE_THETA = 10000.0
ML_HEADS = 8
ML_QK = 64
ML_V = 128
N_EXPERTS = 16
EC_CAPACITY = 2
D_EXPERT = 1024

LANES_V7X = 128
BF16_SUBLANES_V7X = 16
VMEM_BYTES_V7X = 64 * 1024 * 1024
VMEM_LIMIT_BYTES = VMEM_BYTES_V7X - 8 * 1024 * 1024

TOKEN_TILE = 512
ATTN_Q_TILE = 1024
ML_CHUNK_TILE = 256
COMBINE_WINDOW = 256

MLA_HEAD_PAD = 2 * LANES_V7X
VT_ROWS = V_HEAD + BF16_SUBLANES_V7X
LOG2_E = 1.4426950408889634
ML_QK_PAD = LANES_V7X


def _params(*sem):
    return pltpu.CompilerParams(dimension_semantics=sem, vmem_limit_bytes=VMEM_LIMIT_BYTES)


def _rms(x, g):
    ms = jnp.mean(x * x, axis=-1, keepdims=True)
    return x * lax.rsqrt(ms + EPS) * g


def _tile(n, pref):
    t = min(n, pref)
    assert n % t == 0, (n, t)
    return t


def _conv_in_kernel(x_ref, g_ref, w_ref, u_ref, gb_ref):
    d = D_MODEL
    xn = _rms(x_ref[...], g_ref[...]).astype(BF16)
    gb = jnp.dot(xn, w_ref[:, 0:d], preferred_element_type=F32)
    gc = jnp.dot(xn, w_ref[:, d:2 * d], preferred_element_type=F32)
    xv = jnp.dot(xn, w_ref[:, 2 * d:3 * d], preferred_element_type=F32)
    gb_ref[...] = gb.astype(BF16)
    u_ref[...] = (gc * xv).astype(BF16)


def _conv_in(x, g, w):
    n, d = x.shape
    tm = _tile(n, TOKEN_TILE)
    return pl.pallas_call(
        _conv_in_kernel,
        grid=(n // tm,),
        in_specs=[pl.BlockSpec((tm, d), lambda i: (i, 0)),
                  pl.BlockSpec((1, d), lambda i: (0, 0)),
                  pl.BlockSpec((d, 3 * d), lambda i: (0, 0))],
        out_specs=[pl.BlockSpec((tm, d), lambda i: (i, 0)),
                   pl.BlockSpec((tm, d), lambda i: (i, 0))],
        out_shape=[jax.ShapeDtypeStruct((n, d), BF16), jax.ShapeDtypeStruct((n, d), BF16)],
        compiler_params=_params("parallel"),
        name="conv_in",
    )(x, g, w)


def _conv_out_kernel(u_ref, up_ref, un_ref, gb_ref, h_ref, wdw_ref, w_ref, o_ref, *, tiles_per_seq):
    tm = u_ref.shape[0]
    pos = pl.program_id(0) % tiles_per_seq
    u = u_ref[...].astype(F32)
    halo = BF16_SUBLANES_V7X
    prev_row = jnp.where(pos == 0, 0.0, up_ref[halo - 1:halo, :].astype(F32))
    next_row = jnp.where(pos == tiles_per_seq - 1, 0.0, un_ref[0:1, :].astype(F32))
    row = lax.broadcasted_iota(jnp.int32, (tm, 1), 0)
    u_up = jnp.where(row == 0, prev_row, pltpu.roll(u, 1, axis=0))
    u_dn = jnp.where(row == tm - 1, next_row, pltpu.roll(u, tm - 1, axis=0))
    conv = u_up * wdw_ref[0:1, :] + u * wdw_ref[1:2, :] + u_dn * wdw_ref[2:3, :]
    g = (gb_ref[...].astype(F32) * conv).astype(BF16)
    o_ref[...] = h_ref[...] + jnp.dot(g, w_ref[...], preferred_element_type=F32)


def _conv_out(u, gb, h, w_dw, w_out, seq):
    n, d = u.shape
    tm = _tile(seq, TOKEN_TILE)
    halo = BF16_SUBLANES_V7X
    r = tm // halo
    nblk = n // halo
    return pl.pallas_call(
        functools.partial(_conv_out_kernel, tiles_per_seq=seq // tm),
        grid=(n // tm,),
        in_specs=[pl.BlockSpec((tm, d), lambda i: (i, 0)),
                  pl.BlockSpec((halo, d), lambda i: (jnp.maximum(i * r - 1, 0), 0)),
                  pl.BlockSpec((halo, d), lambda i: (jnp.minimum((i + 1) * r, nblk - 1), 0)),
                  pl.BlockSpec((tm, d), lambda i: (i, 0)),
                  pl.BlockSpec((tm, d), lambda i: (i, 0)),
                  pl.BlockSpec((CONV_WIDTH, d), lambda i: (0, 0)),
                  pl.BlockSpec((d, d), lambda i: (0, 0))],
        out_specs=pl.BlockSpec((tm, d), lambda i: (i, 0)),
        out_shape=jax.ShapeDtypeStruct((n, d), F32),
        compiler_params=_params("parallel"),
        name="conv_out",
    )(u, u, u, gb, h, w_dw, w_out)


def _rope_group(x, c, s):
    return x * c + pltpu.roll(x, LANES_V7X // 2, axis=1) * s


def _mla_in_kernel(x_ref, g_ref, win_ref, qn_ref, kvn_ref, wqb_ref, wk_ref, wvt_ref, cos_ref, sin_ref,
                   q_ref, k_ref, vt_ref):
    hp = MLA_HEAD_PAD
    tm = x_ref.shape[0]
    scale = (QK_NOPE + QK_ROPE) ** -0.5 * LOG2_E
    xn = _rms(x_ref[...], g_ref[...]).astype(BF16)
    lat = jnp.dot(xn, win_ref[...], preferred_element_type=F32)
    qn = _rms(lat[:, 0:Q_LORA], qn_ref[...]).astype(BF16)
    kvn = _rms(lat[:, Q_LORA:Q_LORA + KV_LORA], kvn_ref[...]).astype(BF16)
    c = cos_ref[...]
    s = sin_ref[...]
    k_pe = _rope_group(lat[:, Q_LORA + KV_LORA:], c, s).astype(BF16)
    extra = lax.broadcasted_iota(jnp.int32, (VT_ROWS - V_HEAD, tm), 0)
    ones_rows = jnp.where(extra == 0, 1.0, 0.0).astype(BF16)
    for h in range(MLA_HEADS):
        qh = jnp.dot(qn, wqb_ref[:, h * hp:(h + 1) * hp], preferred_element_type=F32)
        q_ref[:, h * hp:h * hp + QK_NOPE] = (qh[:, 0:QK_NOPE] * scale).astype(BF16)
        q_ref[:, h * hp + QK_NOPE:(h + 1) * hp] = (_rope_group(qh[:, QK_NOPE:], c, s) * scale).astype(BF16)
        kh = jnp.dot(kvn, wk_ref[:, h * QK_NOPE:(h + 1) * QK_NOPE], preferred_element_type=F32)
        k_ref[:, h * hp:h * hp + QK_NOPE] = kh.astype(BF16)
        k_ref[:, h * hp + QK_NOPE:(h + 1) * hp] = k_pe
        vt = lax.dot_general(wvt_ref[h * V_HEAD:(h + 1) * V_HEAD, :], kvn, (((1,), (1,)), ((), ())),
                             preferred_element_type=F32)
        vt_ref[0, h, 0:V_HEAD, :] = vt.astype(BF16)
        vt_ref[0, h, V_HEAD:VT_ROWS, :] = ones_rows


def _mla_in(x, g, w_in, q_norm, w_qb, kv_norm, w_k, w_vt, cos, sin, seq):
    n, d = x.shape
    tm = _tile(seq, TOKEN_TILE)
    tps = seq // tm
    hq = MLA_HEADS * MLA_HEAD_PAD
    const = lambda i: (0, 0)
    return pl.pallas_call(
        _mla_in_kernel,
        grid=(n // tm,),
        in_specs=[pl.BlockSpec((tm, d), lambda i: (i, 0)),
                  pl.BlockSpec((1, d), const),
                  pl.BlockSpec(w_in.shape, const),
                  pl.BlockSpec((1, Q_LORA), const),
                  pl.BlockSpec((1, KV_LORA), const),
                  pl.BlockSpec(w_qb.shape, const),
                  pl.BlockSpec(w_k.shape, const),
                  pl.BlockSpec(w_vt.shape, const),
                  pl.BlockSpec((tm, LANES_V7X), lambda i: (i % tps, 0)),
                  pl.BlockSpec((tm, LANES_V7X), lambda i: (i % tps, 0))],
        out_specs=[pl.BlockSpec((tm, hq), lambda i: (i, 0)),
                   pl.BlockSpec((tm, hq), lambda i: (i, 0)),
                   pl.BlockSpec((1, MLA_HEADS, VT_ROWS, tm), lambda i: (i, 0, 0, 0))],
        out_shape=[jax.ShapeDtypeStruct((n, hq), BF16),
                   jax.ShapeDtypeStruct((n, hq), BF16),
                   jax.ShapeDtypeStruct((n // tm, MLA_HEADS, VT_ROWS, tm), BF16)],
        compiler_params=_params("parallel"),
        name="mla_in",
    )(x, g, w_in, q_norm, kv_norm, w_qb, w_k, w_vt, cos, sin)


def _attn_kernel(q_ref, k_ref, vt_ref, o_ref, s0, s1, p0, p1, acc_ref):
    tq = q_ref.shape[0]
    nk, _, _, tk = vt_ref.shape
    assert nk % 2 == 0
    q = q_ref[...]

    def scores(j, s_ref):
        start = pl.multiple_of(j * tk, tk)
        k = k_ref[pl.ds(start, tk), :]
        st = lax.dot_general(k, q, (((1,), (1,)), ((), ())), preferred_element_type=F32)
        s_ref[...] = st
        return jnp.max(st, axis=0, keepdims=True)

    def exps(s_ref, p_ref, m, tile_max):
        m_new = jnp.maximum(m, tile_max)
        p_ref[...] = jnp.exp2(s_ref[...] - m_new).astype(BF16)
        return m_new, jnp.exp2(m - m_new)

    def values(j, p_ref, alpha):
        acc_ref[...] = alpha * acc_ref[...] + jnp.dot(vt_ref[j, 0], p_ref[...], preferred_element_type=F32)

    acc_ref[...] = jnp.zeros_like(acc_ref)
    x0 = scores(0, s0)
    x1 = scores(1, s1)
    m, alpha = exps(s0, p0, jnp.full((1, tq), -jnp.inf, F32), x0)

    steps = nk - 2
    unroll = max(u for u in (2, 4, 6, 8, 10) if steps % u == 0) if steps else 2

    def group(i, carry):
        m, alpha, x1 = carry
        for t in range(0, unroll, 2):
            s = unroll * i + t + 1
            x0 = scores(s + 1, s0)
            values(s - 1, p0, alpha)
            m, alpha = exps(s1, p1, m, x1)
            x1 = scores(s + 2, s1)
            values(s, p1, alpha)
            m, alpha = exps(s0, p0, m, x0)
        return m, alpha, x1

    m, alpha, x1 = lax.fori_loop(0, steps // unroll, group, (m, alpha, x1))
    values(nk - 2, p0, alpha)
    m, alpha = exps(s1, p1, m, x1)
    values(nk - 1, p1, alpha)
    acc = acc_ref[...]
    out_t = acc[0:V_HEAD, :] / acc[V_HEAD:V_HEAD + 1, :]
    o_ref[...] = out_t.T.astype(BF16)


def _attention(q, k, vt, batch, seq):
    n = q.shape[0]
    tq = _tile(seq, ATTN_Q_TILE)
    tk = vt.shape[-1]
    nq = seq // tq
    nk = seq // tk
    hp = MLA_HEAD_PAD
    return pl.pallas_call(
        _attn_kernel,
        grid=(batch, MLA_HEADS, nq),
        in_specs=[pl.BlockSpec((tq, hp), lambda b, h, i: (b * nq + i, h)),
                  pl.BlockSpec((seq, hp), lambda b, h, i: (b, h)),
                  pl.BlockSpec((nk, 1, VT_ROWS, tk), lambda b, h, i: (b, h, 0, 0))],
        out_specs=pl.BlockSpec((tq, V_HEAD), lambda b, h, i: (b * nq + i, h)),
        out_shape=jax.ShapeDtypeStruct((n, MLA_HEADS * V_HEAD), BF16),
        scratch_shapes=[pltpu.VMEM((tk, tq), F32), pltpu.VMEM((tk, tq), F32),
                        pltpu.VMEM((tk, tq), BF16), pltpu.VMEM((tk, tq), BF16),
                        pltpu.VMEM((VT_ROWS, tq), F32)],
        compiler_params=_params("parallel", "parallel", "arbitrary"),
        name="mla_attention",
    )(q, k, vt)


def _proj_residual_kernel(a_ref, h_ref, w_ref, o_ref):
    o_ref[...] = h_ref[...] + jnp.dot(a_ref[...], w_ref[...], preferred_element_type=F32)


def _proj_residual(a, h, w):
    n, d = h.shape
    tm = _tile(n, TOKEN_TILE)
    return pl.pallas_call(
        _proj_residual_kernel,
        grid=(n // tm,),
        in_specs=[pl.BlockSpec((tm, a.shape[1]), lambda i: (i, 0)),
                  pl.BlockSpec((tm, d), lambda i: (i, 0)),
                  pl.BlockSpec(w.shape, lambda i: (0, 0))],
        out_specs=pl.BlockSpec((tm, d), lambda i: (i, 0)),
        out_shape=jax.ShapeDtypeStruct((n, d), F32),
        compiler_params=_params("parallel"),
        name="proj_residual",
    )(a, h, w)


def _ml_in_kernel(x_ref, g_ref, w_ref, b_ref, q_ref, k_ref, v_ref, o_ref, gt_ref):
    qw = ML_HEADS * ML_QK_PAD
    vw = ML_HEADS * ML_V
    xn = _rms(x_ref[...], g_ref[...]).astype(BF16)
    q = jnp.dot(xn, w_ref[:, 0:qw], preferred_element_type=F32)
    q_ref[...] = (q * (ML_QK ** -0.5)).astype(BF16)
    k_ref[...] = jnp.dot(xn, w_ref[:, qw:2 * qw], preferred_element_type=F32).astype(BF16)
    v_ref[...] = jnp.dot(xn, w_ref[:, 2 * qw:2 * qw + vw], preferred_element_type=F32).astype(BF16)
    o_ref[...] = jnp.dot(xn, w_ref[:, 2 * qw + vw:2 * qw + 2 * vw], preferred_element_type=F32).astype(BF16)
    gt_ref[...] = jnp.dot(xn, w_ref[:, 2 * qw + 2 * vw:], preferred_element_type=F32) + b_ref[...]


def _ml_in(x, g, w, b_gates):
    n, d = x.shape
    tm = _tile(n, TOKEN_TILE)
    qw = ML_HEADS * ML_QK_PAD
    vw = ML_HEADS * ML_V
    ng = 4 * ML_HEADS
    row = lambda i: (i, 0)
    const = lambda i: (0, 0)
    return pl.pallas_call(
        _ml_in_kernel,
        grid=(n // tm,),
        in_specs=[pl.BlockSpec((tm, d), row), pl.BlockSpec((1, d), const),
                  pl.BlockSpec(w.shape, const), pl.BlockSpec((1, ng), const)],
        out_specs=[pl.BlockSpec((tm, qw), row), pl.BlockSpec((tm, qw), row),
                   pl.BlockSpec((tm, vw), row), pl.BlockSpec((tm, vw), row),
                   pl.BlockSpec((tm, ng), row)],
        out_shape=[jax.ShapeDtypeStruct((n, qw), BF16), jax.ShapeDtypeStruct((n, qw), BF16),
                   jax.ShapeDtypeStruct((n, vw), BF16), jax.ShapeDtypeStruct((n, vw), BF16),
                   jax.ShapeDtypeStruct((n, ng), F32)],
        compiler_params=_params("parallel"),
        name="mlstm_in",
    )(x, g, w, b_gates)


def _split3(x):
    hi = x.astype(BF16)
    r1 = x - hi.astype(F32)
    mid = r1.astype(BF16)
    lo = (r1 - mid.astype(F32)).astype(BF16)
    return hi, mid, lo


def _mlstm_kernel(q_ref, kt_ref, v_ref, gc_ref, gr_ref, o_ref, c_scr, m_scr):
    lc = q_ref.shape[0]
    nh = ML_HEADS
    kp = ML_QK_PAD
    dv = ML_V
    fwd = pl.program_id(0) == 0

    @pl.when(pl.program_id(2) == 0)
    def _():
        c_scr[...] = jnp.zeros_like(c_scr)
        m_scr[...] = jnp.zeros_like(m_scr)

    gcol = gc_ref[0]
    grow = gr_ref[0]
    lf_col = jax.nn.log_sigmoid(gcol[:, nh:2 * nh])
    li_row = grow[0:nh, :]
    lf_row = jax.nn.log_sigmoid(grow[nh:2 * nh, :])

    r = lax.broadcasted_iota(jnp.int32, (lc, lc), 0)
    cidx = lax.broadcasted_iota(jnp.int32, (lc, lc), 1)
    allowed = (r - cidx) * jnp.where(fwd, 1, -1) >= 0
    tri = jnp.where(allowed, 1.0, 0.0).astype(BF16)

    b_col = jnp.zeros((lc, nh), F32)
    for piece in _split3(lf_col):
        b_col = b_col + jnp.dot(tri, piece, preferred_element_type=F32)
    b_row = jnp.zeros((nh, lc), F32)
    for piece in _split3(lf_row):
        b_row = b_row + lax.dot_general(piece, tri, (((1,), (1,)), ((), ())), preferred_element_type=F32)
    a_all = jnp.sum(lf_row, axis=-1, keepdims=True)

    lane = lax.broadcasted_iota(jnp.int32, (lc, LANES_V7X), 1)
    ones_col = jnp.where(lane == 0, 1.0, 0.0).astype(BF16)

    for h in range(nh):
        q = q_ref[:, h * kp:(h + 1) * kp]
        kt = kt_ref[h * kp:(h + 1) * kp, :]
        v_ext = jnp.concatenate([v_ref[:, h * dv:(h + 1) * dv], ones_col], axis=1)
        b_c = b_col[:, h:h + 1]
        b_r = b_row[h:h + 1, :]
        li_r = li_row[h:h + 1, :]
        a = a_all[h:h + 1, :]
        m_in = m_scr[h][0:1, 0:1]
        c_ext = c_scr[h]

        w_end = a - b_r + li_r
        g = jnp.max(w_end, axis=-1, keepdims=True)
        e_r = jnp.exp(w_end - g)

        dmat = jnp.where(allowed, b_c - b_r + li_r, -jnp.inf)
        inter_log = b_c + m_in
        m_j = jnp.maximum(inter_log, jnp.max(dmat, axis=-1, keepdims=True))
        inter = jnp.exp(inter_log - m_j)
        p = jnp.exp(dmat - m_j)
        s = jnp.dot(q, kt, preferred_element_type=F32)
        qk = (s * p).astype(BF16)
        nd = (jnp.dot(qk, v_ext, preferred_element_type=F32)
              + inter * jnp.dot(q, c_ext.astype(BF16), preferred_element_type=F32))
        den = nd[:, dv:dv + 1]
        o_ref[0, :, h * dv:(h + 1) * dv] = nd[:, 0:dv] / jnp.maximum(jnp.abs(den), jnp.exp(-m_j))

        m_new = jnp.maximum(a + m_in, g)
        fdec = jnp.exp(a + m_in - m_new)
        iin = jnp.exp(g - m_new)
        kte = (kt.astype(F32) * e_r).astype(BF16)
        c_scr[h] = fdec * c_ext + iin * jnp.dot(kte, v_ext, preferred_element_type=F32)
        m_scr[h] = jnp.broadcast_to(m_new, m_scr.shape[1:])


def _mlstm(q, kt, v, gcol, grow, batch, seq):
    n = q.shape[0]
    lc = _tile(seq, ML_CHUNK_TILE)
    nc = seq // lc
    qw = ML_HEADS * ML_QK_PAD
    vw = ML_HEADS * ML_V

    def chunk(d, b, c):
        return b * nc + c + d * (nc - 1 - 2 * c)

    return pl.pallas_call(
        _mlstm_kernel,
        grid=(2, batch, nc),
        in_specs=[pl.BlockSpec((lc, qw), lambda d, b, c: (chunk(d, b, c), 0)),
                  pl.BlockSpec((qw, lc), lambda d, b, c: (0, chunk(d, b, c))),
                  pl.BlockSpec((lc, vw), lambda d, b, c: (chunk(d, b, c), 0)),
                  pl.BlockSpec((1, lc, 2 * ML_HEADS), lambda d, b, c: (d, chunk(d, b, c), 0)),
                  pl.BlockSpec((1, 2 * ML_HEADS, lc), lambda d, b, c: (d, 0, chunk(d, b, c)))],
        out_specs=pl.BlockSpec((1, lc, vw), lambda d, b, c: (d, chunk(d, b, c), 0)),
        out_shape=jax.ShapeDtypeStruct((2, n, vw), F32),
        scratch_shapes=[pltpu.VMEM((ML_HEADS, ML_QK_PAD, 2 * ML_V), F32),
                        pltpu.VMEM((ML_HEADS, 8, LANES_V7X), F32)],
        compiler_params=_params("parallel", "parallel", "arbitrary"),
        name="mlstm_scan",
    )(q, kt, v, gcol, grow)


def _ml_out_kernel(hs_ref, og_ref, hn_ref, h_ref, w_ref, o_ref):
    dv = ML_V
    hs = hs_ref[0] + hs_ref[1]
    parts = []
    for hd in range(ML_HEADS):
        x = hs[:, hd * dv:(hd + 1) * dv]
        parts.append(x * lax.rsqrt(jnp.mean(x * x, axis=-1, keepdims=True) + EPS))
    y = jnp.concatenate(parts, axis=1) * hn_ref[...] * jax.nn.sigmoid(og_ref[...].astype(F32))
    o_ref[...] = h_ref[...] + jnp.dot(y.astype(BF16), w_ref[...], preferred_element_type=F32)


def _ml_out(hs, og, head_norm, h, w):
    n, d = h.shape
    vw = ML_HEADS * ML_V
    tm = _tile(n, TOKEN_TILE)
    row = lambda i: (i, 0)
    const = lambda i: (0, 0)
    return pl.pallas_call(
        _ml_out_kernel,
        grid=(n // tm,),
        in_specs=[pl.BlockSpec((2, tm, vw), lambda i: (0, i, 0)),
                  pl.BlockSpec((tm, vw), row), pl.BlockSpec((1, vw), const),
                  pl.BlockSpec((tm, d), row), pl.BlockSpec(w.shape, const)],
        out_specs=pl.BlockSpec((tm, d), row),
        out_shape=jax.ShapeDtypeStruct((n, d), F32),
        compiler_params=_params("parallel"),
        name="mlstm_out",
    )(hs, og, head_norm, h, w)


def _router_kernel(x_ref, g_ref, w_ref, wlo_ref, xn_ref, aff_ref):
    xf = _rms(x_ref[...], g_ref[...])
    xn = xf.astype(BF16)
    xn_ref[...] = xn
    xlo = (xf - xn.astype(F32)).astype(BF16)
    logits = (jnp.dot(xn, w_ref[...], preferred_element_type=F32)
              + jnp.dot(xlo, w_ref[...], preferred_element_type=F32)
              + jnp.dot(xn, wlo_ref[...], preferred_element_type=F32))
    z = jnp.exp(logits - jnp.max(logits, axis=-1, keepdims=True))
    aff_ref[...] = z / jnp.sum(z, axis=-1, keepdims=True)


def _router(x, g, w_f32):
    n, d = x.shape
    tm = _tile(n, TOKEN_TILE)
    row = lambda i: (i, 0)
    const = lambda i: (0, 0)
    w = w_f32.astype(BF16)
    w_lo = (w_f32 - w.astype(F32)).astype(BF16)
    return pl.pallas_call(
        _router_kernel,
        grid=(n // tm,),
        in_specs=[pl.BlockSpec((tm, d), row), pl.BlockSpec((1, d), const),
                  pl.BlockSpec(w.shape, const), pl.BlockSpec(w.shape, const)],
        out_specs=[pl.BlockSpec((tm, d), row), pl.BlockSpec((tm, N_EXPERTS), row)],
        out_shape=[jax.ShapeDtypeStruct((n, d), BF16), jax.ShapeDtypeStruct((n, N_EXPERTS), F32)],
        compiler_params=_params("parallel"),
        name="moe_router",
    )(x, g, w, w_lo)


def _ffn_kernel(x_ref, gate_ref, wg_ref, wu_ref, wd_ref, o_ref):
    x = x_ref[0]
    g = jnp.dot(x, wg_ref[0], preferred_element_type=F32)
    u = jnp.dot(x, wu_ref[0], preferred_element_type=F32)
    hid = (g * jax.nn.sigmoid(g) * u).astype(BF16)
    y = jnp.dot(hid, wd_ref[0], preferred_element_type=F32)
    o_ref[0] = (y * gate_ref[0]).astype(BF16)


def _expert_ffn(xe, gate, wg, wu, wd):
    e, cap, d = xe.shape
    f = wg.shape[-1]
    tm = _tile(cap, TOKEN_TILE)
    tok = lambda i, j: (i, j, 0)
    wmap = lambda i, j: (i, 0, 0)
    return pl.pallas_call(
        _ffn_kernel,
        grid=(e, cap // tm),
        in_specs=[pl.BlockSpec((1, tm, d), tok), pl.BlockSpec((1, tm, 1), tok),
                  pl.BlockSpec((1, d, f), wmap), pl.BlockSpec((1, d, f), wmap),
                  pl.BlockSpec((1, f, d), wmap)],
        out_specs=pl.BlockSpec((1, tm, d), tok),
        out_shape=jax.ShapeDtypeStruct((e, cap, d), BF16),
        compiler_params=_params("parallel", "arbitrary"),
        name="moe_ffn",
    )(xe, gate, wg, wu, wd)


def _combine_kernel(lo_ref, h_ref, idx_ref, ye_hbm, o_ref, buf, sem):
    tb, d = h_ref.shape
    ne, nrow, lanes = idx_ref.shape
    cap = nrow * lanes
    win = COMBINE_WINDOW
    b = pl.program_id(0)
    nb = pl.num_programs(0)
    tok = b * tb + lax.broadcasted_iota(jnp.int32, (tb, 1), 0)
    lane = lax.broadcasted_iota(jnp.int32, (1, win), 1)

    def first_start(blk, e):
        return jnp.minimum(lo_ref[e, blk] // lanes * lanes, cap - win)

    def fetch(e, start, slot):
        return pltpu.make_async_copy(ye_hbm.at[e, pl.ds(pl.multiple_of(start, lanes), win), :],
                                     buf.at[slot], sem.at[slot])

    def expand(e, start, row_lo, row_hi, slot):
        r0 = start // lanes
        ids = jnp.concatenate([idx_ref[e, pl.ds(r0 + k, 1), :] for k in range(win // lanes)], axis=1)
        rid = start + lane
        ids = jnp.where((rid >= row_lo) & (rid < row_hi), ids, -1)
        sel = jnp.where(tok == ids, 1.0, 0.0).astype(BF16)
        return jnp.dot(sel, buf[slot], preferred_element_type=F32)

    @pl.when(b == 0)
    def _():
        fetch(0, first_start(0, 0), 0).start()

    o_ref[...] = h_ref[...]
    for e in range(ne):
        slot = e % 2
        lo = lo_ref[e, b]
        hi = lo_ref[e, b + 1]
        start0 = lo // lanes * lanes
        start = jnp.minimum(start0, cap - win)
        if e + 1 < ne:
            fetch(e + 1, first_start(b, e + 1), 1 - slot).start()
        fetch(e, start, slot).wait()
        o_ref[...] += expand(e, start, lo, hi, slot)

        if e + 1 == ne:
            @pl.when(b + 1 < nb)
            def _():
                fetch(0, first_start(b + 1, 0), 0).start()

        def more(w, carry, e=e, slot=slot, lo=lo, hi=hi, start0=start0):
            nominal = start0 + w * win
            st = jnp.minimum(nominal, cap - win)
            cp = fetch(e, st, slot)
            cp.start()
            cp.wait()
            o_ref[...] += expand(e, st, jnp.maximum(lo, nominal), hi, slot)
            return carry

        nwin = (hi - start0 + win - 1) // win
        lax.fori_loop(1, nwin, more, 0)


def _combine(h, idx, ye):
    n, d = h.shape
    e, cap = idx.shape
    assert cap >= COMBINE_WINDOW and cap % LANES_V7X == 0 and e % 2 == 0
    tb = _tile(n, TOKEN_TILE)
    nb = n // tb
    bounds = jnp.arange(nb + 1, dtype=jnp.int32) * tb
    lo = jnp.sum(idx[:, :, None] < bounds[None, None, :], axis=1, dtype=jnp.int32)
    idx3 = idx.reshape(e, cap // LANES_V7X, LANES_V7X)
    grid_spec = pltpu.PrefetchScalarGridSpec(
        num_scalar_prefetch=1,
        grid=(nb,),
        in_specs=[pl.BlockSpec((tb, d), lambda i, lo_ref: (i, 0)),
                  pl.BlockSpec(idx3.shape, lambda i, lo_ref: (0, 0, 0)),
                  pl.BlockSpec(memory_space=pl.ANY)],
        out_specs=pl.BlockSpec((tb, d), lambda i, lo_ref: (i, 0)),
        scratch_shapes=[pltpu.VMEM((2, COMBINE_WINDOW, d), BF16), pltpu.SemaphoreType.DMA((2,))])
    return pl.pallas_call(
        _combine_kernel,
        grid_spec=grid_spec,
        out_shape=jax.ShapeDtypeStruct((n, d), F32),
        compiler_params=_params("arbitrary"),
        name="moe_combine",
    )(lo, h, idx3, ye)


def _moe(h, g, w_router, wg, wu, wd):
    n, d = h.shape
    cap = EC_CAPACITY * n // N_EXPERTS
    xn, aff = _router(h, g, w_router)
    gate, idx = lax.top_k(aff.T, cap)
    idx, gate = lax.sort((idx, gate), dimension=1, num_keys=1)
    xe = xn[idx]
    ye = _expert_ffn(xe, gate[..., None], wg, wu, wd)
    return _combine(h, idx, ye)


def _final_norm_kernel(x_ref, g_ref, o_ref):
    o_ref[...] = _rms(x_ref[...], g_ref[...])


def _final_norm(x, g):
    n, d = x.shape
    tm = _tile(n, TOKEN_TILE)
    return pl.pallas_call(
        _final_norm_kernel,
        grid=(n // tm,),
        in_specs=[pl.BlockSpec((tm, d), lambda i: (i, 0)), pl.BlockSpec((1, d), lambda i: (0, 0))],
        out_specs=pl.BlockSpec((tm, d), lambda i: (i, 0)),
        out_shape=jax.ShapeDtypeStruct((n, d), F32),
        compiler_params=_params("parallel"),
        name="final_norm",
    )(x, g)


def _rope_group_cols(w_pe):
    half = QK_ROPE // 2
    z = jnp.zeros((w_pe.shape[0], LANES_V7X // 2 - half), w_pe.dtype)
    return jnp.concatenate([w_pe[:, :half], z, w_pe[:, half:], z], axis=1)


def _rope_tables(seq):
    half = QK_ROPE // 2
    pos = jnp.arange(seq, dtype=F32)
    inv = ROPE_THETA ** (-jnp.arange(0, QK_ROPE, 2, dtype=F32) / QK_ROPE)
    ang = pos[:, None] * inv[None, :]
    c, s = jnp.cos(ang), jnp.sin(ang)
    z = jnp.zeros((seq, LANES_V7X // 2 - half), F32)
    return jnp.concatenate([c, z, c, z], axis=1), jnp.concatenate([-s, z, s, z], axis=1)


def _prep_mla(w_in, w_qb, w_kvb):
    lat = Q_LORA + KV_LORA
    w_in_p = jnp.concatenate([w_in[:, :lat], _rope_group_cols(w_in[:, lat:])], axis=1).astype(BF16)
    hd = QK_NOPE + QK_ROPE
    cols = []
    for h in range(MLA_HEADS):
        cols.append(w_qb[:, h * hd:h * hd + QK_NOPE])
        cols.append(_rope_group_cols(w_qb[:, h * hd + QK_NOPE:(h + 1) * hd]))
    w_kv = w_kvb.reshape(KV_LORA, MLA_HEADS, QK_NOPE + V_HEAD)
    w_k = w_kv[:, :, :QK_NOPE].reshape(KV_LORA, MLA_HEADS * QK_NOPE).astype(BF16)
    w_vt = w_kv[:, :, QK_NOPE:].reshape(KV_LORA, MLA_HEADS * V_HEAD).T.astype(BF16)
    return w_in_p, jnp.concatenate(cols, axis=1).astype(BF16), w_k, w_vt


def _prep_ml_in(w):
    qd = ML_HEADS * ML_QK
    cols = []
    for base in (0, qd):
        for h in range(ML_HEADS):
            cols.append(w[:, base + h * ML_QK:base + (h + 1) * ML_QK])
            cols.append(jnp.zeros((w.shape[0], ML_QK_PAD - ML_QK), w.dtype))
    cols.append(w[:, 2 * qd:])
    return jnp.concatenate(cols, axis=1).astype(BF16)


def _trunk(x, p):
    batch, seq, d = x.shape
    n = batch * seq
    h = x.reshape(n, d)
    for i in range(DEPTH):
        j = i // N_MIXERS
        kind = i % N_MIXERS
        g_mix = p["norm_mix"][i][None, :]
        if kind == 0:
            u, gb = _conv_in(h, g_mix, p["conv_w_in"][j])
            h = _conv_out(u, gb, h, p["conv_w_dw"][j], p["conv_w_out"][j], seq)
        elif kind == 1:
            cos, sin = _rope_tables(seq)
            q, k, vt = _mla_in(h, g_mix, p["mla_w_in"][j], p["mla_q_norm"][j][None, :], p["mla_w_qb"][j],
                               p["mla_kv_norm"][j][None, :], p["mla_w_k"][j], p["mla_w_vt"][j], cos, sin, seq)
            o = _attention(q, k, vt, batch, seq)
            h = _proj_residual(o, h, p["mla_w_out"][j])
        else:
            q, k, v, og, gates = _ml_in(h, g_mix, p["ml_w_in"][j], p["ml_b_gates"][j][None, :])
            gcol = gates.reshape(n, 2, 2 * ML_HEADS).transpose(1, 0, 2)
            grow = gcol.transpose(0, 2, 1)
            hs = _mlstm(q, k.T, v, gcol, grow, batch, seq)
            h = _ml_out(hs, og, p["ml_head_norm"][j][None, :], h, p["ml_w_out"][j])
        h = _moe(h, p["norm_ffn"][i][None, :], p["router_w"][i],
                 p["exp_w_gate"][i], p["exp_w_up"][i], p["exp_w_down"][i])
    return _final_norm(h, p["norm_final"][None, :]).reshape(batch, seq, d)


def kernel(x_prompt, x_sample, conv_w_in, conv_w_dw, conv_w_out, mla_w_in, mla_q_norm, mla_w_qb, mla_kv_norm, mla_w_kvb, mla_w_out, ml_w_in, ml_b_gates, ml_head_norm, ml_w_out, norm_mix, norm_ffn, router_w, exp_w_gate, exp_w_up, exp_w_down, norm_final):
    mla = [_prep_mla(mla_w_in[j], mla_w_qb[j], mla_w_kvb[j]) for j in range(mla_w_in.shape[0])]
    p = dict(
        conv_w_in=conv_w_in.astype(BF16), conv_w_dw=conv_w_dw, conv_w_out=conv_w_out.astype(BF16),
        mla_w_in=[m[0] for m in mla], mla_q_norm=mla_q_norm, mla_w_qb=[m[1] for m in mla],
        mla_kv_norm=mla_kv_norm, mla_w_k=[m[2] for m in mla], mla_w_vt=[m[3] for m in mla],
        mla_w_out=mla_w_out.astype(BF16),
        ml_w_in=[_prep_ml_in(ml_w_in[j]) for j in range(ml_w_in.shape[0])], ml_b_gates=ml_b_gates,
        ml_head_norm=ml_head_norm, ml_w_out=ml_w_out.astype(BF16),
        norm_mix=norm_mix, norm_ffn=norm_ffn, router_w=router_w,
        exp_w_gate=exp_w_gate.astype(BF16), exp_w_up=exp_w_up.astype(BF16),
        exp_w_down=exp_w_down.astype(BF16), norm_final=norm_final)
    return (_trunk(x_prompt, p), _trunk(x_sample, p))
```

```python
import functools

import jax
import jax.numpy as jnp
from jax import lax
from jax.experimental import pallas as pl
from jax.experimental.pallas import tpu as pltpu

F32 = jnp.float32
BF16 = jnp.bfloat16

D_MODEL = 1024
DEPTH = 4
N_MIXERS = 3
EPS = 1e-6
CONV_WIDTH = 3
MLA_HEADS = 8
Q_LORA = 384
KV_LORA = 256
QK_NOPE = 128
QK_ROPE = 64
V_HEAD = 128
ROPE_THETA = 10000.0
ML_HEADS = 8
ML_QK = 64
ML_V = 128
N_EXPERTS = 16
EC_CAPACITY = 2
D_EXPERT = 1024

LANES_V7X = 128
BF16_SUBLANES_V7X = 16
VMEM_BYTES_V7X = 64 * 1024 * 1024
VMEM_LIMIT_BYTES = VMEM_BYTES_V7X - 8 * 1024 * 1024

TOKEN_TILE = 512
ATTN_Q_TILE = 1024
ML_CHUNK_TILE = 256
COMBINE_WINDOW = 256
GATHER_ROWS = 32768

MLA_HEAD_PAD = 2 * LANES_V7X
VT_ROWS = V_HEAD + BF16_SUBLANES_V7X
LOG2_E = 1.4426950408889634
ML_QK_PAD = LANES_V7X


def _params(*sem):
    return pltpu.CompilerParams(dimension_semantics=sem, vmem_limit_bytes=VMEM_LIMIT_BYTES)


def _rms(x, g):
    ms = jnp.mean(x * x, axis=-1, keepdims=True)
    return x * lax.rsqrt(ms + EPS) * g


def _tile(n, pref):
    t = min(n, pref)
    assert n % t == 0, (n, t)
    return t


def _conv_in_kernel(x_ref, g_ref, w_ref, u_ref, gb_ref):
    d = D_MODEL
    xn = _rms(x_ref[...], g_ref[...]).astype(BF16)
    gb = jnp.dot(xn, w_ref[:, 0:d], preferred_element_type=F32)
    gc = jnp.dot(xn, w_ref[:, d:2 * d], preferred_element_type=F32)
    xv = jnp.dot(xn, w_ref[:, 2 * d:3 * d], preferred_element_type=F32)
    gb_ref[...] = gb.astype(BF16)
    u_ref[...] = (gc * xv).astype(BF16)


def _conv_in(x, g, w):
    n, d = x.shape
    tm = _tile(n, TOKEN_TILE)
    return pl.pallas_call(
        _conv_in_kernel,
        grid=(n // tm,),
        in_specs=[pl.BlockSpec((tm, d), lambda i: (i, 0)),
                  pl.BlockSpec((1, d), lambda i: (0, 0)),
                  pl.BlockSpec((d, 3 * d), lambda i: (0, 0))],
        out_specs=[pl.BlockSpec((tm, d), lambda i: (i, 0)),
                   pl.BlockSpec((tm, d), lambda i: (i, 0))],
        out_shape=[jax.ShapeDtypeStruct((n, d), BF16), jax.ShapeDtypeStruct((n, d), BF16)],
        compiler_params=_params("parallel"),
        name="conv_in",
    )(x, g, w)


def _conv_out_kernel(u_ref, up_ref, un_ref, gb_ref, h_ref, wdw_ref, w_ref, o_ref, *, tiles_per_seq):
    tm = u_ref.shape[0]
    pos = pl.program_id(0) % tiles_per_seq
    u = u_ref[...].astype(F32)
    halo = BF16_SUBLANES_V7X
    prev_row = jnp.where(pos == 0, 0.0, up_ref[halo - 1:halo, :].astype(F32))
    next_row = jnp.where(pos == tiles_per_seq - 1, 0.0, un_ref[0:1, :].astype(F32))
    row = lax.broadcasted_iota(jnp.int32, (tm, 1), 0)
    u_up = jnp.where(row == 0, prev_row, pltpu.roll(u, 1, axis=0))
    u_dn = jnp.where(row == tm - 1, next_row, pltpu.roll(u, tm - 1, axis=0))
    conv = u_up * wdw_ref[0:1, :] + u * wdw_ref[1:2, :] + u_dn * wdw_ref[2:3, :]
    g = (gb_ref[...].astype(F32) * conv).astype(BF16)
    o_ref[...] = h_ref[...] + jnp.dot(g, w_ref[...], preferred_element_type=F32)


def _conv_out(u, gb, h, w_dw, w_out, seq):
    n, d = u.shape
    tm = _tile(seq, TOKEN_TILE)
    halo = BF16_SUBLANES_V7X
    r = tm // halo
    nblk = n // halo
    return pl.pallas_call(
        functools.partial(_conv_out_kernel, tiles_per_seq=seq // tm),
        grid=(n // tm,),
        in_specs=[pl.BlockSpec((tm, d), lambda i: (i, 0)),
                  pl.BlockSpec((halo, d), lambda i: (jnp.maximum(i * r - 1, 0), 0)),
                  pl.BlockSpec((halo, d), lambda i: (jnp.minimum((i + 1) * r, nblk - 1), 0)),
                  pl.BlockSpec((tm, d), lambda i: (i, 0)),
                  pl.BlockSpec((tm, d), lambda i: (i, 0)),
                  pl.BlockSpec((CONV_WIDTH, d), lambda i: (0, 0)),
                  pl.BlockSpec((d, d), lambda i: (0, 0))],
        out_specs=pl.BlockSpec((tm, d), lambda i: (i, 0)),
        out_shape=jax.ShapeDtypeStruct((n, d), F32),
        compiler_params=_params("parallel"),
        name="conv_out",
    )(u, u, u, gb, h, w_dw, w_out)


def _rope_group(x, c, s):
    return x * c + pltpu.roll(x, LANES_V7X // 2, axis=1) * s


def _mla_in_kernel(x_ref, g_ref, win_ref, qn_ref, kvn_ref, wqb_ref, wk_ref, wvt_ref, cos_ref, sin_ref,
                   q_ref, k_ref, vt_ref):
    hp = MLA_HEAD_PAD
    tm = x_ref.shape[0]
    scale = (QK_NOPE + QK_ROPE) ** -0.5 * LOG2_E
    xn = _rms(x_ref[...], g_ref[...]).astype(BF16)
    lat = jnp.dot(xn, win_ref[...], preferred_element_type=F32)
    qn = _rms(lat[:, 0:Q_LORA], qn_ref[...]).astype(BF16)
    kvn = _rms(lat[:, Q_LORA:Q_LORA + KV_LORA], kvn_ref[...]).astype(BF16)
    c = cos_ref[...]
    s = sin_ref[...]
    k_pe = _rope_group(lat[:, Q_LORA + KV_LORA:], c, s).astype(BF16)
    extra = lax.broadcasted_iota(jnp.int32, (VT_ROWS - V_HEAD, tm), 0)
    ones_rows = jnp.where(extra == 0, 1.0, 0.0).astype(BF16)
    for h in range(MLA_HEADS):
        qh = jnp.dot(qn, wqb_ref[:, h * hp:(h + 1) * hp], preferred_element_type=F32)
        q_ref[:, h * hp:h * hp + QK_NOPE] = (qh[:, 0:QK_NOPE] * scale).astype(BF16)
        q_ref[:, h * hp + QK_NOPE:(h + 1) * hp] = (_rope_group(qh[:, QK_NOPE:], c, s) * scale).astype(BF16)
        kh = jnp.dot(kvn, wk_ref[:, h * QK_NOPE:(h + 1) * QK_NOPE], preferred_element_type=F32)
        k_ref[:, h * hp:h * hp + QK_NOPE] = kh.astype(BF16)
        k_ref[:, h * hp + QK_NOPE:(h + 1) * hp] = k_pe
        vt = lax.dot_general(wvt_ref[h * V_HEAD:(h + 1) * V_HEAD, :], kvn, (((1,), (1,)), ((), ())),
                             preferred_element_type=F32)
        vt_ref[0, h, 0:V_HEAD, :] = vt.astype(BF16)
        vt_ref[0, h, V_HEAD:VT_ROWS, :] = ones_rows


def _mla_in(x, g, w_in, q_norm, w_qb, kv_norm, w_k, w_vt, cos, sin, seq):
    n, d = x.shape
    tm = _tile(seq, TOKEN_TILE)
    tps = seq // tm
    hq = MLA_HEADS * MLA_HEAD_PAD
    const = lambda i: (0, 0)
    return pl.pallas_call(
        _mla_in_kernel,
        grid=(n // tm,),
        in_specs=[pl.BlockSpec((tm, d), lambda i: (i, 0)),
                  pl.BlockSpec((1, d), const),
                  pl.BlockSpec(w_in.shape, const),
                  pl.BlockSpec((1, Q_LORA), const),
                  pl.BlockSpec((1, KV_LORA), const),
                  pl.BlockSpec(w_qb.shape, const),
                  pl.BlockSpec(w_k.shape, const),
                  pl.BlockSpec(w_vt.shape, const),
                  pl.BlockSpec((tm, LANES_V7X), lambda i: (i % tps, 0)),
                  pl.BlockSpec((tm, LANES_V7X), lambda i: (i % tps, 0))],
        out_specs=[pl.BlockSpec((tm, hq), lambda i: (i, 0)),
                   pl.BlockSpec((tm, hq), lambda i: (i, 0)),
                   pl.BlockSpec((1, MLA_HEADS, VT_ROWS, tm), lambda i: (i, 0, 0, 0))],
        out_shape=[jax.ShapeDtypeStruct((n, hq), BF16),
                   jax.ShapeDtypeStruct((n, hq), BF16),
                   jax.ShapeDtypeStruct((n // tm, MLA_HEADS, VT_ROWS, tm), BF16)],
        compiler_params=_params("parallel"),
        name="mla_in",
    )(x, g, w_in, q_norm, kv_norm, w_qb, w_k, w_vt, cos, sin)


def _attn_kernel(q_ref, k_ref, vt_ref, o_ref, s0, s1, p0, p1, acc_ref):
    tq = q_ref.shape[0]
    nk, _, _, tk = vt_ref.shape
    assert nk % 2 == 0
    q = q_ref[...]

    def scores(j, s_ref):
        start = pl.multiple_of(j * tk, tk)
        k = k_ref[pl.ds(start, tk), :]
        st = lax.dot_general(k, q, (((1,), (1,)), ((), ())), preferred_element_type=F32)
        s_ref[...] = st
        return jnp.max(st, axis=0, keepdims=True)

    def exps(s_ref, p_ref, m, tile_max):
        m_new = jnp.maximum(m, tile_max)
        p_ref[...] = jnp.exp2(s_ref[...] - m_new).astype(BF16)
        return m_new, jnp.exp2(m - m_new)

    def values(j, p_ref, alpha):
        acc_ref[...] = alpha * acc_ref[...] + jnp.dot(vt_ref[j, 0], p_ref[...], preferred_element_type=F32)

    acc_ref[...] = jnp.zeros_like(acc_ref)
    x0 = scores(0, s0)
    x1 = scores(1, s1)
    m, alpha = exps(s0, p0, jnp.full((1, tq), -jnp.inf, F32), x0)

    steps = nk - 2
    unroll = max(u for u in (2, 4, 6, 8, 10) if steps % u == 0) if steps else 2

    def group(i, carry):
        m, alpha, x1 = carry
        for t in range(0, unroll, 2):
            s = unroll * i + t + 1
            x0 = scores(s + 1, s0)
            values(s - 1, p0, alpha)
            m, alpha = exps(s1, p1, m, x1)
            x1 = scores(s + 2, s1)
            values(s, p1, alpha)
            m, alpha = exps(s0, p0, m, x0)
        return m, alpha, x1

    m, alpha, x1 = lax.fori_loop(0, steps // unroll, group, (m, alpha, x1))
    values(nk - 2, p0, alpha)
    m, alpha = exps(s1, p1, m, x1)
    values(nk - 1, p1, alpha)
    acc = acc_ref[...]
    out_t = acc[0:V_HEAD, :] / acc[V_HEAD:V_HEAD + 1, :]
    o_ref[...] = out_t.T.astype(BF16)


def _attention(q, k, vt, batch, seq):
    n = q.shape[0]
    tq = _tile(seq, ATTN_Q_TILE)
    tk = vt.shape[-1]
    nq = seq // tq
    nk = seq // tk
    hp = MLA_HEAD_PAD
    return pl.pallas_call(
        _attn_kernel,
        grid=(batch, MLA_HEADS, nq),
        in_specs=[pl.BlockSpec((tq, hp), lambda b, h, i: (b * nq + i, h)),
                  pl.BlockSpec((seq, hp), lambda b, h, i: (b, h)),
                  pl.BlockSpec((nk, 1, VT_ROWS, tk), lambda b, h, i: (b, h, 0, 0))],
        out_specs=pl.BlockSpec((tq, V_HEAD), lambda b, h, i: (b * nq + i, h)),
        out_shape=jax.ShapeDtypeStruct((n, MLA_HEADS * V_HEAD), BF16),
        scratch_shapes=[pltpu.VMEM((tk, tq), F32), pltpu.VMEM((tk, tq), F32),
                        pltpu.VMEM((tk, tq), BF16), pltpu.VMEM((tk, tq), BF16),
                        pltpu.VMEM((VT_ROWS, tq), F32)],
        compiler_params=_params("parallel", "parallel", "arbitrary"),
        name="mla_attention",
    )(q, k, vt)


def _proj_residual_kernel(a_ref, h_ref, w_ref, o_ref):
    o_ref[...] = h_ref[...] + jnp.dot(a_ref[...], w_ref[...], preferred_element_type=F32)


def _proj_residual(a, h, w):
    n, d = h.shape
    tm = _tile(n, TOKEN_TILE)
    return pl.pallas_call(
        _proj_residual_kernel,
        grid=(n // tm,),
        in_specs=[pl.BlockSpec((tm, a.shape[1]), lambda i: (i, 0)),
                  pl.BlockSpec((tm, d), lambda i: (i, 0)),
                  pl.BlockSpec(w.shape, lambda i: (0, 0))],
        out_specs=pl.BlockSpec((tm, d), lambda i: (i, 0)),
        out_shape=jax.ShapeDtypeStruct((n, d), F32),
        compiler_params=_params("parallel"),
        name="proj_residual",
    )(a, h, w)


def _ml_in_kernel(x_ref, g_ref, w_ref, b_ref, q_ref, k_ref, v_ref, o_ref, gt_ref):
    qw = ML_HEADS * ML_QK_PAD
    vw = ML_HEADS * ML_V
    xn = _rms(x_ref[...], g_ref[...]).astype(BF16)
    q = jnp.dot(xn, w_ref[:, 0:qw], preferred_element_type=F32)
    q_ref[...] = (q * (ML_QK ** -0.5)).astype(BF16)
    k_ref[...] = jnp.dot(xn, w_ref[:, qw:2 * qw], preferred_element_type=F32).astype(BF16)
    v_ref[...] = jnp.dot(xn, w_ref[:, 2 * qw:2 * qw + vw], preferred_element_type=F32).astype(BF16)
    o_ref[...] = jnp.dot(xn, w_ref[:, 2 * qw + vw:2 * qw + 2 * vw], preferred_element_type=F32).astype(BF16)
    gt_ref[...] = jnp.dot(xn, w_ref[:, 2 * qw + 2 * vw:], preferred_element_type=F32) + b_ref[...]


def _ml_in(x, g, w, b_gates):
    n, d = x.shape
    tm = _tile(n, TOKEN_TILE)
    qw = ML_HEADS * ML_QK_PAD
    vw = ML_HEADS * ML_V
    ng = 4 * ML_HEADS
    row = lambda i: (i, 0)
    const = lambda i: (0, 0)
    return pl.pallas_call(
        _ml_in_kernel,
        grid=(n // tm,),
        in_specs=[pl.BlockSpec((tm, d), row), pl.BlockSpec((1, d), const),
                  pl.BlockSpec(w.shape, const), pl.BlockSpec((1, ng), const)],
        out_specs=[pl.BlockSpec((tm, qw), row), pl.BlockSpec((tm, qw), row),
                   pl.BlockSpec((tm, vw), row), pl.BlockSpec((tm, vw), row),
                   pl.BlockSpec((tm, ng), row)],
        out_shape=[jax.ShapeDtypeStruct((n, qw), BF16), jax.ShapeDtypeStruct((n, qw), BF16),
                   jax.ShapeDtypeStruct((n, vw), BF16), jax.ShapeDtypeStruct((n, vw), BF16),
                   jax.ShapeDtypeStruct((n, ng), F32)],
        compiler_params=_params("parallel"),
        name="mlstm_in",
    )(x, g, w, b_gates)


def _split3(x):
    hi = x.astype(BF16)
    r1 = x - hi.astype(F32)
    mid = r1.astype(BF16)
    lo = (r1 - mid.astype(F32)).astype(BF16)
    return hi, mid, lo


def _mlstm_kernel(q_ref, kt_ref, v_ref, gc_ref, gr_ref, o_ref, c_scr, m_scr):
    lc = q_ref.shape[0]
    nh = ML_HEADS
    kp = ML_QK_PAD
    dv = ML_V
    fwd = pl.program_id(0) == 0

    @pl.when(pl.program_id(2) == 0)
    def _():
        c_scr[...] = jnp.zeros_like(c_scr)
        m_scr[...] = jnp.zeros_like(m_scr)

    gcol = gc_ref[0]
    grow = gr_ref[0]
    lf_col = jax.nn.log_sigmoid(gcol[:, nh:2 * nh])
    li_row = grow[0:nh, :]
    lf_row = jax.nn.log_sigmoid(grow[nh:2 * nh, :])

    r = lax.broadcasted_iota(jnp.int32, (lc, lc), 0)
    cidx = lax.broadcasted_iota(jnp.int32, (lc, lc), 1)
    allowed = (r - cidx) * jnp.where(fwd, 1, -1) >= 0
    tri = jnp.where(allowed, 1.0, 0.0).astype(BF16)

    b_col = jnp.zeros((lc, nh), F32)
    for piece in _split3(lf_col):
        b_col = b_col + jnp.dot(tri, piece, preferred_element_type=F32)
    b_row = jnp.zeros((nh, lc), F32)
    for piece in _split3(lf_row):
        b_row = b_row + lax.dot_general(piece, tri, (((1,), (1,)), ((), ())), preferred_element_type=F32)
    a_all = jnp.sum(lf_row, axis=-1, keepdims=True)

    lane = lax.broadcasted_iota(jnp.int32, (lc, LANES_V7X), 1)
    ones_col = jnp.where(lane == 0, 1.0, 0.0).astype(BF16)

    for h in range(nh):
        q = q_ref[:, h * kp:(h + 1) * kp]
        kt = kt_ref[h * kp:(h + 1) * kp, :]
        v_ext = jnp.concatenate([v_ref[:, h * dv:(h + 1) * dv], ones_col], axis=1)
        b_c = b_col[:, h:h + 1]
        b_r = b_row[h:h + 1, :]
        li_r = li_row[h:h + 1, :]
        a = a_all[h:h + 1, :]
        m_in = m_scr[h][0:1, 0:1]
        c_ext = c_scr[h]

        w_end = a - b_r + li_r
        g = jnp.max(w_end, axis=-1, keepdims=True)
        e_r = jnp.exp(w_end - g)

        dmat = jnp.where(allowed, b_c - b_r + li_r, -jnp.inf)
        inter_log = b_c + m_in
        m_j = jnp.maximum(inter_log, jnp.max(dmat, axis=-1, keepdims=True))
        inter = jnp.exp(inter_log - m_j)
        p = jnp.exp(dmat - m_j)
        s = jnp.dot(q, kt, preferred_element_type=F32)
        qk = (s * p).astype(BF16)
        nd = (jnp.dot(qk, v_ext, preferred_element_type=F32)
              + inter * jnp.dot(q, c_ext.astype(BF16), preferred_element_type=F32))
        den = nd[:, dv:dv + 1]
        o_ref[0, :, h * dv:(h + 1) * dv] = nd[:, 0:dv] / jnp.maximum(jnp.abs(den), jnp.exp(-m_j))

        m_new = jnp.maximum(a + m_in, g)
        fdec = jnp.exp(a + m_in - m_new)
        iin = jnp.exp(g - m_new)
        kte = (kt.astype(F32) * e_r).astype(BF16)
        c_scr[h] = fdec * c_ext + iin * jnp.dot(kte, v_ext, preferred_element_type=F32)
        m_scr[h] = jnp.broadcast_to(m_new, m_scr.shape[1:])


def _mlstm(q, kt, v, gcol, grow, batch, seq):
    n = q.shape[0]
    lc = _tile(seq, ML_CHUNK_TILE)
    nc = seq // lc
    qw = ML_HEADS * ML_QK_PAD
    vw = ML_HEADS * ML_V

    def chunk(d, b, c):
        return b * nc + c + d * (nc - 1 - 2 * c)

    return pl.pallas_call(
        _mlstm_kernel,
        grid=(2, batch, nc),
        in_specs=[pl.BlockSpec((lc, qw), lambda d, b, c: (chunk(d, b, c), 0)),
                  pl.BlockSpec((qw, lc), lambda d, b, c: (0, chunk(d, b, c))),
                  pl.BlockSpec((lc, vw), lambda d, b, c: (chunk(d, b, c), 0)),
                  pl.BlockSpec((1, lc, 2 * ML_HEADS), lambda d, b, c: (d, chunk(d, b, c), 0)),
                  pl.BlockSpec((1, 2 * ML_HEADS, lc), lambda d, b, c: (d, 0, chunk(d, b, c)))],
        out_specs=pl.BlockSpec((1, lc, vw), lambda d, b, c: (d, chunk(d, b, c), 0)),
        out_shape=jax.ShapeDtypeStruct((2, n, vw), F32),
        scratch_shapes=[pltpu.VMEM((ML_HEADS, ML_QK_PAD, 2 * ML_V), F32),
                        pltpu.VMEM((ML_HEADS, 8, LANES_V7X), F32)],
        compiler_params=_params("parallel", "parallel", "arbitrary"),
        name="mlstm_scan",
    )(q, kt, v, gcol, grow)


def _ml_out_kernel(hs_ref, og_ref, hn_ref, h_ref, w_ref, o_ref):
    dv = ML_V
    hs = hs_ref[0] + hs_ref[1]
    parts = []
    for hd in range(ML_HEADS):
        x = hs[:, hd * dv:(hd + 1) * dv]
        parts.append(x * lax.rsqrt(jnp.mean(x * x, axis=-1, keepdims=True) + EPS))
    y = jnp.concatenate(parts, axis=1) * hn_ref[...] * jax.nn.sigmoid(og_ref[...].astype(F32))
    o_ref[...] = h_ref[...] + jnp.dot(y.astype(BF16), w_ref[...], preferred_element_type=F32)


def _ml_out(hs, og, head_norm, h, w):
    n, d = h.shape
    vw = ML_HEADS * ML_V
    tm = _tile(n, TOKEN_TILE)
    row = lambda i: (i, 0)
    const = lambda i: (0, 0)
    return pl.pallas_call(
        _ml_out_kernel,
        grid=(n // tm,),
        in_specs=[pl.BlockSpec((2, tm, vw), lambda i: (0, i, 0)),
                  pl.BlockSpec((tm, vw), row), pl.BlockSpec((1, vw), const),
                  pl.BlockSpec((tm, d), row), pl.BlockSpec(w.shape, const)],
        out_specs=pl.BlockSpec((tm, d), row),
        out_shape=jax.ShapeDtypeStruct((n, d), F32),
        compiler_params=_params("parallel"),
        name="mlstm_out",
    )(hs, og, head_norm, h, w)


def _router_kernel(x_ref, g_ref, w_ref, wlo_ref, xn_ref, aff_ref):
    xf = _rms(x_ref[...], g_ref[...])
    xn = xf.astype(BF16)
    xn_ref[...] = xn
    xlo = (xf - xn.astype(F32)).astype(BF16)
    logits = (jnp.dot(xn, w_ref[...], preferred_element_type=F32)
              + jnp.dot(xlo, w_ref[...], preferred_element_type=F32)
              + jnp.dot(xn, wlo_ref[...], preferred_element_type=F32))
    z = jnp.exp(logits - jnp.max(logits, axis=-1, keepdims=True))
    aff_ref[...] = z / jnp.sum(z, axis=-1, keepdims=True)


def _router(x, g, w_f32):
    n, d = x.shape
    tm = _tile(n, TOKEN_TILE)
    row = lambda i: (i, 0)
    const = lambda i: (0, 0)
    w = w_f32.astype(BF16)
    w_lo = (w_f32 - w.astype(F32)).astype(BF16)
    return pl.pallas_call(
        _router_kernel,
        grid=(n // tm,),
        in_specs=[pl.BlockSpec((tm, d), row), pl.BlockSpec((1, d), const),
                  pl.BlockSpec(w.shape, const), pl.BlockSpec(w.shape, const)],
        out_specs=[pl.BlockSpec((tm, d), row), pl.BlockSpec((tm, N_EXPERTS), row)],
        out_shape=[jax.ShapeDtypeStruct((n, d), BF16), jax.ShapeDtypeStruct((n, N_EXPERTS), F32)],
        compiler_params=_params("parallel"),
        name="moe_router",
    )(x, g, w, w_lo)


def _ffn_kernel(*refs, nchunk, epc):
    x_refs = refs[:nchunk]
    gate_ref, wg_ref, wu_ref, wd_ref, o_ref = refs[nchunk:]

    def compute(x):
        g = jnp.dot(x, wg_ref[0], preferred_element_type=F32)
        u = jnp.dot(x, wu_ref[0], preferred_element_type=F32)
        hid = (g * jax.nn.sigmoid(g) * u).astype(BF16)
        y = jnp.dot(hid, wd_ref[0], preferred_element_type=F32)
        o_ref[0] = (y * gate_ref[0]).astype(BF16)

    if nchunk == 1:
        compute(x_refs[0][0])
    else:
        chunk = pl.program_id(0) // epc
        for c in range(nchunk):
            @pl.when(chunk == c)
            def _(c=c):
                compute(x_refs[c][0])


def _expert_ffn(xes, gate, wg, wu, wd):
    nchunk = len(xes)
    epc, cap, d = xes[0].shape
    e = nchunk * epc
    f = wg.shape[-1]
    tm = _tile(cap, TOKEN_TILE)
    nj = cap // tm
    tok = lambda i, j: (i, j, 0)
    wmap = lambda i, j: (i, 0, 0)

    def chunk_map(c):
        def index(i, j):
            jj = jnp.where(i < c * epc, 0, jnp.where(i >= (c + 1) * epc, nj - 1, j))
            return (jnp.clip(i - c * epc, 0, epc - 1), jj, 0)
        return index

    return pl.pallas_call(
        functools.partial(_ffn_kernel, nchunk=nchunk, epc=epc),
        grid=(e, nj),
        in_specs=[pl.BlockSpec((1, tm, d), chunk_map(c)) for c in range(nchunk)]
        + [pl.BlockSpec((1, tm, 1), tok),
           pl.BlockSpec((1, d, f), wmap), pl.BlockSpec((1, d, f), wmap),
           pl.BlockSpec((1, f, d), wmap)],
        out_specs=pl.BlockSpec((1, tm, d), tok),
        out_shape=jax.ShapeDtypeStruct((e, cap, d), BF16),
        compiler_params=_params("parallel", "arbitrary"),
        name="moe_ffn",
    )(*xes, gate, wg, wu, wd)


def _combine_kernel(lo_ref, h_ref, idx_ref, ye_hbm, o_ref, buf, sem):
    tb, d = h_ref.shape
    ne, nrow, lanes = idx_ref.shape
    cap = nrow * lanes
    win = COMBINE_WINDOW
    b = pl.program_id(0)
    nb = pl.num_programs(0)
    tok = b * tb + lax.broadcasted_iota(jnp.int32, (tb, 1), 0)
    lane = lax.broadcasted_iota(jnp.int32, (1, win), 1)

    def first_start(blk, e):
        return jnp.minimum(lo_ref[e, blk] // lanes * lanes, cap - win)

    def fetch(e, start, slot):
        return pltpu.make_async_copy(ye_hbm.at[e, pl.ds(pl.multiple_of(start, lanes), win), :],
                                     buf.at[slot], sem.at[slot])

    def expand(e, start, row_lo, row_hi, slot):
        r0 = start // lanes
        ids = jnp.concatenate([idx_ref[e, pl.ds(r0 + k, 1), :] for k in range(win // lanes)], axis=1)
        rid = start + lane
        ids = jnp.where((rid >= row_lo) & (rid < row_hi), ids, -1)
        sel = jnp.where(tok == ids, 1.0, 0.0).astype(BF16)
        return jnp.dot(sel, buf[slot], preferred_element_type=F32)

    @pl.when(b == 0)
    def _():
        for e in range(ne):
            fetch(e, first_start(0, e), e).start()

    o_ref[...] = h_ref[...]
    group = 4
    total = None
    for e0 in range(0, ne, group):
        first = []
        for e in range(e0, e0 + group):
            lo = lo_ref[e, b]
            hi = lo_ref[e, b + 1]
            start0 = lo // lanes * lanes
            first.append((e, lo, hi, start0, jnp.minimum(start0, cap - win)))
        for e, lo, hi, start0, start in first:
            fetch(e, start, e).wait()
        for e, lo, hi, start0, start in first:
            part = expand(e, start, lo, hi, e)
            total = part if total is None else total + part

        for e, lo, hi, start0, start in first:
            def more(w, carry, e=e, lo=lo, hi=hi, start0=start0):
                nominal = start0 + w * win
                st = jnp.minimum(nominal, cap - win)
                cp = fetch(e, st, e)
                cp.start()
                cp.wait()
                o_ref[...] += expand(e, st, jnp.maximum(lo, nominal), hi, e)
                return carry

            nwin = (hi - start0 + win - 1) // win
            lax.fori_loop(1, nwin, more, 0)

            @pl.when(b + 1 < nb)
            def _(e=e):
                fetch(e, first_start(b + 1, e), e).start()

    o_ref[...] += total


def _combine(h, idx, ye):
    n, d = h.shape
    e, cap = idx.shape
    assert cap >= COMBINE_WINDOW and cap % LANES_V7X == 0 and e % 4 == 0
    tb = _tile(n, TOKEN_TILE)
    nb = n // tb
    bounds = jnp.arange(nb + 1, dtype=jnp.int32) * tb
    lo = jnp.sum(idx[:, :, None] < bounds[None, None, :], axis=1, dtype=jnp.int32)
    idx3 = idx.reshape(e, cap // LANES_V7X, LANES_V7X)
    grid_spec = pltpu.PrefetchScalarGridSpec(
        num_scalar_prefetch=1,
        grid=(nb,),
        in_specs=[pl.BlockSpec((tb, d), lambda i, lo_ref: (i, 0)),
                  pl.BlockSpec(idx3.shape, lambda i, lo_ref: (0, 0, 0)),
                  pl.BlockSpec(memory_space=pl.ANY)],
        out_specs=pl.BlockSpec((tb, d), lambda i, lo_ref: (i, 0)),
        scratch_shapes=[pltpu.VMEM((e, COMBINE_WINDOW, d), BF16), pltpu.SemaphoreType.DMA((e,))])
    return pl.pallas_call(
        _combine_kernel,
        grid_spec=grid_spec,
        out_shape=jax.ShapeDtypeStruct((n, d), F32),
        compiler_params=_params("arbitrary"),
        name="moe_combine",
    )(lo, h, idx3, ye)


def _moe(h, g, w_router, wg, wu, wd):
    n, d = h.shape
    cap = EC_CAPACITY * n // N_EXPERTS
    xn, aff = _router(h, g, w_router)
    gate, idx = lax.top_k(aff.T, cap)
    idx, gate = lax.sort((idx, gate), dimension=1, num_keys=1)
    nchunk = max(1, idx.size // GATHER_ROWS)
    epc = N_EXPERTS // nchunk
    xes = [xn[idx[c * epc:(c + 1) * epc]] for c in range(nchunk)]
    ye = _expert_ffn(xes, gate[..., None], wg, wu, wd)
    return _combine(h, idx, ye)


def _final_norm_kernel(x_ref, g_ref, o_ref):
    o_ref[...] = _rms(x_ref[...], g_ref[...])


def _final_norm(x, g):
    n, d = x.shape
    tm = _tile(n, TOKEN_TILE)
    return pl.pallas_call(
        _final_norm_kernel,
        grid=(n // tm,),
        in_specs=[pl.BlockSpec((tm, d), lambda i: (i, 0)), pl.BlockSpec((1, d), lambda i: (0, 0))],
        out_specs=pl.BlockSpec((tm, d), lambda i: (i, 0)),
        out_shape=jax.ShapeDtypeStruct((n, d), F32),
        compiler_params=_params("parallel"),
        name="final_norm",
    )(x, g)


def _rope_group_cols(w_pe):
    half = QK_ROPE // 2
    z = jnp.zeros((w_pe.shape[0], LANES_V7X // 2 - half), w_pe.dtype)
    return jnp.concatenate([w_pe[:, :half], z, w_pe[:, half:], z], axis=1)


def _rope_tables(seq):
    half = QK_ROPE // 2
    pos = jnp.arange(seq, dtype=F32)
    inv = ROPE_THETA ** (-jnp.arange(0, QK_ROPE, 2, dtype=F32) / QK_ROPE)
    ang = pos[:, None] * inv[None, :]
    c, s = jnp.cos(ang), jnp.sin(ang)
    z = jnp.zeros((seq, LANES_V7X // 2 - half), F32)
    return jnp.concatenate([c, z, c, z], axis=1), jnp.concatenate([-s, z, s, z], axis=1)


def _prep_mla(w_in, w_qb, w_kvb):
    lat = Q_LORA + KV_LORA
    w_in_p = jnp.concatenate([w_in[:, :lat], _rope_group_cols(w_in[:, lat:])], axis=1).astype(BF16)
    hd = QK_NOPE + QK_ROPE
    cols = []
    for h in range(MLA_HEADS):
        cols.append(w_qb[:, h * hd:h * hd + QK_NOPE])
        cols.append(_rope_group_cols(w_qb[:, h * hd + QK_NOPE:(h + 1) * hd]))
    w_kv = w_kvb.reshape(KV_LORA, MLA_HEADS, QK_NOPE + V_HEAD)
    w_k = w_kv[:, :, :QK_NOPE].reshape(KV_LORA, MLA_HEADS * QK_NOPE).astype(BF16)
    w_vt = w_kv[:, :, QK_NOPE:].reshape(KV_LORA, MLA_HEADS * V_HEAD).T.astype(BF16)
    return w_in_p, jnp.concatenate(cols, axis=1).astype(BF16), w_k, w_vt


def _prep_ml_in(w):
    qd = ML_HEADS * ML_QK
    cols = []
    for base in (0, qd):
        for h in range(ML_HEADS):
            cols.append(w[:, base + h * ML_QK:base + (h + 1) * ML_QK])
            cols.append(jnp.zeros((w.shape[0], ML_QK_PAD - ML_QK), w.dtype))
    cols.append(w[:, 2 * qd:])
    return jnp.concatenate(cols, axis=1).astype(BF16)


def _trunk(x, p):
    batch, seq, d = x.shape
    n = batch * seq
    h = x.reshape(n, d)
    for i in range(DEPTH):
        j = i // N_MIXERS
        kind = i % N_MIXERS
        g_mix = p["norm_mix"][i][None, :]
        if kind == 0:
            u, gb = _conv_in(h, g_mix, p["conv_w_in"][j])
            h = _conv_out(u, gb, h, p["conv_w_dw"][j], p["conv_w_out"][j], seq)
        elif kind == 1:
            cos, sin = _rope_tables(seq)
            q, k, vt = _mla_in(h, g_mix, p["mla_w_in"][j], p["mla_q_norm"][j][None, :], p["mla_w_qb"][j],
                               p["mla_kv_norm"][j][None, :], p["mla_w_k"][j], p["mla_w_vt"][j], cos, sin, seq)
            o = _attention(q, k, vt, batch, seq)
            h = _proj_residual(o, h, p["mla_w_out"][j])
        else:
            q, k, v, og, gates = _ml_in(h, g_mix, p["ml_w_in"][j], p["ml_b_gates"][j][None, :])
            gcol = gates.reshape(n, 2, 2 * ML_HEADS).transpose(1, 0, 2)
            grow = gcol.transpose(0, 2, 1)
            hs = _mlstm(q, k.T, v, gcol, grow, batch, seq)
            h = _ml_out(hs, og, p["ml_head_norm"][j][None, :], h, p["ml_w_out"][j])
        h = _moe(h, p["norm_ffn"][i][None, :], p["router_w"][i],
                 p["exp_w_gate"][i], p["exp_w_up"][i], p["exp_w_down"][i])
    return _final_norm(h, p["norm_final"][None, :]).reshape(batch, seq, d)


def kernel(x_prompt, x_sample, conv_w_in, conv_w_dw, conv_w_out, mla_w_in, mla_q_norm, mla_w_qb, mla_kv_norm, mla_w_kvb, mla_w_out, ml_w_in, ml_b_gates, ml_head_norm, ml_w_out, norm_mix, norm_ffn, router_w, exp_w_gate, exp_w_up, exp_w_down, norm_final):
    mla = [_prep_mla(mla_w_in[j], mla_w_qb[j], mla_w_kvb[j]) for j in range(mla_w_in.shape[0])]
    p = dict(
        conv_w_in=conv_w_in.astype(BF16), conv_w_dw=conv_w_dw, conv_w_out=conv_w_out.astype(BF16),
        mla_w_in=[m[0] for m in mla], mla_q_norm=mla_q_norm, mla_w_qb=[m[1] for m in mla],
        mla_kv_norm=mla_kv_norm, mla_w_k=[m[2] for m in mla], mla_w_vt=[m[3] for m in mla],
        mla_w_out=mla_w_out.astype(BF16),
        ml_w_in=[_prep_ml_in(ml_w_in[j]) for j in range(ml_w_in.shape[0])], ml_b_gates=ml_b_gates,
        ml_head_norm=ml_head_norm, ml_w_out=ml_w_out.astype(BF16),
        norm_mix=norm_mix, norm_ffn=norm_ffn, router_w=router_w,
        exp_w_gate=exp_w_gate.astype(BF16), exp_w_up=exp_w_up.astype(BF16),
        exp_w_down=exp_w_down.astype(BF16), norm_final=norm_final)
    return (_trunk(x_prompt, p), _trunk(x_sample, p))
```

```python
import functools

import jax
import jax.numpy as jnp
from jax import lax
from jax.experimental import pallas as pl
from jax.experimental.pallas import tpu as pltpu

F32 = jnp.float32
BF16 = jnp.bfloat16

D_MODEL = 1024
DEPTH = 4
N_MIXERS = 3
EPS = 1e-6
CONV_WIDTH = 3
MLA_HEADS = 8
Q_LORA = 384
KV_LORA = 256
QK_NOPE = 128
QK_ROPE = 64
V_HEAD = 128
ROPE_THETA = 10000.0
ML_HEADS = 8
ML_QK = 64
ML_V = 128
N_EXPERTS = 16
EC_CAPACITY = 2
D_EXPERT = 1024

LANES_V7X = 128
BF16_SUBLANES_V7X = 16
VMEM_BYTES_V7X = 64 * 1024 * 1024
VMEM_LIMIT_BYTES = VMEM_BYTES_V7X - 8 * 1024 * 1024

TOKEN_TILE = 512
ATTN_Q_TILE = 1024
ML_CHUNK_TILE = 256
COMBINE_WINDOW = 256
GATHER_ROWS = 32768
GATHER_OPERAND_ROWS = 32768

MLA_HEAD_PAD = 2 * LANES_V7X
VT_ROWS = V_HEAD + BF16_SUBLANES_V7X
LOG2_E = 1.4426950408889634
ML_QK_PAD = LANES_V7X


def _params(*sem):
    return pltpu.CompilerParams(dimension_semantics=sem, vmem_limit_bytes=VMEM_LIMIT_BYTES)


def _rms(x, g):
    ms = jnp.mean(x * x, axis=-1, keepdims=True)
    return x * lax.rsqrt(ms + EPS) * g


def _tile(n, pref):
    t = min(n, pref)
    assert n % t == 0, (n, t)
    return t


def _conv_in_kernel(x_ref, g_ref, w_ref, u_ref, gb_ref):
    d = D_MODEL
    xn = _rms(x_ref[...], g_ref[...]).astype(BF16)
    gb = jnp.dot(xn, w_ref[:, 0:d], preferred_element_type=F32)
    gc = jnp.dot(xn, w_ref[:, d:2 * d], preferred_element_type=F32)
    xv = jnp.dot(xn, w_ref[:, 2 * d:3 * d], preferred_element_type=F32)
    gb_ref[...] = gb.astype(BF16)
    u_ref[...] = (gc * xv).astype(BF16)


def _conv_in(x, g, w):
    n, d = x.shape
    tm = _tile(n, TOKEN_TILE)
    return pl.pallas_call(
        _conv_in_kernel,
        grid=(n // tm,),
        in_specs=[pl.BlockSpec((tm, d), lambda i: (i, 0)),
                  pl.BlockSpec((1, d), lambda i: (0, 0)),
                  pl.BlockSpec((d, 3 * d), lambda i: (0, 0))],
        out_specs=[pl.BlockSpec((tm, d), lambda i: (i, 0)),
                   pl.BlockSpec((tm, d), lambda i: (i, 0))],
        out_shape=[jax.ShapeDtypeStruct((n, d), BF16), jax.ShapeDtypeStruct((n, d), BF16)],
        compiler_params=_params("parallel"),
        name="conv_in",
    )(x, g, w)


def _conv_out_kernel(u_ref, up_ref, un_ref, gb_ref, h_ref, wdw_ref, w_ref, o_ref, *, tiles_per_seq):
    tm = u_ref.shape[0]
    pos = pl.program_id(0) % tiles_per_seq
    u = u_ref[...].astype(F32)
    halo = BF16_SUBLANES_V7X
    prev_row = jnp.where(pos == 0, 0.0, up_ref[halo - 1:halo, :].astype(F32))
    next_row = jnp.where(pos == tiles_per_seq - 1, 0.0, un_ref[0:1, :].astype(F32))
    row = lax.broadcasted_iota(jnp.int32, (tm, 1), 0)
    u_up = jnp.where(row == 0, prev_row, pltpu.roll(u, 1, axis=0))
    u_dn = jnp.where(row == tm - 1, next_row, pltpu.roll(u, tm - 1, axis=0))
    conv = u_up * wdw_ref[0:1, :] + u * wdw_ref[1:2, :] + u_dn * wdw_ref[2:3, :]
    g = (gb_ref[...].astype(F32) * conv).astype(BF16)
    o_ref[...] = h_ref[...] + jnp.dot(g, w_ref[...], preferred_element_type=F32)


def _conv_out(u, gb, h, w_dw, w_out, seq):
    n, d = u.shape
    tm = _tile(seq, TOKEN_TILE)
    halo = BF16_SUBLANES_V7X
    r = tm // halo
    nblk = n // halo
    return pl.pallas_call(
        functools.partial(_conv_out_kernel, tiles_per_seq=seq // tm),
        grid=(n // tm,),
        in_specs=[pl.BlockSpec((tm, d), lambda i: (i, 0)),
                  pl.BlockSpec((halo, d), lambda i: (jnp.maximum(i * r - 1, 0), 0)),
                  pl.BlockSpec((halo, d), lambda i: (jnp.minimum((i + 1) * r, nblk - 1), 0)),
                  pl.BlockSpec((tm, d), lambda i: (i, 0)),
                  pl.BlockSpec((tm, d), lambda i: (i, 0)),
                  pl.BlockSpec((CONV_WIDTH, d), lambda i: (0, 0)),
                  pl.BlockSpec((d, d), lambda i: (0, 0))],
        out_specs=pl.BlockSpec((tm, d), lambda i: (i, 0)),
        out_shape=jax.ShapeDtypeStruct((n, d), F32),
        compiler_params=_params("parallel"),
        name="conv_out",
    )(u, u, u, gb, h, w_dw, w_out)


def _rope_group(x, c, s):
    return x * c + pltpu.roll(x, LANES_V7X // 2, axis=1) * s


def _mla_in_kernel(x_ref, g_ref, win_ref, qn_ref, kvn_ref, wqb_ref, wk_ref, wvt_ref, cos_ref, sin_ref,
                   q_ref, k_ref, vt_ref):
    hp = MLA_HEAD_PAD
    tm = x_ref.shape[0]
    scale = (QK_NOPE + QK_ROPE) ** -0.5 * LOG2_E
    xn = _rms(x_ref[...], g_ref[...]).astype(BF16)
    lat = jnp.dot(xn, win_ref[...], preferred_element_type=F32)
    qn = _rms(lat[:, 0:Q_LORA], qn_ref[...]).astype(BF16)
    kvn = _rms(lat[:, Q_LORA:Q_LORA + KV_LORA], kvn_ref[...]).astype(BF16)
    c = cos_ref[...]
    s = sin_ref[...]
    k_pe = _rope_group(lat[:, Q_LORA + KV_LORA:], c, s).astype(BF16)
    extra = lax.broadcasted_iota(jnp.int32, (VT_ROWS - V_HEAD, tm), 0)
    ones_rows = jnp.where(extra == 0, 1.0, 0.0).astype(BF16)
    for h in range(MLA_HEADS):
        qh = jnp.dot(qn, wqb_ref[:, h * hp:(h + 1) * hp], preferred_element_type=F32)
        q_ref[:, h * hp:h * hp + QK_NOPE] = (qh[:, 0:QK_NOPE] * scale).astype(BF16)
        q_ref[:, h * hp + QK_NOPE:(h + 1) * hp] = (_rope_group(qh[:, QK_NOPE:], c, s) * scale).astype(BF16)
        kh = jnp.dot(kvn, wk_ref[:, h * QK_NOPE:(h + 1) * QK_NOPE], preferred_element_type=F32)
        k_ref[:, h * hp:h * hp + QK_NOPE] = kh.astype(BF16)
        k_ref[:, h * hp + QK_NOPE:(h + 1) * hp] = k_pe
        vt = lax.dot_general(wvt_ref[h * V_HEAD:(h + 1) * V_HEAD, :], kvn, (((1,), (1,)), ((), ())),
                             preferred_element_type=F32)
        vt_ref[0, h, 0:V_HEAD, :] = vt.astype(BF16)
        vt_ref[0, h, V_HEAD:VT_ROWS, :] = ones_rows


def _mla_in(x, g, w_in, q_norm, w_qb, kv_norm, w_k, w_vt, cos, sin, seq):
    n, d = x.shape
    tm = _tile(seq, TOKEN_TILE)
    tps = seq // tm
    hq = MLA_HEADS * MLA_HEAD_PAD
    const = lambda i: (0, 0)
    return pl.pallas_call(
        _mla_in_kernel,
        grid=(n // tm,),
        in_specs=[pl.BlockSpec((tm, d), lambda i: (i, 0)),
                  pl.BlockSpec((1, d), const),
                  pl.BlockSpec(w_in.shape, const),
                  pl.BlockSpec((1, Q_LORA), const),
                  pl.BlockSpec((1, KV_LORA), const),
                  pl.BlockSpec(w_qb.shape, const),
                  pl.BlockSpec(w_k.shape, const),
                  pl.BlockSpec(w_vt.shape, const),
                  pl.BlockSpec((tm, LANES_V7X), lambda i: (i % tps, 0)),
                  pl.BlockSpec((tm, LANES_V7X), lambda i: (i % tps, 0))],
        out_specs=[pl.BlockSpec((tm, hq), lambda i: (i, 0)),
                   pl.BlockSpec((tm, hq), lambda i: (i, 0)),
                   pl.BlockSpec((1, MLA_HEADS, VT_ROWS, tm), lambda i: (i, 0, 0, 0))],
        out_shape=[jax.ShapeDtypeStruct((n, hq), BF16),
                   jax.ShapeDtypeStruct((n, hq), BF16),
                   jax.ShapeDtypeStruct((n // tm, MLA_HEADS, VT_ROWS, tm), BF16)],
        compiler_params=_params("parallel"),
        name="mla_in",
    )(x, g, w_in, q_norm, kv_norm, w_qb, w_k, w_vt, cos, sin)


def _attn_kernel(q_ref, k_ref, vt_ref, o_ref, s0, s1, p0, p1, acc_ref):
    tq = q_ref.shape[0]
    nk, _, _, tk = vt_ref.shape
    assert nk % 2 == 0
    q = q_ref[...]

    def scores(j, s_ref):
        start = pl.multiple_of(j * tk, tk)
        k = k_ref[pl.ds(start, tk), :]
        st = lax.dot_general(k, q, (((1,), (1,)), ((), ())), preferred_element_type=F32)
        s_ref[...] = st
        return jnp.max(st, axis=0, keepdims=True)

    def exps(s_ref, p_ref, m, tile_max):
        m_new = jnp.maximum(m, tile_max)
        p_ref[...] = jnp.exp2(s_ref[...] - m_new).astype(BF16)
        return m_new, jnp.exp2(m - m_new)

    def values(j, p_ref, alpha):
        acc_ref[...] = alpha * acc_ref[...] + jnp.dot(vt_ref[j, 0], p_ref[...], preferred_element_type=F32)

    acc_ref[...] = jnp.zeros_like(acc_ref)
    x0 = scores(0, s0)
    x1 = scores(1, s1)
    m, alpha = exps(s0, p0, jnp.full((1, tq), -jnp.inf, F32), x0)

    steps = nk - 2
    unroll = max(u for u in (2, 4, 6, 8, 10) if steps % u == 0) if steps else 2

    def group(i, carry):
        m, alpha, x1 = carry
        for t in range(0, unroll, 2):
            s = unroll * i + t + 1
            x0 = scores(s + 1, s0)
            values(s - 1, p0, alpha)
            m, alpha = exps(s1, p1, m, x1)
            x1 = scores(s + 2, s1)
            values(s, p1, alpha)
            m, alpha = exps(s0, p0, m, x0)
        return m, alpha, x1

    m, alpha, x1 = lax.fori_loop(0, steps // unroll, group, (m, alpha, x1))
    values(nk - 2, p0, alpha)
    m, alpha = exps(s1, p1, m, x1)
    values(nk - 1, p1, alpha)
    acc = acc_ref[...]
    out_t = acc[0:V_HEAD, :] / acc[V_HEAD:V_HEAD + 1, :]
    o_ref[...] = out_t.T.astype(BF16)


def _attention(q, k, vt, batch, seq):
    n = q.shape[0]
    tq = _tile(seq, ATTN_Q_TILE)
    tk = vt.shape[-1]
    nq = seq // tq
    nk = seq // tk
    hp = MLA_HEAD_PAD
    return pl.pallas_call(
        _attn_kernel,
        grid=(batch, MLA_HEADS, nq),
        in_specs=[pl.BlockSpec((tq, hp), lambda b, h, i: (b * nq + i, h)),
                  pl.BlockSpec((seq, hp), lambda b, h, i: (b, h)),
                  pl.BlockSpec((nk, 1, VT_ROWS, tk), lambda b, h, i: (b, h, 0, 0))],
        out_specs=pl.BlockSpec((tq, V_HEAD), lambda b, h, i: (b * nq + i, h)),
        out_shape=jax.ShapeDtypeStruct((n, MLA_HEADS * V_HEAD), BF16),
        scratch_shapes=[pltpu.VMEM((tk, tq), F32), pltpu.VMEM((tk, tq), F32),
                        pltpu.VMEM((tk, tq), BF16), pltpu.VMEM((tk, tq), BF16),
                        pltpu.VMEM((VT_ROWS, tq), F32)],
        compiler_params=_params("parallel", "parallel", "arbitrary"),
        name="mla_attention",
    )(q, k, vt)


def _proj_residual_kernel(a_ref, h_ref, w_ref, o_ref):
    o_ref[...] = h_ref[...] + jnp.dot(a_ref[...], w_ref[...], preferred_element_type=F32)


def _proj_residual(a, h, w):
    n, d = h.shape
    tm = _tile(n, TOKEN_TILE)
    return pl.pallas_call(
        _proj_residual_kernel,
        grid=(n // tm,),
        in_specs=[pl.BlockSpec((tm, a.shape[1]), lambda i: (i, 0)),
                  pl.BlockSpec((tm, d), lambda i: (i, 0)),
                  pl.BlockSpec(w.shape, lambda i: (0, 0))],
        out_specs=pl.BlockSpec((tm, d), lambda i: (i, 0)),
        out_shape=jax.ShapeDtypeStruct((n, d), F32),
        compiler_params=_params("parallel"),
        name="proj_residual",
    )(a, h, w)


def _ml_in_kernel(x_ref, g_ref, w_ref, b_ref, q_ref, k_ref, v_ref, o_ref, gt_ref):
    qw = ML_HEADS * ML_QK_PAD
    vw = ML_HEADS * ML_V
    xn = _rms(x_ref[...], g_ref[...]).astype(BF16)
    q = jnp.dot(xn, w_ref[:, 0:qw], preferred_element_type=F32)
    q_ref[...] = (q * (ML_QK ** -0.5)).astype(BF16)
    k_ref[...] = jnp.dot(xn, w_ref[:, qw:2 * qw], preferred_element_type=F32).astype(BF16)
    v_ref[...] = jnp.dot(xn, w_ref[:, 2 * qw:2 * qw + vw], preferred_element_type=F32).astype(BF16)
    o_ref[...] = jnp.dot(xn, w_ref[:, 2 * qw + vw:2 * qw + 2 * vw], preferred_element_type=F32).astype(BF16)
    gt_ref[...] = jnp.dot(xn, w_ref[:, 2 * qw + 2 * vw:], preferred_element_type=F32) + b_ref[...]


def _ml_in(x, g, w, b_gates):
    n, d = x.shape
    tm = _tile(n, TOKEN_TILE)
    qw = ML_HEADS * ML_QK_PAD
    vw = ML_HEADS * ML_V
    ng = 4 * ML_HEADS
    row = lambda i: (i, 0)
    const = lambda i: (0, 0)
    return pl.pallas_call(
        _ml_in_kernel,
        grid=(n // tm,),
        in_specs=[pl.BlockSpec((tm, d), row), pl.BlockSpec((1, d), const),
                  pl.BlockSpec(w.shape, const), pl.BlockSpec((1, ng), const)],
        out_specs=[pl.BlockSpec((tm, qw), row), pl.BlockSpec((tm, qw), row),
                   pl.BlockSpec((tm, vw), row), pl.BlockSpec((tm, vw), row),
                   pl.BlockSpec((tm, ng), row)],
        out_shape=[jax.ShapeDtypeStruct((n, qw), BF16), jax.ShapeDtypeStruct((n, qw), BF16),
                   jax.ShapeDtypeStruct((n, vw), BF16), jax.ShapeDtypeStruct((n, vw), BF16),
                   jax.ShapeDtypeStruct((n, ng), F32)],
        compiler_params=_params("parallel"),
        name="mlstm_in",
    )(x, g, w, b_gates)


def _split3(x):
    hi = x.astype(BF16)
    r1 = x - hi.astype(F32)
    mid = r1.astype(BF16)
    lo = (r1 - mid.astype(F32)).astype(BF16)
    return hi, mid, lo


def _mlstm_kernel(q_ref, kt_ref, v_ref, gc_ref, gr_ref, o_ref, c_scr, m_scr):
    lc = q_ref.shape[0]
    nh = ML_HEADS
    kp = ML_QK_PAD
    dv = ML_V
    fwd = pl.program_id(0) == 0

    @pl.when(pl.program_id(2) == 0)
    def _():
        c_scr[...] = jnp.zeros_like(c_scr)
        m_scr[...] = jnp.zeros_like(m_scr)

    gcol = gc_ref[0]
    grow = gr_ref[0]
    lf_col = jax.nn.log_sigmoid(gcol[:, nh:2 * nh])
    li_row = grow[0:nh, :]
    lf_row = jax.nn.log_sigmoid(grow[nh:2 * nh, :])

    r = lax.broadcasted_iota(jnp.int32, (lc, lc), 0)
    cidx = lax.broadcasted_iota(jnp.int32, (lc, lc), 1)
    allowed = (r - cidx) * jnp.where(fwd, 1, -1) >= 0
    tri = jnp.where(allowed, 1.0, 0.0).astype(BF16)

    b_col = jnp.zeros((lc, nh), F32)
    for piece in _split3(lf_col):
        b_col = b_col + jnp.dot(tri, piece, preferred_element_type=F32)
    b_row = jnp.zeros((nh, lc), F32)
    for piece in _split3(lf_row):
        b_row = b_row + lax.dot_general(piece, tri, (((1,), (1,)), ((), ())), preferred_element_type=F32)
    a_all = jnp.sum(lf_row, axis=-1, keepdims=True)

    lane = lax.broadcasted_iota(jnp.int32, (lc, LANES_V7X), 1)
    ones_col = jnp.where(lane == 0, 1.0, 0.0).astype(BF16)

    for h in range(nh):
        q = q_ref[:, h * kp:(h + 1) * kp]
        kt = kt_ref[h * kp:(h + 1) * kp, :]
        v_ext = jnp.concatenate([v_ref[:, h * dv:(h + 1) * dv], ones_col], axis=1)
        b_c = b_col[:, h:h + 1]
        b_r = b_row[h:h + 1, :]
        li_r = li_row[h:h + 1, :]
        a = a_all[h:h + 1, :]
        m_in = m_scr[h][0:1, 0:1]
        c_ext = c_scr[h]

        w_end = a - b_r + li_r
        g = jnp.max(w_end, axis=-1, keepdims=True)
        e_r = jnp.exp(w_end - g)

        dmat = jnp.where(allowed, b_c - b_r + li_r, -jnp.inf)
        inter_log = b_c + m_in
        m_j = jnp.maximum(inter_log, jnp.max(dmat, axis=-1, keepdims=True))
        inter = jnp.exp(inter_log - m_j)
        p = jnp.exp(dmat - m_j)
        s = jnp.dot(q, kt, preferred_element_type=F32)
        qk = (s * p).astype(BF16)
        nd = (jnp.dot(qk, v_ext, preferred_element_type=F32)
              + inter * jnp.dot(q, c_ext.astype(BF16), preferred_element_type=F32))
        den = nd[:, dv:dv + 1]
        o_ref[0, :, h * dv:(h + 1) * dv] = nd[:, 0:dv] / jnp.maximum(jnp.abs(den), jnp.exp(-m_j))

        m_new = jnp.maximum(a + m_in, g)
        fdec = jnp.exp(a + m_in - m_new)
        iin = jnp.exp(g - m_new)
        kte = (kt.astype(F32) * e_r).astype(BF16)
        c_scr[h] = fdec * c_ext + iin * jnp.dot(kte, v_ext, preferred_element_type=F32)
        m_scr[h] = jnp.broadcast_to(m_new, m_scr.shape[1:])


def _mlstm(q, kt, v, gcol, grow, batch, seq):
    n = q.shape[0]
    lc = _tile(seq, ML_CHUNK_TILE)
    nc = seq // lc
    qw = ML_HEADS * ML_QK_PAD
    vw = ML_HEADS * ML_V

    def chunk(d, b, c):
        return b * nc + c + d * (nc - 1 - 2 * c)

    return pl.pallas_call(
        _mlstm_kernel,
        grid=(2, batch, nc),
        in_specs=[pl.BlockSpec((lc, qw), lambda d, b, c: (chunk(d, b, c), 0)),
                  pl.BlockSpec((qw, lc), lambda d, b, c: (0, chunk(d, b, c))),
                  pl.BlockSpec((lc, vw), lambda d, b, c: (chunk(d, b, c), 0)),
                  pl.BlockSpec((1, lc, 2 * ML_HEADS), lambda d, b, c: (d, chunk(d, b, c), 0)),
                  pl.BlockSpec((1, 2 * ML_HEADS, lc), lambda d, b, c: (d, 0, chunk(d, b, c)))],
        out_specs=pl.BlockSpec((1, lc, vw), lambda d, b, c: (d, chunk(d, b, c), 0)),
        out_shape=jax.ShapeDtypeStruct((2, n, vw), F32),
        scratch_shapes=[pltpu.VMEM((ML_HEADS, ML_QK_PAD, 2 * ML_V), F32),
                        pltpu.VMEM((ML_HEADS, 8, LANES_V7X), F32)],
        compiler_params=_params("parallel", "parallel", "arbitrary"),
        name="mlstm_scan",
    )(q, kt, v, gcol, grow)


def _ml_out_kernel(hs_ref, og_ref, hn_ref, h_ref, w_ref, o_ref):
    dv = ML_V
    hs = hs_ref[0] + hs_ref[1]
    parts = []
    for hd in range(ML_HEADS):
        x = hs[:, hd * dv:(hd + 1) * dv]
        parts.append(x * lax.rsqrt(jnp.mean(x * x, axis=-1, keepdims=True) + EPS))
    y = jnp.concatenate(parts, axis=1) * hn_ref[...] * jax.nn.sigmoid(og_ref[...].astype(F32))
    o_ref[...] = h_ref[...] + jnp.dot(y.astype(BF16), w_ref[...], preferred_element_type=F32)


def _ml_out(hs, og, head_norm, h, w):
    n, d = h.shape
    vw = ML_HEADS * ML_V
    tm = _tile(n, TOKEN_TILE)
    row = lambda i: (i, 0)
    const = lambda i: (0, 0)
    return pl.pallas_call(
        _ml_out_kernel,
        grid=(n // tm,),
        in_specs=[pl.BlockSpec((2, tm, vw), lambda i: (0, i, 0)),
                  pl.BlockSpec((tm, vw), row), pl.BlockSpec((1, vw), const),
                  pl.BlockSpec((tm, d), row), pl.BlockSpec(w.shape, const)],
        out_specs=pl.BlockSpec((tm, d), row),
        out_shape=jax.ShapeDtypeStruct((n, d), F32),
        compiler_params=_params("parallel"),
        name="mlstm_out",
    )(hs, og, head_norm, h, w)


def _router_kernel(x_ref, g_ref, w_ref, wlo_ref, xn_ref, aff_ref, *, nreal):
    @pl.when(pl.program_id(0) < nreal)
    def _():
        xf = _rms(x_ref[...], g_ref[...])
        xn = xf.astype(BF16)
        xn_ref[...] = xn
        xlo = (xf - xn.astype(F32)).astype(BF16)
        logits = (jnp.dot(xn, w_ref[...], preferred_element_type=F32)
                  + jnp.dot(xlo, w_ref[...], preferred_element_type=F32)
                  + jnp.dot(xn, wlo_ref[...], preferred_element_type=F32))
        z = jnp.exp(logits - jnp.max(logits, axis=-1, keepdims=True))
        aff_ref[...] = z / jnp.sum(z, axis=-1, keepdims=True)

    @pl.when(pl.program_id(0) >= nreal)
    def _():
        xn_ref[...] = jnp.zeros_like(xn_ref)


def _router(x, g, w_f32):
    n, d = x.shape
    tm = _tile(n, TOKEN_TILE)
    nreal = n // tm
    npad = max(n, GATHER_OPERAND_ROWS)
    row = lambda i: (jnp.minimum(i, nreal - 1), 0)
    const = lambda i: (0, 0)
    w = w_f32.astype(BF16)
    w_lo = (w_f32 - w.astype(F32)).astype(BF16)
    return pl.pallas_call(
        functools.partial(_router_kernel, nreal=nreal),
        grid=(npad // tm,),
        in_specs=[pl.BlockSpec((tm, d), row), pl.BlockSpec((1, d), const),
                  pl.BlockSpec(w.shape, const), pl.BlockSpec(w.shape, const)],
        out_specs=[pl.BlockSpec((tm, d), lambda i: (i, 0)), pl.BlockSpec((tm, N_EXPERTS), row)],
        out_shape=[jax.ShapeDtypeStruct((npad, d), BF16), jax.ShapeDtypeStruct((n, N_EXPERTS), F32)],
        compiler_params=_params("arbitrary"),
        name="moe_router",
    )(x, g, w, w_lo)


def _ffn_kernel(*refs, nchunk, epc):
    x_refs = refs[:nchunk]
    gate_ref, wg_ref, wu_ref, wd_ref, o_ref, wg_s, wu_s, wd_s = refs[nchunk:]

    @pl.when(pl.program_id(1) == 0)
    def _():
        wg_s[...] = wg_ref[0, 0].astype(BF16)
        wu_s[...] = wu_ref[0, 0].astype(BF16)
        wd_s[...] = wd_ref[0, 0].astype(BF16)

    def compute(x):
        g = jnp.dot(x, wg_s[...], preferred_element_type=F32)
        u = jnp.dot(x, wu_s[...], preferred_element_type=F32)
        hid = (g * jax.nn.sigmoid(g) * u).astype(BF16)
        y = jnp.dot(hid, wd_s[...], preferred_element_type=F32)
        o_ref[0] = (y * gate_ref[0]).astype(BF16)

    if nchunk == 1:
        compute(x_refs[0][0])
    else:
        chunk = pl.program_id(0) // epc
        for c in range(nchunk):
            @pl.when(chunk == c)
            def _(c=c):
                compute(x_refs[c][0])


def _expert_ffn(xes, gate, wg, wu, wd, layer):
    nchunk = len(xes)
    epc, cap, d = xes[0].shape
    e = nchunk * epc
    f = wg.shape[-1]
    tm = _tile(cap, TOKEN_TILE)
    nj = cap // tm
    tok = lambda i, j: (i, j, 0)
    wmap = lambda i, j: (layer, i, 0, 0)

    def chunk_map(c):
        def index(i, j):
            jj = jnp.where(i < c * epc, 0, jnp.where(i >= (c + 1) * epc, nj - 1, j))
            return (jnp.clip(i - c * epc, 0, epc - 1), jj, 0)
        return index

    return pl.pallas_call(
        functools.partial(_ffn_kernel, nchunk=nchunk, epc=epc),
        grid=(e, nj),
        in_specs=[pl.BlockSpec((1, tm, d), chunk_map(c)) for c in range(nchunk)]
        + [pl.BlockSpec((1, tm, 1), tok),
           pl.BlockSpec((1, 1, d, f), wmap), pl.BlockSpec((1, 1, d, f), wmap),
           pl.BlockSpec((1, 1, f, d), wmap)],
        out_specs=pl.BlockSpec((1, tm, d), tok),
        out_shape=jax.ShapeDtypeStruct((e, cap, d), BF16),
        scratch_shapes=[pltpu.VMEM((d, f), BF16), pltpu.VMEM((d, f), BF16), pltpu.VMEM((f, d), BF16)],
        compiler_params=_params("arbitrary", "arbitrary"),
        name="moe_ffn",
    )(*xes, gate, wg, wu, wd)


def _combine_kernel(lo_ref, h_ref, idx_ref, ye_hbm, o_ref, buf, sem):
    tb, d = h_ref.shape
    ne, nrow, lanes = idx_ref.shape
    cap = nrow * lanes
    win = COMBINE_WINDOW
    b = pl.program_id(0)
    nb = pl.num_programs(0)
    tok = b * tb + lax.broadcasted_iota(jnp.int32, (tb, 1), 0)
    lane = lax.broadcasted_iota(jnp.int32, (1, win), 1)

    def first_start(blk, e):
        return jnp.minimum(lo_ref[e, blk] // lanes * lanes, cap - win)

    def fetch(e, start, slot):
        return pltpu.make_async_copy(ye_hbm.at[e, pl.ds(pl.multiple_of(start, lanes), win), :],
                                     buf.at[slot], sem.at[slot])

    def expand(e, start, row_lo, row_hi, slot):
        r0 = start // lanes
        ids = jnp.concatenate([idx_ref[e, pl.ds(r0 + k, 1), :] for k in range(win // lanes)], axis=1)
        rid = start + lane
        ids = jnp.where((rid >= row_lo) & (rid < row_hi), ids, -1)
        sel = jnp.where(tok == ids, 1.0, 0.0).astype(BF16)
        return jnp.dot(sel, buf[slot], preferred_element_type=F32)

    @pl.when(b == 0)
    def _():
        for e in range(ne):
            fetch(e, first_start(0, e), e).start()

    o_ref[...] = h_ref[...]
    group = 4
    total = None
    for e0 in range(0, ne, group):
        first = []
        for e in range(e0, e0 + group):
            lo = lo_ref[e, b]
            hi = lo_ref[e, b + 1]
            start0 = lo // lanes * lanes
            first.append((e, lo, hi, start0, jnp.minimum(start0, cap - win)))
        for e, lo, hi, start0, start in first:
            fetch(e, start, e).wait()
        for e, lo, hi, start0, start in first:
            part = expand(e, start, lo, hi, e)
            total = part if total is None else total + part

        for e, lo, hi, start0, start in first:
            def more(w, carry, e=e, lo=lo, hi=hi, start0=start0):
                nominal = start0 + w * win
                st = jnp.minimum(nominal, cap - win)
                cp = fetch(e, st, e)
                cp.start()
                cp.wait()
                o_ref[...] += expand(e, st, jnp.maximum(lo, nominal), hi, e)
                return carry

            nwin = (hi - start0 + win - 1) // win
            lax.fori_loop(1, nwin, more, 0)

            @pl.when(b + 1 < nb)
            def _(e=e):
                fetch(e, first_start(b + 1, e), e).start()

    o_ref[...] += total


def _combine(h, idx, ye):
    n, d = h.shape
    e, cap = idx.shape
    assert cap >= COMBINE_WINDOW and cap % LANES_V7X == 0 and e % 4 == 0
    tb = _tile(n, TOKEN_TILE)
    nb = n // tb
    bounds = jnp.arange(nb + 1, dtype=jnp.int32) * tb
    lo = jnp.sum(idx[:, :, None] < bounds[None, None, :], axis=1, dtype=jnp.int32)
    idx3 = idx.reshape(e, cap // LANES_V7X, LANES_V7X)
    grid_spec = pltpu.PrefetchScalarGridSpec(
        num_scalar_prefetch=1,
        grid=(nb,),
        in_specs=[pl.BlockSpec((tb, d), lambda i, lo_ref: (i, 0)),
                  pl.BlockSpec(idx3.shape, lambda i, lo_ref: (0, 0, 0)),
                  pl.BlockSpec(memory_space=pl.ANY)],
        out_specs=pl.BlockSpec((tb, d), lambda i, lo_ref: (i, 0)),
        scratch_shapes=[pltpu.VMEM((e, COMBINE_WINDOW, d), BF16), pltpu.SemaphoreType.DMA((e,))])
    return pl.pallas_call(
        _combine_kernel,
        grid_spec=grid_spec,
        out_shape=jax.ShapeDtypeStruct((n, d), F32),
        compiler_params=_params("arbitrary"),
        name="moe_combine",
    )(lo, h, idx3, ye)


def _moe(h, g, w_router, wg, wu, wd, layer):
    n, d = h.shape
    cap = EC_CAPACITY * n // N_EXPERTS
    xn, aff = _router(h, g, w_router)
    gate, idx = lax.top_k(aff.T, cap)
    idx, gate = lax.sort((idx, gate), dimension=1, num_keys=1)
    nchunk = max(1, idx.size // GATHER_ROWS)
    epc = N_EXPERTS // nchunk
    xes = [xn[idx[c * epc:(c + 1) * epc]] for c in range(nchunk)]
    ye = _expert_ffn(xes, gate[..., None], wg, wu, wd, layer)
    return _combine(h, idx, ye)


def _final_norm_kernel(x_ref, g_ref, o_ref):
    o_ref[...] = _rms(x_ref[...], g_ref[...])


def _final_norm(x, g):
    n, d = x.shape
    tm = _tile(n, TOKEN_TILE)
    return pl.pallas_call(
        _final_norm_kernel,
        grid=(n // tm,),
        in_specs=[pl.BlockSpec((tm, d), lambda i: (i, 0)), pl.BlockSpec((1, d), lambda i: (0, 0))],
        out_specs=pl.BlockSpec((tm, d), lambda i: (i, 0)),
        out_shape=jax.ShapeDtypeStruct((n, d), F32),
        compiler_params=_params("parallel"),
        name="final_norm",
    )(x, g)


def _rope_group_cols(w_pe):
    half = QK_ROPE // 2
    z = jnp.zeros((w_pe.shape[0], LANES_V7X // 2 - half), w_pe.dtype)
    return jnp.concatenate([w_pe[:, :half], z, w_pe[:, half:], z], axis=1)


def _rope_tables(seq):
    half = QK_ROPE // 2
    pos = jnp.arange(seq, dtype=F32)
    inv = ROPE_THETA ** (-jnp.arange(0, QK_ROPE, 2, dtype=F32) / QK_ROPE)
    ang = pos[:, None] * inv[None, :]
    c, s = jnp.cos(ang), jnp.sin(ang)
    z = jnp.zeros((seq, LANES_V7X // 2 - half), F32)
    return jnp.concatenate([c, z, c, z], axis=1), jnp.concatenate([-s, z, s, z], axis=1)


def _prep_mla(w_in, w_qb, w_kvb):
    lat = Q_LORA + KV_LORA
    w_in_p = jnp.concatenate([w_in[:, :lat], _rope_group_cols(w_in[:, lat:])], axis=1).astype(BF16)
    hd = QK_NOPE + QK_ROPE
    cols = []
    for h in range(MLA_HEADS):
        cols.append(w_qb[:, h * hd:h * hd + QK_NOPE])
        cols.append(_rope_group_cols(w_qb[:, h * hd + QK_NOPE:(h + 1) * hd]))
    w_kv = w_kvb.reshape(KV_LORA, MLA_HEADS, QK_NOPE + V_HEAD)
    w_k = w_kv[:, :, :QK_NOPE].reshape(KV_LORA, MLA_HEADS * QK_NOPE).astype(BF16)
    w_vt = w_kv[:, :, QK_NOPE:].reshape(KV_LORA, MLA_HEADS * V_HEAD).T.astype(BF16)
    return w_in_p, jnp.concatenate(cols, axis=1).astype(BF16), w_k, w_vt


def _prep_ml_in(w):
    qd = ML_HEADS * ML_QK
    cols = []
    for base in (0, qd):
        for h in range(ML_HEADS):
            cols.append(w[:, base + h * ML_QK:base + (h + 1) * ML_QK])
            cols.append(jnp.zeros((w.shape[0], ML_QK_PAD - ML_QK), w.dtype))
    cols.append(w[:, 2 * qd:])
    return jnp.concatenate(cols, axis=1).astype(BF16)


def _trunk(x, p):
    batch, seq, d = x.shape
    n = batch * seq
    h = x.reshape(n, d)
    for i in range(DEPTH):
        j = i // N_MIXERS
        kind = i % N_MIXERS
        g_mix = p["norm_mix"][i][None, :]
        if kind == 0:
            u, gb = _conv_in(h, g_mix, p["conv_w_in"][j])
            h = _conv_out(u, gb, h, p["conv_w_dw"][j], p["conv_w_out"][j], seq)
        elif kind == 1:
            cos, sin = _rope_tables(seq)
            q, k, vt = _mla_in(h, g_mix, p["mla_w_in"][j], p["mla_q_norm"][j][None, :], p["mla_w_qb"][j],
                               p["mla_kv_norm"][j][None, :], p["mla_w_k"][j], p["mla_w_vt"][j], cos, sin, seq)
            o = _attention(q, k, vt, batch, seq)
            h = _proj_residual(o, h, p["mla_w_out"][j])
        else:
            q, k, v, og, gates = _ml_in(h, g_mix, p["ml_w_in"][j], p["ml_b_gates"][j][None, :])
            gcol = gates.reshape(n, 2, 2 * ML_HEADS).transpose(1, 0, 2)
            grow = gcol.transpose(0, 2, 1)
            hs = _mlstm(q, k.T, v, gcol, grow, batch, seq)
            h = _ml_out(hs, og, p["ml_head_norm"][j][None, :], h, p["ml_w_out"][j])
        h = _moe(h, p["norm_ffn"][i][None, :], p["router_w"][i],
                 p["exp_w_gate"], p["exp_w_up"], p["exp_w_down"], i)
    return _final_norm(h, p["norm_final"][None, :]).reshape(batch, seq, d)


def kernel(x_prompt, x_sample, conv_w_in, conv_w_dw, conv_w_out, mla_w_in, mla_q_norm, mla_w_qb, mla_kv_norm, mla_w_kvb, mla_w_out, ml_w_in, ml_b_gates, ml_head_norm, ml_w_out, norm_mix, norm_ffn, router_w, exp_w_gate, exp_w_up, exp_w_down, norm_final):
    mla = [_prep_mla(mla_w_in[j], mla_w_qb[j], mla_w_kvb[j]) for j in range(mla_w_in.shape[0])]
    p = dict(
        conv_w_in=conv_w_in.astype(BF16), conv_w_dw=conv_w_dw, conv_w_out=conv_w_out.astype(BF16),
        mla_w_in=[m[0] for m in mla], mla_q_norm=mla_q_norm, mla_w_qb=[m[1] for m in mla],
        mla_kv_norm=mla_kv_norm, mla_w_k=[m[2] for m in mla], mla_w_vt=[m[3] for m in mla],
        mla_w_out=mla_w_out.astype(BF16),
        ml_w_in=[_prep_ml_in(ml_w_in[j]) for j in range(ml_w_in.shape[0])], ml_b_gates=ml_b_gates,
        ml_head_norm=ml_head_norm, ml_w_out=ml_w_out.astype(BF16),
        norm_mix=norm_mix, norm_ffn=norm_ffn, router_w=router_w,
        exp_w_gate=exp_w_gate, exp_w_up=exp_w_up, exp_w_down=exp_w_down, norm_final=norm_final)
    return (_trunk(x_prompt, p), _trunk(x_sample, p))
```

```python
import functools

import jax
import jax.numpy as jnp
from jax import lax
from jax.experimental import pallas as pl
from jax.experimental.pallas import tpu as pltpu

F32 = jnp.float32
BF16 = jnp.bfloat16

D_MODEL = 1024
DEPTH = 4
N_MIXERS = 3
EPS = 1e-6
CONV_WIDTH = 3
MLA_HEADS = 8
Q_LORA = 384
KV_LORA = 256
QK_NOPE = 128
QK_ROPE = 64
V_HEAD = 128
ROPE_THETA = 10000.0
ML_HEADS = 8
ML_QK = 64
ML_V = 128
N_EXPERTS = 16
EC_CAPACITY = 2
D_EXPERT = 1024

LANES_V7X = 128
BF16_SUBLANES_V7X = 16
VMEM_BYTES_V7X = 64 * 1024 * 1024
VMEM_LIMIT_BYTES = VMEM_BYTES_V7X - 8 * 1024 * 1024

TOKEN_TILE = 512
MLA_TOKEN_TILE = 1024
ATTN_Q_TILE = 1024
ML_CHUNK_TILE = 256
COMBINE_WINDOW = 256
GATHER_ROWS = 32768
GATHER_OPERAND_ROWS = 32768

MLA_HEAD_PAD = 2 * LANES_V7X
VT_ROWS = V_HEAD + BF16_SUBLANES_V7X
LOG2_E = 1.4426950408889634
ML_QK_PAD = LANES_V7X


def _params(*sem):
    return pltpu.CompilerParams(dimension_semantics=sem, vmem_limit_bytes=VMEM_LIMIT_BYTES)


def _rms(x, g):
    ms = jnp.mean(x * x, axis=-1, keepdims=True)
    return x * lax.rsqrt(ms + EPS) * g


def _tile(n, pref):
    t = min(n, pref)
    assert n % t == 0, (n, t)
    return t


def _conv_in_kernel(x_ref, g_ref, w_ref, u_ref, gb_ref):
    d = D_MODEL
    xn = _rms(x_ref[...], g_ref[...]).astype(BF16)
    gb = jnp.dot(xn, w_ref[:, 0:d], preferred_element_type=F32)
    gc = jnp.dot(xn, w_ref[:, d:2 * d], preferred_element_type=F32)
    xv = jnp.dot(xn, w_ref[:, 2 * d:3 * d], preferred_element_type=F32)
    gb_ref[...] = gb.astype(BF16)
    u_ref[...] = (gc * xv).astype(BF16)


def _conv_in(x, g, w):
    n, d = x.shape
    tm = _tile(n, TOKEN_TILE)
    return pl.pallas_call(
        _conv_in_kernel,
        grid=(n // tm,),
        in_specs=[pl.BlockSpec((tm, d), lambda i: (i, 0)),
                  pl.BlockSpec((1, d), lambda i: (0, 0)),
                  pl.BlockSpec((d, 3 * d), lambda i: (0, 0))],
        out_specs=[pl.BlockSpec((tm, d), lambda i: (i, 0)),
                   pl.BlockSpec((tm, d), lambda i: (i, 0))],
        out_shape=[jax.ShapeDtypeStruct((n, d), BF16), jax.ShapeDtypeStruct((n, d), BF16)],
        compiler_params=_params("parallel"),
        name="conv_in",
    )(x, g, w)


def _conv_out_kernel(u_ref, up_ref, un_ref, gb_ref, h_ref, wdw_ref, w_ref, o_ref, *, tiles_per_seq):
    tm = u_ref.shape[0]
    pos = pl.program_id(0) % tiles_per_seq
    u = u_ref[...].astype(F32)
    halo = BF16_SUBLANES_V7X
    prev_row = jnp.where(pos == 0, 0.0, up_ref[halo - 1:halo, :].astype(F32))
    next_row = jnp.where(pos == tiles_per_seq - 1, 0.0, un_ref[0:1, :].astype(F32))
    row = lax.broadcasted_iota(jnp.int32, (tm, 1), 0)
    u_up = jnp.where(row == 0, prev_row, pltpu.roll(u, 1, axis=0))
    u_dn = jnp.where(row == tm - 1, next_row, pltpu.roll(u, tm - 1, axis=0))
    conv = u_up * wdw_ref[0:1, :] + u * wdw_ref[1:2, :] + u_dn * wdw_ref[2:3, :]
    g = (gb_ref[...].astype(F32) * conv).astype(BF16)
    o_ref[...] = h_ref[...] + jnp.dot(g, w_ref[...], preferred_element_type=F32)


def _conv_out(u, gb, h, w_dw, w_out, seq):
    n, d = u.shape
    tm = _tile(seq, TOKEN_TILE)
    halo = BF16_SUBLANES_V7X
    r = tm // halo
    nblk = n // halo
    return pl.pallas_call(
        functools.partial(_conv_out_kernel, tiles_per_seq=seq // tm),
        grid=(n // tm,),
        in_specs=[pl.BlockSpec((tm, d), lambda i: (i, 0)),
                  pl.BlockSpec((halo, d), lambda i: (jnp.maximum(i * r - 1, 0), 0)),
                  pl.BlockSpec((halo, d), lambda i: (jnp.minimum((i + 1) * r, nblk - 1), 0)),
                  pl.BlockSpec((tm, d), lambda i: (i, 0)),
                  pl.BlockSpec((tm, d), lambda i: (i, 0)),
                  pl.BlockSpec((CONV_WIDTH, d), lambda i: (0, 0)),
                  pl.BlockSpec((d, d), lambda i: (0, 0))],
        out_specs=pl.BlockSpec((tm, d), lambda i: (i, 0)),
        out_shape=jax.ShapeDtypeStruct((n, d), F32),
        compiler_params=_params("parallel"),
        name="conv_out",
    )(u, u, u, gb, h, w_dw, w_out)


def _rope_group(x, c, s):
    return x * c + pltpu.roll(x, LANES_V7X // 2, axis=1) * s


def _mla_in_kernel(x_ref, g_ref, win_ref, qn_ref, kvn_ref, wqb_ref, wk_ref, wvt_ref, cos_ref, sin_ref,
                   q_ref, k_ref, vt_ref):
    hp = MLA_HEAD_PAD
    tm = x_ref.shape[0]
    scale = (QK_NOPE + QK_ROPE) ** -0.5 * LOG2_E
    xn = _rms(x_ref[...], g_ref[...]).astype(BF16)
    lat = jnp.dot(xn, win_ref[...], preferred_element_type=F32)
    qn = _rms(lat[:, 0:Q_LORA], qn_ref[...]).astype(BF16)
    kvn = _rms(lat[:, Q_LORA:Q_LORA + KV_LORA], kvn_ref[...]).astype(BF16)
    c = cos_ref[...]
    s = sin_ref[...]
    k_pe = _rope_group(lat[:, Q_LORA + KV_LORA:], c, s).astype(BF16)
    extra = lax.broadcasted_iota(jnp.int32, (VT_ROWS - V_HEAD, tm), 0)
    ones_rows = jnp.where(extra == 0, 1.0, 0.0).astype(BF16)
    for h in range(MLA_HEADS):
        qh = jnp.dot(qn, wqb_ref[:, h * hp:(h + 1) * hp], preferred_element_type=F32)
        q_ref[:, h * hp:h * hp + QK_NOPE] = (qh[:, 0:QK_NOPE] * scale).astype(BF16)
        q_ref[:, h * hp + QK_NOPE:(h + 1) * hp] = (_rope_group(qh[:, QK_NOPE:], c, s) * scale).astype(BF16)
        kh = jnp.dot(kvn, wk_ref[:, h * QK_NOPE:(h + 1) * QK_NOPE], preferred_element_type=F32)
        k_ref[:, h * hp:h * hp + QK_NOPE] = kh.astype(BF16)
        k_ref[:, h * hp + QK_NOPE:(h + 1) * hp] = k_pe
        vt = lax.dot_general(wvt_ref[h * V_HEAD:(h + 1) * V_HEAD, :], kvn, (((1,), (1,)), ((), ())),
                             preferred_element_type=F32)
        vt_ref[0, h, 0:V_HEAD, :] = vt.astype(BF16)
        vt_ref[0, h, V_HEAD:VT_ROWS, :] = ones_rows


def _mla_in(x, g, w_in, q_norm, w_qb, kv_norm, w_k, w_vt, cos, sin, seq):
    n, d = x.shape
    tm = _tile(seq, MLA_TOKEN_TILE)
    tps = seq // tm
    hq = MLA_HEADS * MLA_HEAD_PAD
    const = lambda i: (0, 0)
    return pl.pallas_call(
        _mla_in_kernel,
        grid=(n // tm,),
        in_specs=[pl.BlockSpec((tm, d), lambda i: (i, 0)),
                  pl.BlockSpec((1, d), const),
                  pl.BlockSpec(w_in.shape, const),
                  pl.BlockSpec((1, Q_LORA), const),
                  pl.BlockSpec((1, KV_LORA), const),
                  pl.BlockSpec(w_qb.shape, const),
                  pl.BlockSpec(w_k.shape, const),
                  pl.BlockSpec(w_vt.shape, const),
                  pl.BlockSpec((tm, LANES_V7X), lambda i: (i % tps, 0)),
                  pl.BlockSpec((tm, LANES_V7X), lambda i: (i % tps, 0))],
        out_specs=[pl.BlockSpec((tm, hq), lambda i: (i, 0)),
                   pl.BlockSpec((tm, hq), lambda i: (i, 0)),
                   pl.BlockSpec((1, MLA_HEADS, VT_ROWS, tm), lambda i: (i, 0, 0, 0))],
        out_shape=[jax.ShapeDtypeStruct((n, hq), BF16),
                   jax.ShapeDtypeStruct((n, hq), BF16),
                   jax.ShapeDtypeStruct((n // tm, MLA_HEADS, VT_ROWS, tm), BF16)],
        compiler_params=_params("parallel"),
        name="mla_in",
    )(x, g, w_in, q_norm, kv_norm, w_qb, w_k, w_vt, cos, sin)


def _attn_kernel(q_ref, k_ref, vt_ref, o_ref, s0, s1, p0, p1, acc_ref):
    tq = q_ref.shape[0]
    nk, _, _, tk = vt_ref.shape
    assert nk % 2 == 0
    q = q_ref[...]

    def scores(j, s_ref):
        start = pl.multiple_of(j * tk, tk)
        k = k_ref[pl.ds(start, tk), :]
        st = lax.dot_general(k, q, (((1,), (1,)), ((), ())), preferred_element_type=F32)
        s_ref[...] = st
        return jnp.max(st, axis=0, keepdims=True)

    def exps(s_ref, p_ref, m, tile_max):
        m_new = jnp.maximum(m, tile_max)
        p_ref[...] = jnp.exp2(s_ref[...] - m_new).astype(BF16)
        return m_new, jnp.exp2(m - m_new)

    def values(j, p_ref, alpha):
        acc_ref[...] = alpha * acc_ref[...] + jnp.dot(vt_ref[j, 0], p_ref[...], preferred_element_type=F32)

    acc_ref[...] = jnp.zeros_like(acc_ref)
    x0 = scores(0, s0)
    x1 = scores(1, s1)
    m, alpha = exps(s0, p0, jnp.full((1, tq), -jnp.inf, F32), x0)

    steps = nk - 2
    unroll = max(u for u in (2, 4, 6, 8, 10) if steps % u == 0) if steps else 2

    def group(i, carry):
        m, alpha, x1 = carry
        for t in range(0, unroll, 2):
            s = unroll * i + t + 1
            x0 = scores(s + 1, s0)
            values(s - 1, p0, alpha)
            m, alpha = exps(s1, p1, m, x1)
            x1 = scores(s + 2, s1)
            values(s, p1, alpha)
            m, alpha = exps(s0, p0, m, x0)
        return m, alpha, x1

    m, alpha, x1 = lax.fori_loop(0, steps // unroll, group, (m, alpha, x1))
    values(nk - 2, p0, alpha)
    m, alpha = exps(s1, p1, m, x1)
    values(nk - 1, p1, alpha)
    acc = acc_ref[...]
    out_t = acc[0:V_HEAD, :] / acc[V_HEAD:V_HEAD + 1, :]
    o_ref[...] = out_t.T.astype(BF16)


def _attention(q, k, vt, batch, seq):
    n = q.shape[0]
    tq = _tile(seq, ATTN_Q_TILE)
    tk = vt.shape[-1]
    nq = seq // tq
    nk = seq // tk
    hp = MLA_HEAD_PAD
    return pl.pallas_call(
        _attn_kernel,
        grid=(batch, MLA_HEADS, nq),
        in_specs=[pl.BlockSpec((tq, hp), lambda b, h, i: (b * nq + i, h)),
                  pl.BlockSpec((seq, hp), lambda b, h, i: (b, h)),
                  pl.BlockSpec((nk, 1, VT_ROWS, tk), lambda b, h, i: (b, h, 0, 0))],
        out_specs=pl.BlockSpec((tq, V_HEAD), lambda b, h, i: (b * nq + i, h)),
        out_shape=jax.ShapeDtypeStruct((n, MLA_HEADS * V_HEAD), BF16),
        scratch_shapes=[pltpu.VMEM((tk, tq), F32), pltpu.VMEM((tk, tq), F32),
                        pltpu.VMEM((tk, tq), BF16), pltpu.VMEM((tk, tq), BF16),
                        pltpu.VMEM((VT_ROWS, tq), F32)],
        compiler_params=_params("parallel", "parallel", "arbitrary"),
        name="mla_attention",
    )(q, k, vt)


def _proj_residual_kernel(a_ref, h_ref, w_ref, o_ref):
    o_ref[...] = h_ref[...] + jnp.dot(a_ref[...], w_ref[...], preferred_element_type=F32)


def _proj_residual(a, h, w):
    n, d = h.shape
    tm = _tile(n, TOKEN_TILE)
    return pl.pallas_call(
        _proj_residual_kernel,
        grid=(n // tm,),
        in_specs=[pl.BlockSpec((tm, a.shape[1]), lambda i: (i, 0)),
                  pl.BlockSpec((tm, d), lambda i: (i, 0)),
                  pl.BlockSpec(w.shape, lambda i: (0, 0))],
        out_specs=pl.BlockSpec((tm, d), lambda i: (i, 0)),
        out_shape=jax.ShapeDtypeStruct((n, d), F32),
        compiler_params=_params("parallel"),
        name="proj_residual",
    )(a, h, w)


def _ml_in_kernel(x_ref, g_ref, w_ref, b_ref, q_ref, k_ref, v_ref, o_ref, gt_ref):
    qw = ML_HEADS * ML_QK_PAD
    vw = ML_HEADS * ML_V
    xn = _rms(x_ref[...], g_ref[...]).astype(BF16)
    q = jnp.dot(xn, w_ref[:, 0:qw], preferred_element_type=F32)
    q_ref[...] = (q * (ML_QK ** -0.5)).astype(BF16)
    k_ref[...] = jnp.dot(xn, w_ref[:, qw:2 * qw], preferred_element_type=F32).astype(BF16)
    v_ref[...] = jnp.dot(xn, w_ref[:, 2 * qw:2 * qw + vw], preferred_element_type=F32).astype(BF16)
    o_ref[...] = jnp.dot(xn, w_ref[:, 2 * qw + vw:2 * qw + 2 * vw], preferred_element_type=F32).astype(BF16)
    gt_ref[...] = jnp.dot(xn, w_ref[:, 2 * qw + 2 * vw:], preferred_element_type=F32) + b_ref[...]


def _ml_in(x, g, w, b_gates):
    n, d = x.shape
    tm = _tile(n, TOKEN_TILE)
    qw = ML_HEADS * ML_QK_PAD
    vw = ML_HEADS * ML_V
    ng = 4 * ML_HEADS
    row = lambda i: (i, 0)
    const = lambda i: (0, 0)
    return pl.pallas_call(
        _ml_in_kernel,
        grid=(n // tm,),
        in_specs=[pl.BlockSpec((tm, d), row), pl.BlockSpec((1, d), const),
                  pl.BlockSpec(w.shape, const), pl.BlockSpec((1, ng), const)],
        out_specs=[pl.BlockSpec((tm, qw), row), pl.BlockSpec((tm, qw), row),
                   pl.BlockSpec((tm, vw), row), pl.BlockSpec((tm, vw), row),
                   pl.BlockSpec((tm, ng), row)],
        out_shape=[jax.ShapeDtypeStruct((n, qw), BF16), jax.ShapeDtypeStruct((n, qw), BF16),
                   jax.ShapeDtypeStruct((n, vw), BF16), jax.ShapeDtypeStruct((n, vw), BF16),
                   jax.ShapeDtypeStruct((n, ng), F32)],
        compiler_params=_params("parallel"),
        name="mlstm_in",
    )(x, g, w, b_gates)


def _split3(x):
    hi = x.astype(BF16)
    r1 = x - hi.astype(F32)
    mid = r1.astype(BF16)
    lo = (r1 - mid.astype(F32)).astype(BF16)
    return hi, mid, lo


def _mlstm_kernel(q_ref, k_ref, vt_ref, gc_ref, gr_ref, o_ref, c_scr, m_scr):
    lc = q_ref.shape[0]
    nh = ML_HEADS
    kp = ML_QK_PAD
    dv = ML_V
    fwd = pl.program_id(0) == 0

    @pl.when(pl.program_id(2) == 0)
    def _():
        c_scr[...] = jnp.zeros_like(c_scr)
        m_scr[...] = jnp.zeros_like(m_scr)

    gcol = gc_ref[0]
    grow = gr_ref[0]
    li_col = gcol[:, 0:nh]
    lf_col = jax.nn.log_sigmoid(gcol[:, nh:2 * nh])
    li_row = grow[0:nh, :]
    lf_row = jax.nn.log_sigmoid(grow[nh:2 * nh, :])

    r = lax.broadcasted_iota(jnp.int32, (lc, lc), 0)
    cidx = lax.broadcasted_iota(jnp.int32, (lc, lc), 1)
    sign = jnp.where(fwd, 1, -1)
    tri = jnp.where((r - cidx) * sign >= 0, 1.0, 0.0).astype(BF16)
    allowed_t = (cidx - r) * sign >= 0

    b_col = jnp.zeros((lc, nh), F32)
    for piece in _split3(lf_col):
        b_col = b_col + jnp.dot(tri, piece, preferred_element_type=F32)
    b_row = jnp.zeros((nh, lc), F32)
    for piece in _split3(lf_row):
        b_row = b_row + lax.dot_general(piece, tri, (((1,), (1,)), ((), ())), preferred_element_type=F32)
    a_all = jnp.sum(lf_row, axis=-1, keepdims=True)
    src_col = li_col - b_col

    sub = lax.broadcasted_iota(jnp.int32, (dv, lc), 0)
    ones_rows = jnp.where(sub == 0, 1.0, 0.0).astype(BF16)
    nt = (((1,), (1,)), ((), ()))

    for h in range(nh):
        q = q_ref[:, h * kp:(h + 1) * kp]
        k = k_ref[:, h * kp:(h + 1) * kp]
        vt_ext = jnp.concatenate([vt_ref[h * dv:(h + 1) * dv, :], ones_rows], axis=0)
        b_r = b_row[h:h + 1, :]
        li_r = li_row[h:h + 1, :]
        a = a_all[h:h + 1, :]
        m_in = m_scr[h][0:1, 0:1]
        ct_ext = c_scr[h]

        w_end = a - b_r + li_r
        g = jnp.max(w_end, axis=-1, keepdims=True)
        e_r = jnp.exp(w_end - g)

        dmat = jnp.where(allowed_t, b_r + src_col[:, h:h + 1], -jnp.inf)
        inter_log = b_r + m_in
        m_j = jnp.maximum(inter_log, jnp.max(dmat, axis=0, keepdims=True))
        inter = jnp.exp(inter_log - m_j)
        p = jnp.exp(dmat - m_j)
        s = lax.dot_general(k, q, nt, preferred_element_type=F32)
        qk = (s * p).astype(BF16)
        nd = (jnp.dot(vt_ext, qk, preferred_element_type=F32)
              + inter * lax.dot_general(ct_ext.astype(BF16), q, nt, preferred_element_type=F32))
        den = nd[dv:dv + 1, :]
        out_t = nd[0:dv, :] / jnp.maximum(jnp.abs(den), jnp.exp(-m_j))
        o_ref[0, :, h * dv:(h + 1) * dv] = out_t.T

        m_new = jnp.maximum(a + m_in, g)
        fdec = jnp.exp(a + m_in - m_new)
        iin = jnp.exp(g - m_new)
        vte = (vt_ext.astype(F32) * e_r).astype(BF16)
        c_scr[h] = fdec * ct_ext + iin * jnp.dot(vte, k, preferred_element_type=F32)
        m_scr[h] = jnp.broadcast_to(m_new, m_scr.shape[1:])


def _mlstm(q, k, vt, gcol, grow, batch, seq):
    n = q.shape[0]
    lc = _tile(seq, ML_CHUNK_TILE)
    nc = seq // lc
    qw = ML_HEADS * ML_QK_PAD
    vw = ML_HEADS * ML_V

    def chunk(d, b, c):
        return b * nc + c + d * (nc - 1 - 2 * c)

    return pl.pallas_call(
        _mlstm_kernel,
        grid=(2, batch, nc),
        in_specs=[pl.BlockSpec((lc, qw), lambda d, b, c: (chunk(d, b, c), 0)),
                  pl.BlockSpec((lc, qw), lambda d, b, c: (chunk(d, b, c), 0)),
                  pl.BlockSpec((vw, lc), lambda d, b, c: (0, chunk(d, b, c))),
                  pl.BlockSpec((1, lc, 2 * ML_HEADS), lambda d, b, c: (d, chunk(d, b, c), 0)),
                  pl.BlockSpec((1, 2 * ML_HEADS, lc), lambda d, b, c: (d, 0, chunk(d, b, c)))],
        out_specs=pl.BlockSpec((1, lc, vw), lambda d, b, c: (d, chunk(d, b, c), 0)),
        out_shape=jax.ShapeDtypeStruct((2, n, vw), F32),
        scratch_shapes=[pltpu.VMEM((ML_HEADS, 2 * ML_V, ML_QK_PAD), F32),
                        pltpu.VMEM((ML_HEADS, 8, LANES_V7X), F32)],
        compiler_params=_params("parallel", "parallel", "arbitrary"),
        name="mlstm_scan",
    )(q, k, vt, gcol, grow)


def _ml_out_kernel(hs_ref, og_ref, hn_ref, h_ref, w_ref, o_ref):
    dv = ML_V
    hs = hs_ref[0] + hs_ref[1]
    parts = []
    for hd in range(ML_HEADS):
        x = hs[:, hd * dv:(hd + 1) * dv]
        parts.append(x * lax.rsqrt(jnp.mean(x * x, axis=-1, keepdims=True) + EPS))
    y = jnp.concatenate(parts, axis=1) * hn_ref[...] * jax.nn.sigmoid(og_ref[...].astype(F32))
    o_ref[...] = h_ref[...] + jnp.dot(y.astype(BF16), w_ref[...], preferred_element_type=F32)


def _ml_out(hs, og, head_norm, h, w):
    n, d = h.shape
    vw = ML_HEADS * ML_V
    tm = _tile(n, TOKEN_TILE)
    row = lambda i: (i, 0)
    const = lambda i: (0, 0)
    return pl.pallas_call(
        _ml_out_kernel,
        grid=(n // tm,),
        in_specs=[pl.BlockSpec((2, tm, vw), lambda i: (0, i, 0)),
                  pl.BlockSpec((tm, vw), row), pl.BlockSpec((1, vw), const),
                  pl.BlockSpec((tm, d), row), pl.BlockSpec(w.shape, const)],
        out_specs=pl.BlockSpec((tm, d), row),
        out_shape=jax.ShapeDtypeStruct((n, d), F32),
        compiler_params=_params("parallel"),
        name="mlstm_out",
    )(hs, og, head_norm, h, w)


def _router_kernel(x_ref, g_ref, w_ref, wlo_ref, xn_ref, aff_ref, *, nreal):
    @pl.when(pl.program_id(0) < nreal)
    def _():
        xf = _rms(x_ref[...], g_ref[...])
        xn = xf.astype(BF16)
        xn_ref[...] = xn
        xlo = (xf - xn.astype(F32)).astype(BF16)
        logits = (jnp.dot(xn, w_ref[...], preferred_element_type=F32)
                  + jnp.dot(xlo, w_ref[...], preferred_element_type=F32)
                  + jnp.dot(xn, wlo_ref[...], preferred_element_type=F32))
        z = jnp.exp(logits - jnp.max(logits, axis=-1, keepdims=True))
        aff_ref[...] = z / jnp.sum(z, axis=-1, keepdims=True)

    @pl.when(pl.program_id(0) >= nreal)
    def _():
        xn_ref[...] = jnp.zeros_like(xn_ref)


def _router(x, g, w_f32):
    n, d = x.shape
    tm = _tile(n, TOKEN_TILE)
    nreal = n // tm
    npad = max(n, GATHER_OPERAND_ROWS)
    row = lambda i: (jnp.minimum(i, nreal - 1), 0)
    const = lambda i: (0, 0)
    w = w_f32.astype(BF16)
    w_lo = (w_f32 - w.astype(F32)).astype(BF16)
    return pl.pallas_call(
        functools.partial(_router_kernel, nreal=nreal),
        grid=(npad // tm,),
        in_specs=[pl.BlockSpec((tm, d), row), pl.BlockSpec((1, d), const),
                  pl.BlockSpec(w.shape, const), pl.BlockSpec(w.shape, const)],
        out_specs=[pl.BlockSpec((tm, d), lambda i: (i, 0)), pl.BlockSpec((tm, N_EXPERTS), row)],
        out_shape=[jax.ShapeDtypeStruct((npad, d), BF16), jax.ShapeDtypeStruct((n, N_EXPERTS), F32)],
        compiler_params=_params("arbitrary"),
        name="moe_router",
    )(x, g, w, w_lo)


def _ffn_kernel(*refs, nchunk, epc):
    x_refs = refs[:nchunk]
    gate_ref, wg_ref, wu_ref, wd_ref, o_ref = refs[nchunk:]

    def compute(x):
        g = jnp.dot(x, wg_ref[0, 0], preferred_element_type=F32)
        u = jnp.dot(x, wu_ref[0, 0], preferred_element_type=F32)
        hid = (g * jax.nn.sigmoid(g) * u).astype(BF16)
        y = jnp.dot(hid, wd_ref[0, 0], preferred_element_type=F32)
        o_ref[0] = (y * gate_ref[0]).astype(BF16)

    if nchunk == 1:
        compute(x_refs[0][0])
    else:
        chunk = pl.program_id(0) // epc
        for c in range(nchunk):
            @pl.when(chunk == c)
            def _(c=c):
                compute(x_refs[c][0])


def _expert_ffn(xes, gate, wg, wu, wd, layer):
    nchunk = len(xes)
    epc, cap, d = xes[0].shape
    e = nchunk * epc
    f = wg.shape[-1]
    tm = _tile(cap, TOKEN_TILE)
    nj = cap // tm
    tok = lambda i, j: (i, j, 0)
    wmap = lambda i, j: (layer, i, 0, 0)

    def chunk_map(c):
        def index(i, j):
            jj = jnp.where(i < c * epc, 0, jnp.where(i >= (c + 1) * epc, nj - 1, j))
            return (jnp.clip(i - c * epc, 0, epc - 1), jj, 0)
        return index

    return pl.pallas_call(
        functools.partial(_ffn_kernel, nchunk=nchunk, epc=epc),
        grid=(e, nj),
        in_specs=[pl.BlockSpec((1, tm, d), chunk_map(c)) for c in range(nchunk)]
        + [pl.BlockSpec((1, tm, 1), tok),
           pl.BlockSpec((1, 1, d, f), wmap), pl.BlockSpec((1, 1, d, f), wmap),
           pl.BlockSpec((1, 1, f, d), wmap)],
        out_specs=pl.BlockSpec((1, tm, d), tok),
        out_shape=jax.ShapeDtypeStruct((e, cap, d), BF16),
        compiler_params=_params("parallel", "arbitrary"),
        name="moe_ffn",
    )(*xes, gate, wg, wu, wd)


def _combine_kernel(lo_ref, h_ref, idx_ref, ye_hbm, o_ref, buf, sem):
    tb, d = h_ref.shape
    ne, nrow, lanes = idx_ref.shape
    cap = nrow * lanes
    win = COMBINE_WINDOW
    b = pl.program_id(0)
    nb = pl.num_programs(0)
    tok = b * tb + lax.broadcasted_iota(jnp.int32, (tb, 1), 0)
    lane = lax.broadcasted_iota(jnp.int32, (1, win), 1)

    def first_start(blk, e):
        return jnp.minimum(lo_ref[e, blk] // lanes * lanes, cap - win)

    def fetch(e, start, slot):
        return pltpu.make_async_copy(ye_hbm.at[e, pl.ds(pl.multiple_of(start, lanes), win), :],
                                     buf.at[slot], sem.at[slot])

    def expand(e, start, row_lo, row_hi, slot):
        r0 = start // lanes
        ids = jnp.concatenate([idx_ref[e, pl.ds(r0 + k, 1), :] for k in range(win // lanes)], axis=1)
        rid = start + lane
        ids = jnp.where((rid >= row_lo) & (rid < row_hi), ids, -1)
        sel = jnp.where(tok == ids, 1.0, 0.0).astype(BF16)
        return jnp.dot(sel, buf[slot], preferred_element_type=F32)

    @pl.when(b == 0)
    def _():
        for e in range(ne):
            fetch(e, first_start(0, e), e).start()

    o_ref[...] = h_ref[...]
    group = 4
    total = None
    for e0 in range(0, ne, group):
        first = []
        for e in range(e0, e0 + group):
            lo = lo_ref[e, b]
            hi = lo_ref[e, b + 1]
            start0 = lo // lanes * lanes
            first.append((e, lo, hi, start0, jnp.minimum(start0, cap - win)))
        for e, lo, hi, start0, start in first:
            fetch(e, start, e).wait()
        for e, lo, hi, start0, start in first:
            part = expand(e, start, lo, hi, e)
            total = part if total is None else total + part

        for e, lo, hi, start0, start in first:
            def more(w, carry, e=e, lo=lo, hi=hi, start0=start0):
                nominal = start0 + w * win
                st = jnp.minimum(nominal, cap - win)
                cp = fetch(e, st, e)
                cp.start()
                cp.wait()
                o_ref[...] += expand(e, st, jnp.maximum(lo, nominal), hi, e)
                return carry

            nwin = (hi - start0 + win - 1) // win
            lax.fori_loop(1, nwin, more, 0)

            @pl.when(b + 1 < nb)
            def _(e=e):
                fetch(e, first_start(b + 1, e), e).start()

    o_ref[...] += total


def _combine(h, idx, ye):
    n, d = h.shape
    e, cap = idx.shape
    assert cap >= COMBINE_WINDOW and cap % LANES_V7X == 0 and e % 4 == 0
    tb = _tile(n, TOKEN_TILE)
    nb = n // tb
    bounds = jnp.arange(nb + 1, dtype=jnp.int32) * tb
    lo = jnp.sum(idx[:, :, None] < bounds[None, None, :], axis=1, dtype=jnp.int32)
    idx3 = idx.reshape(e, cap // LANES_V7X, LANES_V7X)
    grid_spec = pltpu.PrefetchScalarGridSpec(
        num_scalar_prefetch=1,
        grid=(nb,),
        in_specs=[pl.BlockSpec((tb, d), lambda i, lo_ref: (i, 0)),
                  pl.BlockSpec(idx3.shape, lambda i, lo_ref: (0, 0, 0)),
                  pl.BlockSpec(memory_space=pl.ANY)],
        out_specs=pl.BlockSpec((tb, d), lambda i, lo_ref: (i, 0)),
        scratch_shapes=[pltpu.VMEM((e, COMBINE_WINDOW, d), BF16), pltpu.SemaphoreType.DMA((e,))])
    return pl.pallas_call(
        _combine_kernel,
        grid_spec=grid_spec,
        out_shape=jax.ShapeDtypeStruct((n, d), F32),
        compiler_params=_params("arbitrary"),
        name="moe_combine",
    )(lo, h, idx3, ye)


def _moe(h, g, w_router, wg, wu, wd, layer):
    n, d = h.shape
    cap = EC_CAPACITY * n // N_EXPERTS
    xn, aff = _router(h, g, w_router)
    gate, idx = lax.top_k(aff.T, cap)
    idx, gate = lax.sort((idx, gate), dimension=1, num_keys=1)
    nchunk = max(1, idx.size // GATHER_ROWS)
    epc = N_EXPERTS // nchunk
    xes = [xn[idx[c * epc:(c + 1) * epc]] for c in range(nchunk)]
    ye = _expert_ffn(xes, gate[..., None], wg, wu, wd, layer)
    return _combine(h, idx, ye)


def _final_norm_kernel(x_ref, g_ref, o_ref):
    o_ref[...] = _rms(x_ref[...], g_ref[...])


def _final_norm(x, g):
    n, d = x.shape
    tm = _tile(n, TOKEN_TILE)
    return pl.pallas_call(
        _final_norm_kernel,
        grid=(n // tm,),
        in_specs=[pl.BlockSpec((tm, d), lambda i: (i, 0)), pl.BlockSpec((1, d), lambda i: (0, 0))],
        out_specs=pl.BlockSpec((tm, d), lambda i: (i, 0)),
        out_shape=jax.ShapeDtypeStruct((n, d), F32),
        compiler_params=_params("parallel"),
        name="final_norm",
    )(x, g)


def _rope_group_cols(w_pe):
    half = QK_ROPE // 2
    z = jnp.zeros((w_pe.shape[0], LANES_V7X // 2 - half), w_pe.dtype)
    return jnp.concatenate([w_pe[:, :half], z, w_pe[:, half:], z], axis=1)


def _rope_tables(seq):
    half = QK_ROPE // 2
    pos = jnp.arange(seq, dtype=F32)
    inv = ROPE_THETA ** (-jnp.arange(0, QK_ROPE, 2, dtype=F32) / QK_ROPE)
    ang = pos[:, None] * inv[None, :]
    c, s = jnp.cos(ang), jnp.sin(ang)
    z = jnp.zeros((seq, LANES_V7X // 2 - half), F32)
    return jnp.concatenate([c, z, c, z], axis=1), jnp.concatenate([-s, z, s, z], axis=1)


def _prep_mla(w_in, w_qb, w_kvb):
    lat = Q_LORA + KV_LORA
    w_in_p = jnp.concatenate([w_in[:, :lat], _rope_group_cols(w_in[:, lat:])], axis=1).astype(BF16)
    hd = QK_NOPE + QK_ROPE
    cols = []
    for h in range(MLA_HEADS):
        cols.append(w_qb[:, h * hd:h * hd + QK_NOPE])
        cols.append(_rope_group_cols(w_qb[:, h * hd + QK_NOPE:(h + 1) * hd]))
    w_kv = w_kvb.reshape(KV_LORA, MLA_HEADS, QK_NOPE + V_HEAD)
    w_k = w_kv[:, :, :QK_NOPE].reshape(KV_LORA, MLA_HEADS * QK_NOPE).astype(BF16)
    w_vt = w_kv[:, :, QK_NOPE:].reshape(KV_LORA, MLA_HEADS * V_HEAD).T.astype(BF16)
    return w_in_p, jnp.concatenate(cols, axis=1).astype(BF16), w_k, w_vt


def _prep_ml_in(w):
    qd = ML_HEADS * ML_QK
    cols = []
    for base in (0, qd):
        for h in range(ML_HEADS):
            cols.append(w[:, base + h * ML_QK:base + (h + 1) * ML_QK])
            cols.append(jnp.zeros((w.shape[0], ML_QK_PAD - ML_QK), w.dtype))
    cols.append(w[:, 2 * qd:])
    return jnp.concatenate(cols, axis=1).astype(BF16)


def _trunk(x, p):
    batch, seq, d = x.shape
    n = batch * seq
    h = x.reshape(n, d)
    for i in range(DEPTH):
        j = i // N_MIXERS
        kind = i % N_MIXERS
        g_mix = p["norm_mix"][i][None, :]
        if kind == 0:
            u, gb = _conv_in(h, g_mix, p["conv_w_in"][j])
            h = _conv_out(u, gb, h, p["conv_w_dw"][j], p["conv_w_out"][j], seq)
        elif kind == 1:
            cos, sin = _rope_tables(seq)
            q, k, vt = _mla_in(h, g_mix, p["mla_w_in"][j], p["mla_q_norm"][j][None, :], p["mla_w_qb"][j],
                               p["mla_kv_norm"][j][None, :], p["mla_w_k"][j], p["mla_w_vt"][j], cos, sin, seq)
            o = _attention(q, k, vt, batch, seq)
            h = _proj_residual(o, h, p["mla_w_out"][j])
        else:
            q, k, v, og, gates = _ml_in(h, g_mix, p["ml_w_in"][j], p["ml_b_gates"][j][None, :])
            gcol = gates.reshape(n, 2, 2 * ML_HEADS).transpose(1, 0, 2)
            grow = gcol.transpose(0, 2, 1)
            hs = _mlstm(q, k, v.T, gcol, grow, batch, seq)
            h = _ml_out(hs, og, p["ml_head_norm"][j][None, :], h, p["ml_w_out"][j])
        h = _moe(h, p["norm_ffn"][i][None, :], p["router_w"][i],
                 p["exp_w_gate"], p["exp_w_up"], p["exp_w_down"], i)
    return _final_norm(h, p["norm_final"][None, :]).reshape(batch, seq, d)


def kernel(x_prompt, x_sample, conv_w_in, conv_w_dw, conv_w_out, mla_w_in, mla_q_norm, mla_w_qb, mla_kv_norm, mla_w_kvb, mla_w_out, ml_w_in, ml_b_gates, ml_head_norm, ml_w_out, norm_mix, norm_ffn, router_w, exp_w_gate, exp_w_up, exp_w_down, norm_final):
    mla = [_prep_mla(mla_w_in[j], mla_w_qb[j], mla_w_kvb[j]) for j in range(mla_w_in.shape[0])]
    p = dict(
        conv_w_in=conv_w_in.astype(BF16), conv_w_dw=conv_w_dw, conv_w_out=conv_w_out.astype(BF16),
        mla_w_in=[m[0] for m in mla], mla_q_norm=mla_q_norm, mla_w_qb=[m[1] for m in mla],
        mla_kv_norm=mla_kv_norm, mla_w_k=[m[2] for m in mla], mla_w_vt=[m[3] for m in mla],
        mla_w_out=mla_w_out.astype(BF16),
        ml_w_in=[_prep_ml_in(ml_w_in[j]) for j in range(ml_w_in.shape[0])], ml_b_gates=ml_b_gates,
        ml_head_norm=ml_head_norm, ml_w_out=ml_w_out.astype(BF16),
        norm_mix=norm_mix, norm_ffn=norm_ffn, router_w=router_w,
        exp_w_gate=exp_w_gate.astype(BF16), exp_w_up=exp_w_up.astype(BF16),
        exp_w_down=exp_w_down.astype(BF16), norm_final=norm_final)
    return (_trunk(x_prompt, p), _trunk(x_sample, p))
```

```python
import functools

import jax
import jax.numpy as jnp
from jax import lax
from jax.experimental import pallas as pl
from jax.experimental.pallas import tpu as pltpu

F32 = jnp.float32
BF16 = jnp.bfloat16

D_MODEL = 1024
DEPTH = 4
N_MIXERS = 3
EPS = 1e-6
CONV_WIDTH = 3
MLA_HEADS = 8
Q_LORA = 384
KV_LORA = 256
QK_NOPE = 128
QK_ROPE = 64
V_HEAD = 128
ROPE_THETA = 10000.0
ML_HEADS = 8
ML_QK = 64
ML_V = 128
N_EXPERTS = 16
EC_CAPACITY = 2
D_EXPERT = 1024

LANES_V7X = 128
BF16_SUBLANES_V7X = 16
VMEM_BYTES_V7X = 64 * 1024 * 1024
VMEM_LIMIT_BYTES = VMEM_BYTES_V7X - 8 * 1024 * 1024

TOKEN_TILE = 512
MLA_TOKEN_TILE = 1024
ATTN_Q_TILE = 1024
ML_CHUNK_TILE = 256
COMBINE_WINDOW = 128
GATHER_ROWS = 32768
GATHER_OPERAND_ROWS = 32768

MLA_HEAD_PAD = 2 * LANES_V7X
VT_ROWS = V_HEAD + BF16_SUBLANES_V7X
LOG2_E = 1.4426950408889634
ML_QK_PAD = LANES_V7X


def _params(*sem):
    return pltpu.CompilerParams(dimension_semantics=sem, vmem_limit_bytes=VMEM_LIMIT_BYTES)


def _rms(x, g):
    ms = jnp.mean(x * x, axis=-1, keepdims=True)
    return x * lax.rsqrt(ms + EPS) * g


def _tile(n, pref):
    t = min(n, pref)
    assert n % t == 0, (n, t)
    return t


def _conv_in_kernel(x_ref, g_ref, w_ref, u_ref, gb_ref):
    d = D_MODEL
    xn = _rms(x_ref[...], g_ref[...]).astype(BF16)
    gb = jnp.dot(xn, w_ref[:, 0:d], preferred_element_type=F32)
    gc = jnp.dot(xn, w_ref[:, d:2 * d], preferred_element_type=F32)
    xv = jnp.dot(xn, w_ref[:, 2 * d:3 * d], preferred_element_type=F32)
    gb_ref[...] = gb.astype(BF16)
    u_ref[...] = (gc * xv).astype(BF16)


def _conv_in(x, g, w):
    n, d = x.shape
    tm = _tile(n, TOKEN_TILE)
    return pl.pallas_call(
        _conv_in_kernel,
        grid=(n // tm,),
        in_specs=[pl.BlockSpec((tm, d), lambda i: (i, 0)),
                  pl.BlockSpec((1, d), lambda i: (0, 0)),
                  pl.BlockSpec((d, 3 * d), lambda i: (0, 0))],
        out_specs=[pl.BlockSpec((tm, d), lambda i: (i, 0)),
                   pl.BlockSpec((tm, d), lambda i: (i, 0))],
        out_shape=[jax.ShapeDtypeStruct((n, d), BF16), jax.ShapeDtypeStruct((n, d), BF16)],
        compiler_params=_params("parallel"),
        name="conv_in",
    )(x, g, w)


def _conv_out_kernel(u_ref, up_ref, un_ref, gb_ref, h_ref, wdw_ref, w_ref, o_ref, *, tiles_per_seq):
    tm = u_ref.shape[0]
    pos = pl.program_id(0) % tiles_per_seq
    u = u_ref[...].astype(F32)
    halo = BF16_SUBLANES_V7X
    prev_row = jnp.where(pos == 0, 0.0, up_ref[halo - 1:halo, :].astype(F32))
    next_row = jnp.where(pos == tiles_per_seq - 1, 0.0, un_ref[0:1, :].astype(F32))
    row = lax.broadcasted_iota(jnp.int32, (tm, 1), 0)
    u_up = jnp.where(row == 0, prev_row, pltpu.roll(u, 1, axis=0))
    u_dn = jnp.where(row == tm - 1, next_row, pltpu.roll(u, tm - 1, axis=0))
    conv = u_up * wdw_ref[0:1, :] + u * wdw_ref[1:2, :] + u_dn * wdw_ref[2:3, :]
    g = (gb_ref[...].astype(F32) * conv).astype(BF16)
    o_ref[...] = h_ref[...] + jnp.dot(g, w_ref[...], preferred_element_type=F32)


def _conv_out(u, gb, h, w_dw, w_out, seq):
    n, d = u.shape
    tm = _tile(seq, TOKEN_TILE)
    halo = BF16_SUBLANES_V7X
    r = tm // halo
    nblk = n // halo
    return pl.pallas_call(
        functools.partial(_conv_out_kernel, tiles_per_seq=seq // tm),
        grid=(n // tm,),
        in_specs=[pl.BlockSpec((tm, d), lambda i: (i, 0)),
                  pl.BlockSpec((halo, d), lambda i: (jnp.maximum(i * r - 1, 0), 0)),
                  pl.BlockSpec((halo, d), lambda i: (jnp.minimum((i + 1) * r, nblk - 1), 0)),
                  pl.BlockSpec((tm, d), lambda i: (i, 0)),
                  pl.BlockSpec((tm, d), lambda i: (i, 0)),
                  pl.BlockSpec((CONV_WIDTH, d), lambda i: (0, 0)),
                  pl.BlockSpec((d, d), lambda i: (0, 0))],
        out_specs=pl.BlockSpec((tm, d), lambda i: (i, 0)),
        out_shape=jax.ShapeDtypeStruct((n, d), F32),
        compiler_params=_params("parallel"),
        name="conv_out",
    )(u, u, u, gb, h, w_dw, w_out)


def _rope_group(x, c, s):
    return x * c + pltpu.roll(x, LANES_V7X // 2, axis=1) * s


def _mla_in_kernel(x_ref, g_ref, win_ref, qn_ref, kvn_ref, wqb_ref, wk_ref, wvt_ref, cos_ref, sin_ref,
                   q_ref, k_ref, vt_ref):
    hp = MLA_HEAD_PAD
    tm = x_ref.shape[0]
    scale = (QK_NOPE + QK_ROPE) ** -0.5 * LOG2_E
    xn = _rms(x_ref[...], g_ref[...]).astype(BF16)
    lat = jnp.dot(xn, win_ref[...], preferred_element_type=F32)
    qn = _rms(lat[:, 0:Q_LORA], qn_ref[...]).astype(BF16)
    kvn = _rms(lat[:, Q_LORA:Q_LORA + KV_LORA], kvn_ref[...]).astype(BF16)
    c = cos_ref[...]
    s = sin_ref[...]
    k_pe = _rope_group(lat[:, Q_LORA + KV_LORA:], c, s).astype(BF16)
    extra = lax.broadcasted_iota(jnp.int32, (VT_ROWS - V_HEAD, tm), 0)
    ones_rows = jnp.where(extra == 0, 1.0, 0.0).astype(BF16)
    for h in range(MLA_HEADS):
        qh = jnp.dot(qn, wqb_ref[:, h * hp:(h + 1) * hp], preferred_element_type=F32)
        q_ref[:, h * hp:h * hp + QK_NOPE] = (qh[:, 0:QK_NOPE] * scale).astype(BF16)
        q_ref[:, h * hp + QK_NOPE:(h + 1) * hp] = (_rope_group(qh[:, QK_NOPE:], c, s) * scale).astype(BF16)
        kh = jnp.dot(kvn, wk_ref[:, h * QK_NOPE:(h + 1) * QK_NOPE], preferred_element_type=F32)
        k_ref[:, h * hp:h * hp + QK_NOPE] = kh.astype(BF16)
        k_ref[:, h * hp + QK_NOPE:(h + 1) * hp] = k_pe
        vt = lax.dot_general(wvt_ref[h * V_HEAD:(h + 1) * V_HEAD, :], kvn, (((1,), (1,)), ((), ())),
                             preferred_element_type=F32)
        vt_ref[0, h, 0:V_HEAD, :] = vt.astype(BF16)
        vt_ref[0, h, V_HEAD:VT_ROWS, :] = ones_rows


def _mla_in(x, g, w_in, q_norm, w_qb, kv_norm, w_k, w_vt, cos, sin, seq):
    n, d = x.shape
    tm = _tile(seq, min(MLA_TOKEN_TILE, max(seq // 8, LANES_V7X)))
    tps = seq // tm
    hq = MLA_HEADS * MLA_HEAD_PAD
    const = lambda i: (0, 0)
    return pl.pallas_call(
        _mla_in_kernel,
        grid=(n // tm,),
        in_specs=[pl.BlockSpec((tm, d), lambda i: (i, 0)),
                  pl.BlockSpec((1, d), const),
                  pl.BlockSpec(w_in.shape, const),
                  pl.BlockSpec((1, Q_LORA), const),
                  pl.BlockSpec((1, KV_LORA), const),
                  pl.BlockSpec(w_qb.shape, const),
                  pl.BlockSpec(w_k.shape, const),
                  pl.BlockSpec(w_vt.shape, const),
                  pl.BlockSpec((tm, LANES_V7X), lambda i: (i % tps, 0)),
                  pl.BlockSpec((tm, LANES_V7X), lambda i: (i % tps, 0))],
        out_specs=[pl.BlockSpec((tm, hq), lambda i: (i, 0)),
                   pl.BlockSpec((tm, hq), lambda i: (i, 0)),
                   pl.BlockSpec((1, MLA_HEADS, VT_ROWS, tm), lambda i: (i, 0, 0, 0))],
        out_shape=[jax.ShapeDtypeStruct((n, hq), BF16),
                   jax.ShapeDtypeStruct((n, hq), BF16),
                   jax.ShapeDtypeStruct((n // tm, MLA_HEADS, VT_ROWS, tm), BF16)],
        compiler_params=_params("parallel"),
        name="mla_in",
    )(x, g, w_in, q_norm, kv_norm, w_qb, w_k, w_vt, cos, sin)


def _attn_kernel(q_ref, k_ref, vt_ref, o_ref, s0, s1, p0, p1, acc_ref):
    tq = q_ref.shape[0]
    nk, _, _, tk = vt_ref.shape
    assert nk % 2 == 0
    q = q_ref[...]

    def scores(j, s_ref):
        start = pl.multiple_of(j * tk, tk)
        k = k_ref[pl.ds(start, tk), :]
        st = lax.dot_general(k, q, (((1,), (1,)), ((), ())), preferred_element_type=F32)
        s_ref[...] = st
        return jnp.max(st, axis=0, keepdims=True)

    def exps(s_ref, p_ref, m, tile_max):
        m_new = jnp.maximum(m, tile_max)
        p_ref[...] = jnp.exp2(s_ref[...] - m_new).astype(BF16)
        return m_new, jnp.exp2(m - m_new)

    def values(j, p_ref, alpha):
        acc_ref[...] = alpha * acc_ref[...] + jnp.dot(vt_ref[j, 0], p_ref[...], preferred_element_type=F32)

    acc_ref[...] = jnp.zeros_like(acc_ref)
    x0 = scores(0, s0)
    x1 = scores(1, s1)
    m, alpha = exps(s0, p0, jnp.full((1, tq), -jnp.inf, F32), x0)

    steps = nk - 2
    unroll = max(u for u in (2, 4, 6, 8, 10) if steps % u == 0) if steps else 2

    def group(i, carry):
        m, alpha, x1 = carry
        for t in range(0, unroll, 2):
            s = unroll * i + t + 1
            x0 = scores(s + 1, s0)
            values(s - 1, p0, alpha)
            m, alpha = exps(s1, p1, m, x1)
            x1 = scores(s + 2, s1)
            values(s, p1, alpha)
            m, alpha = exps(s0, p0, m, x0)
        return m, alpha, x1

    m, alpha, x1 = lax.fori_loop(0, steps // unroll, group, (m, alpha, x1))
    values(nk - 2, p0, alpha)
    m, alpha = exps(s1, p1, m, x1)
    values(nk - 1, p1, alpha)
    acc = acc_ref[...]
    out_t = acc[0:V_HEAD, :] / acc[V_HEAD:V_HEAD + 1, :]
    o_ref[...] = out_t.T.astype(BF16)


def _attention(q, k, vt, batch, seq):
    n = q.shape[0]
    tq = _tile(seq, ATTN_Q_TILE)
    tk = vt.shape[-1]
    nq = seq // tq
    nk = seq // tk
    hp = MLA_HEAD_PAD
    return pl.pallas_call(
        _attn_kernel,
        grid=(batch, MLA_HEADS, nq),
        in_specs=[pl.BlockSpec((tq, hp), lambda b, h, i: (b * nq + i, h)),
                  pl.BlockSpec((seq, hp), lambda b, h, i: (b, h)),
                  pl.BlockSpec((nk, 1, VT_ROWS, tk), lambda b, h, i: (b, h, 0, 0))],
        out_specs=pl.BlockSpec((tq, V_HEAD), lambda b, h, i: (b * nq + i, h)),
        out_shape=jax.ShapeDtypeStruct((n, MLA_HEADS * V_HEAD), BF16),
        scratch_shapes=[pltpu.VMEM((tk, tq), F32), pltpu.VMEM((tk, tq), F32),
                        pltpu.VMEM((tk, tq), BF16), pltpu.VMEM((tk, tq), BF16),
                        pltpu.VMEM((VT_ROWS, tq), F32)],
        compiler_params=_params("parallel", "parallel", "arbitrary"),
        name="mla_attention",
    )(q, k, vt)


def _proj_residual_kernel(a_ref, h_ref, w_ref, o_ref):
    o_ref[...] = h_ref[...] + jnp.dot(a_ref[...], w_ref[...], preferred_element_type=F32)


def _proj_residual(a, h, w):
    n, d = h.shape
    tm = _tile(n, TOKEN_TILE)
    return pl.pallas_call(
        _proj_residual_kernel,
        grid=(n // tm,),
        in_specs=[pl.BlockSpec((tm, a.shape[1]), lambda i: (i, 0)),
                  pl.BlockSpec((tm, d), lambda i: (i, 0)),
                  pl.BlockSpec(w.shape, lambda i: (0, 0))],
        out_specs=pl.BlockSpec((tm, d), lambda i: (i, 0)),
        out_shape=jax.ShapeDtypeStruct((n, d), F32),
        compiler_params=_params("parallel"),
        name="proj_residual",
    )(a, h, w)


def _ml_in_kernel(x_ref, g_ref, w_ref, b_ref, q_ref, k_ref, v_ref, o_ref, gt_ref):
    qw = ML_HEADS * ML_QK_PAD
    vw = ML_HEADS * ML_V
    xn = _rms(x_ref[...], g_ref[...]).astype(BF16)
    q = jnp.dot(xn, w_ref[:, 0:qw], preferred_element_type=F32)
    q_ref[...] = (q * (ML_QK ** -0.5)).astype(BF16)
    k_ref[...] = jnp.dot(xn, w_ref[:, qw:2 * qw], preferred_element_type=F32).astype(BF16)
    v_ref[...] = jnp.dot(xn, w_ref[:, 2 * qw:2 * qw + vw], preferred_element_type=F32).astype(BF16)
    o_ref[...] = jnp.dot(xn, w_ref[:, 2 * qw + vw:2 * qw + 2 * vw], preferred_element_type=F32).astype(BF16)
    gt_ref[...] = jnp.dot(xn, w_ref[:, 2 * qw + 2 * vw:], preferred_element_type=F32) + b_ref[...]


def _ml_in(x, g, w, b_gates):
    n, d = x.shape
    tm = _tile(n, TOKEN_TILE)
    qw = ML_HEADS * ML_QK_PAD
    vw = ML_HEADS * ML_V
    ng = 4 * ML_HEADS
    row = lambda i: (i, 0)
    const = lambda i: (0, 0)
    return pl.pallas_call(
        _ml_in_kernel,
        grid=(n // tm,),
        in_specs=[pl.BlockSpec((tm, d), row), pl.BlockSpec((1, d), const),
                  pl.BlockSpec(w.shape, const), pl.BlockSpec((1, ng), const)],
        out_specs=[pl.BlockSpec((tm, qw), row), pl.BlockSpec((tm, qw), row),
                   pl.BlockSpec((tm, vw), row), pl.BlockSpec((tm, vw), row),
                   pl.BlockSpec((tm, ng), row)],
        out_shape=[jax.ShapeDtypeStruct((n, qw), BF16), jax.ShapeDtypeStruct((n, qw), BF16),
                   jax.ShapeDtypeStruct((n, vw), BF16), jax.ShapeDtypeStruct((n, vw), BF16),
                   jax.ShapeDtypeStruct((n, ng), F32)],
        compiler_params=_params("parallel"),
        name="mlstm_in",
    )(x, g, w, b_gates)


def _split3(x):
    hi = x.astype(BF16)
    r1 = x - hi.astype(F32)
    mid = r1.astype(BF16)
    lo = (r1 - mid.astype(F32)).astype(BF16)
    return hi, mid, lo


def _mlstm_kernel(q_ref, k_ref, vt_ref, gc_ref, gr_ref, o_ref, c_scr, m_scr):
    lc = q_ref.shape[0]
    nh = ML_HEADS
    kp = ML_QK_PAD
    dv = ML_V
    fwd = pl.program_id(0) == 0

    @pl.when(pl.program_id(2) == 0)
    def _():
        c_scr[...] = jnp.zeros_like(c_scr)
        m_scr[...] = jnp.zeros_like(m_scr)

    gcol = gc_ref[0]
    grow = gr_ref[0]
    li_col = gcol[:, 0:nh]
    lf_col = jax.nn.log_sigmoid(gcol[:, nh:2 * nh])
    li_row = grow[0:nh, :]
    lf_row = jax.nn.log_sigmoid(grow[nh:2 * nh, :])

    r = lax.broadcasted_iota(jnp.int32, (lc, lc), 0)
    cidx = lax.broadcasted_iota(jnp.int32, (lc, lc), 1)
    sign = jnp.where(fwd, 1, -1)
    tri = jnp.where((r - cidx) * sign >= 0, 1.0, 0.0).astype(BF16)
    allowed_t = (cidx - r) * sign >= 0

    b_col = jnp.zeros((lc, nh), F32)
    for piece in _split3(lf_col):
        b_col = b_col + jnp.dot(tri, piece, preferred_element_type=F32)
    b_row = jnp.zeros((nh, lc), F32)
    for piece in _split3(lf_row):
        b_row = b_row + lax.dot_general(piece, tri, (((1,), (1,)), ((), ())), preferred_element_type=F32)
    a_all = jnp.sum(lf_row, axis=-1, keepdims=True)
    src_col = li_col - b_col

    sub = lax.broadcasted_iota(jnp.int32, (dv, lc), 0)
    ones_rows = jnp.where(sub == 0, 1.0, 0.0).astype(BF16)
    nt = (((1,), (1,)), ((), ()))

    for h in range(nh):
        q = q_ref[:, h * kp:(h + 1) * kp]
        k = k_ref[:, h * kp:(h + 1) * kp]
        vt_ext = jnp.concatenate([vt_ref[h * dv:(h + 1) * dv, :], ones_rows], axis=0)
        b_r = b_row[h:h + 1, :]
        li_r = li_row[h:h + 1, :]
        a = a_all[h:h + 1, :]
        m_in = m_scr[h][0:1, 0:1]
        ct_ext = c_scr[h]

        w_end = a - b_r + li_r
        g = jnp.max(w_end, axis=-1, keepdims=True)
        e_r = jnp.exp(w_end - g)

        dmat = jnp.where(allowed_t, b_r + src_col[:, h:h + 1], -jnp.inf)
        inter_log = b_r + m_in
        m_j = jnp.maximum(inter_log, jnp.max(dmat, axis=0, keepdims=True))
        inter = jnp.exp(inter_log - m_j)
        p = jnp.exp(dmat - m_j)
        s = lax.dot_general(k, q, nt, preferred_element_type=F32)
        qk = (s * p).astype(BF16)
        nd = (jnp.dot(vt_ext, qk, preferred_element_type=F32)
              + inter * lax.dot_general(ct_ext.astype(BF16), q, nt, preferred_element_type=F32))
        den = nd[dv:dv + 1, :]
        out_t = nd[0:dv, :] / jnp.maximum(jnp.abs(den), jnp.exp(-m_j))
        o_ref[0, :, h * dv:(h + 1) * dv] = out_t.T

        m_new = jnp.maximum(a + m_in, g)
        fdec = jnp.exp(a + m_in - m_new)
        iin = jnp.exp(g - m_new)
        vte = (vt_ext.astype(F32) * e_r).astype(BF16)
        c_scr[h] = fdec * ct_ext + iin * jnp.dot(vte, k, preferred_element_type=F32)
        m_scr[h] = jnp.broadcast_to(m_new, m_scr.shape[1:])


def _mlstm(q, k, vt, gcol, grow, batch, seq):
    n = q.shape[0]
    lc = _tile(seq, ML_CHUNK_TILE)
    nc = seq // lc
    qw = ML_HEADS * ML_QK_PAD
    vw = ML_HEADS * ML_V

    def chunk(d, b, c):
        return b * nc + c + d * (nc - 1 - 2 * c)

    return pl.pallas_call(
        _mlstm_kernel,
        grid=(2, batch, nc),
        in_specs=[pl.BlockSpec((lc, qw), lambda d, b, c: (chunk(d, b, c), 0)),
                  pl.BlockSpec((lc, qw), lambda d, b, c: (chunk(d, b, c), 0)),
                  pl.BlockSpec((vw, lc), lambda d, b, c: (0, chunk(d, b, c))),
                  pl.BlockSpec((1, lc, 2 * ML_HEADS), lambda d, b, c: (d, chunk(d, b, c), 0)),
                  pl.BlockSpec((1, 2 * ML_HEADS, lc), lambda d, b, c: (d, 0, chunk(d, b, c)))],
        out_specs=pl.BlockSpec((1, lc, vw), lambda d, b, c: (d, chunk(d, b, c), 0)),
        out_shape=jax.ShapeDtypeStruct((2, n, vw), F32),
        scratch_shapes=[pltpu.VMEM((ML_HEADS, 2 * ML_V, ML_QK_PAD), F32),
                        pltpu.VMEM((ML_HEADS, 8, LANES_V7X), F32)],
        compiler_params=_params("parallel", "parallel", "arbitrary"),
        name="mlstm_scan",
    )(q, k, vt, gcol, grow)


def _ml_out_kernel(hs_ref, og_ref, hn_ref, h_ref, w_ref, o_ref):
    dv = ML_V
    hs = hs_ref[0] + hs_ref[1]
    parts = []
    for hd in range(ML_HEADS):
        x = hs[:, hd * dv:(hd + 1) * dv]
        parts.append(x * lax.rsqrt(jnp.mean(x * x, axis=-1, keepdims=True) + EPS))
    y = jnp.concatenate(parts, axis=1) * hn_ref[...] * jax.nn.sigmoid(og_ref[...].astype(F32))
    o_ref[...] = h_ref[...] + jnp.dot(y.astype(BF16), w_ref[...], preferred_element_type=F32)


def _ml_out(hs, og, head_norm, h, w):
    n, d = h.shape
    vw = ML_HEADS * ML_V
    tm = _tile(n, TOKEN_TILE)
    row = lambda i: (i, 0)
    const = lambda i: (0, 0)
    return pl.pallas_call(
        _ml_out_kernel,
        grid=(n // tm,),
        in_specs=[pl.BlockSpec((2, tm, vw), lambda i: (0, i, 0)),
                  pl.BlockSpec((tm, vw), row), pl.BlockSpec((1, vw), const),
                  pl.BlockSpec((tm, d), row), pl.BlockSpec(w.shape, const)],
        out_specs=pl.BlockSpec((tm, d), row),
        out_shape=jax.ShapeDtypeStruct((n, d), F32),
        compiler_params=_params("parallel"),
        name="mlstm_out",
    )(hs, og, head_norm, h, w)


def _router_kernel(x_ref, g_ref, w_ref, wlo_ref, xn_ref, aff_ref, *, nreal):
    @pl.when(pl.program_id(0) < nreal)
    def _():
        xf = _rms(x_ref[...], g_ref[...])
        xn = xf.astype(BF16)
        xn_ref[...] = xn
        xlo = (xf - xn.astype(F32)).astype(BF16)
        logits = (jnp.dot(xn, w_ref[...], preferred_element_type=F32)
                  + jnp.dot(xlo, w_ref[...], preferred_element_type=F32)
                  + jnp.dot(xn, wlo_ref[...], preferred_element_type=F32))
        z = jnp.exp(logits - jnp.max(logits, axis=-1, keepdims=True))
        aff_ref[...] = z / jnp.sum(z, axis=-1, keepdims=True)

    @pl.when(pl.program_id(0) >= nreal)
    def _():
        xn_ref[...] = jnp.zeros_like(xn_ref)


def _router(x, g, w_f32):
    n, d = x.shape
    tm = _tile(n, TOKEN_TILE)
    nreal = n // tm
    npad = max(n, GATHER_OPERAND_ROWS)
    row = lambda i: (jnp.minimum(i, nreal - 1), 0)
    const = lambda i: (0, 0)
    w = w_f32.astype(BF16)
    w_lo = (w_f32 - w.astype(F32)).astype(BF16)
    return pl.pallas_call(
        functools.partial(_router_kernel, nreal=nreal),
        grid=(npad // tm,),
        in_specs=[pl.BlockSpec((tm, d), row), pl.BlockSpec((1, d), const),
                  pl.BlockSpec(w.shape, const), pl.BlockSpec(w.shape, const)],
        out_specs=[pl.BlockSpec((tm, d), lambda i: (i, 0)), pl.BlockSpec((tm, N_EXPERTS), row)],
        out_shape=[jax.ShapeDtypeStruct((npad, d), BF16), jax.ShapeDtypeStruct((n, N_EXPERTS), F32)],
        compiler_params=_params("arbitrary"),
        name="moe_router",
    )(x, g, w, w_lo)


def _ffn_kernel(*refs, nchunk, epc):
    x_refs = refs[:nchunk]
    gate_ref, wg_ref, wu_ref, wd_ref, o_ref = refs[nchunk:]

    def compute(x):
        g = jnp.dot(x, wg_ref[0, 0], preferred_element_type=F32)
        u = jnp.dot(x, wu_ref[0, 0], preferred_element_type=F32)
        hid = (g * jax.nn.sigmoid(g) * u).astype(BF16)
        y = jnp.dot(hid, wd_ref[0, 0], preferred_element_type=F32)
        o_ref[0] = (y * gate_ref[0]).astype(BF16)

    if nchunk == 1:
        compute(x_refs[0][0])
    else:
        chunk = pl.program_id(0) // epc
        for c in range(nchunk):
            @pl.when(chunk == c)
            def _(c=c):
                compute(x_refs[c][0])


def _expert_ffn(xes, gate, wg, wu, wd, layer):
    nchunk = len(xes)
    epc, cap, d = xes[0].shape
    e = nchunk * epc
    f = wg.shape[-1]
    tm = _tile(cap, TOKEN_TILE)
    nj = cap // tm
    tok = lambda i, j: (i, j, 0)
    wmap = lambda i, j: (layer, i, 0, 0)

    def chunk_map(c):
        def index(i, j):
            jj = jnp.where(i < c * epc, 0, jnp.where(i >= (c + 1) * epc, nj - 1, j))
            return (jnp.clip(i - c * epc, 0, epc - 1), jj, 0)
        return index

    return pl.pallas_call(
        functools.partial(_ffn_kernel, nchunk=nchunk, epc=epc),
        grid=(e, nj),
        in_specs=[pl.BlockSpec((1, tm, d), chunk_map(c)) for c in range(nchunk)]
        + [pl.BlockSpec((1, tm, 1), tok),
           pl.BlockSpec((1, 1, d, f), wmap), pl.BlockSpec((1, 1, d, f), wmap),
           pl.BlockSpec((1, 1, f, d), wmap)],
        out_specs=pl.BlockSpec((1, tm, d), tok),
        out_shape=jax.ShapeDtypeStruct((e, cap, d), BF16),
        compiler_params=_params("parallel", "arbitrary"),
        name="moe_ffn",
    )(*xes, gate, wg, wu, wd)


def _combine_kernel(lo_ref, h_ref, idx_ref, ye_hbm, o_ref, buf, sem):
    tb, d = h_ref.shape
    ne, nrow, lanes = idx_ref.shape
    cap = nrow * lanes
    win = COMBINE_WINDOW
    align = BF16_SUBLANES_V7X
    b = pl.program_id(0)
    nb = pl.num_programs(0)
    tok = b * tb + lax.broadcasted_iota(jnp.int32, (tb, 1), 0)
    lane = lax.broadcasted_iota(jnp.int32, (1, win), 1)

    def clamp(start):
        return jnp.minimum(start, cap - win)

    def first_start(blk, e):
        return clamp(lo_ref[e, blk] // align * align)

    def fetch(e, start):
        return pltpu.make_async_copy(ye_hbm.at[e, pl.ds(pl.multiple_of(start, align), win), :],
                                     buf.at[pl.ds(e * win, win)], sem.at[e])

    def select(e, start, row_lo, row_hi):
        r0 = start // lanes
        two = jnp.concatenate([idx_ref[e, pl.ds(r0, 1), :],
                               idx_ref[e, pl.ds(jnp.minimum(r0 + 1, nrow - 1), 1), :]], axis=1)
        ids = pltpu.roll(two, (2 * lanes - start % lanes) % (2 * lanes), axis=1)[:, 0:win]
        rid = start + lane
        ids = jnp.where((rid >= row_lo) & (rid < row_hi), ids, -1)
        return jnp.where(tok == ids, 1.0, 0.0).astype(BF16)

    @pl.when(b == 0)
    def _():
        for e in range(ne):
            fetch(e, first_start(0, e)).start()

    o_ref[...] = h_ref[...]
    group = 4
    total = None
    for e0 in range(0, ne, group):
        first = []
        for e in range(e0, e0 + group):
            lo = lo_ref[e, b]
            hi = lo_ref[e, b + 1]
            start0 = lo // align * align
            first.append((e, lo, hi, start0, clamp(start0)))
        for e, lo, hi, start0, start in first:
            fetch(e, start).wait()
        sel = jnp.concatenate([select(e, start, lo, hi) for e, lo, hi, start0, start in first], axis=1)
        part = jnp.dot(sel, buf[e0 * win:(e0 + group) * win, :], preferred_element_type=F32)
        total = part if total is None else total + part

        for e, lo, hi, start0, start in first:
            def more(w, carry, e=e, lo=lo, hi=hi, start0=start0):
                nominal = start0 + w * win
                st = clamp(nominal)
                cp = fetch(e, st)
                cp.start()
                cp.wait()
                o_ref[...] += jnp.dot(select(e, st, jnp.maximum(lo, nominal), hi), buf[e * win:(e + 1) * win, :],
                                      preferred_element_type=F32)
                return carry

            nwin = (hi - start0 + win - 1) // win
            lax.fori_loop(1, nwin, more, 0)

            @pl.when(b + 1 < nb)
            def _(e=e):
                fetch(e, first_start(b + 1, e)).start()

    o_ref[...] += total


def _combine(h, idx, ye):
    n, d = h.shape
    e, cap = idx.shape
    assert cap >= COMBINE_WINDOW and cap % LANES_V7X == 0 and e % 4 == 0 and COMBINE_WINDOW <= LANES_V7X
    tb = _tile(n, TOKEN_TILE)
    nb = n // tb
    bounds = jnp.arange(nb + 1, dtype=jnp.int32) * tb
    lo = jnp.sum(idx[:, :, None] < bounds[None, None, :], axis=1, dtype=jnp.int32)
    idx3 = idx.reshape(e, cap // LANES_V7X, LANES_V7X)
    grid_spec = pltpu.PrefetchScalarGridSpec(
        num_scalar_prefetch=1,
        grid=(nb,),
        in_specs=[pl.BlockSpec((tb, d), lambda i, lo_ref: (i, 0)),
                  pl.BlockSpec(idx3.shape, lambda i, lo_ref: (0, 0, 0)),
                  pl.BlockSpec(memory_space=pl.ANY)],
        out_specs=pl.BlockSpec((tb, d), lambda i, lo_ref: (i, 0)),
        scratch_shapes=[pltpu.VMEM((e * COMBINE_WINDOW, d), BF16), pltpu.SemaphoreType.DMA((e,))])
    return pl.pallas_call(
        _combine_kernel,
        grid_spec=grid_spec,
        out_shape=jax.ShapeDtypeStruct((n, d), F32),
        compiler_params=_params("arbitrary"),
        name="moe_combine",
    )(lo, h, idx3, ye)


def _moe(h, g, w_router, wg, wu, wd, layer):
    n, d = h.shape
    cap = EC_CAPACITY * n // N_EXPERTS
    xn, aff = _router(h, g, w_router)
    gate, idx = lax.top_k(aff.T, cap)
    idx, gate = lax.sort((idx, gate), dimension=1, num_keys=1)
    nchunk = max(1, idx.size // GATHER_ROWS)
    epc = N_EXPERTS // nchunk
    xes = [xn[idx[c * epc:(c + 1) * epc]] for c in range(nchunk)]
    ye = _expert_ffn(xes, gate[..., None], wg, wu, wd, layer)
    return _combine(h, idx, ye)


def _final_norm_kernel(x_ref, g_ref, o_ref):
    o_ref[...] = _rms(x_ref[...], g_ref[...])


def _final_norm(x, g):
    n, d = x.shape
    tm = _tile(n, TOKEN_TILE)
    return pl.pallas_call(
        _final_norm_kernel,
        grid=(n // tm,),
        in_specs=[pl.BlockSpec((tm, d), lambda i: (i, 0)), pl.BlockSpec((1, d), lambda i: (0, 0))],
        out_specs=pl.BlockSpec((tm, d), lambda i: (i, 0)),
        out_shape=jax.ShapeDtypeStruct((n, d), F32),
        compiler_params=_params("parallel"),
        name="final_norm",
    )(x, g)


def _rope_group_cols(w_pe):
    half = QK_ROPE // 2
    z = jnp.zeros((w_pe.shape[0], LANES_V7X // 2 - half), w_pe.dtype)
    return jnp.concatenate([w_pe[:, :half], z, w_pe[:, half:], z], axis=1)


def _rope_tables(seq):
    half = QK_ROPE // 2
    pos = jnp.arange(seq, dtype=F32)
    inv = ROPE_THETA ** (-jnp.arange(0, QK_ROPE, 2, dtype=F32) / QK_ROPE)
    ang = pos[:, None] * inv[None, :]
    c, s = jnp.cos(ang), jnp.sin(ang)
    z = jnp.zeros((seq, LANES_V7X // 2 - half), F32)
    return jnp.concatenate([c, z, c, z], axis=1), jnp.concatenate([-s, z, s, z], axis=1)


def _prep_mla(w_in, w_qb, w_kvb):
    lat = Q_LORA + KV_LORA
    w_in_p = jnp.concatenate([w_in[:, :lat], _rope_group_cols(w_in[:, lat:])], axis=1).astype(BF16)
    hd = QK_NOPE + QK_ROPE
    cols = []
    for h in range(MLA_HEADS):
        cols.append(w_qb[:, h * hd:h * hd + QK_NOPE])
        cols.append(_rope_group_cols(w_qb[:, h * hd + QK_NOPE:(h + 1) * hd]))
    w_kv = w_kvb.reshape(KV_LORA, MLA_HEADS, QK_NOPE + V_HEAD)
    w_k = w_kv[:, :, :QK_NOPE].reshape(KV_LORA, MLA_HEADS * QK_NOPE).astype(BF16)
    w_vt = w_kv[:, :, QK_NOPE:].reshape(KV_LORA, MLA_HEADS * V_HEAD).T.astype(BF16)
    return w_in_p, jnp.concatenate(cols, axis=1).astype(BF16), w_k, w_vt


def _prep_ml_in(w):
    qd = ML_HEADS * ML_QK
    cols = []
    for base in (0, qd):
        for h in range(ML_HEADS):
            cols.append(w[:, base + h * ML_QK:base + (h + 1) * ML_QK])
            cols.append(jnp.zeros((w.shape[0], ML_QK_PAD - ML_QK), w.dtype))
    cols.append(w[:, 2 * qd:])
    return jnp.concatenate(cols, axis=1).astype(BF16)


def _trunk(x, p):
    batch, seq, d = x.shape
    n = batch * seq
    h = x.reshape(n, d)
    for i in range(DEPTH):
        j = i // N_MIXERS
        kind = i % N_MIXERS
        g_mix = p["norm_mix"][i][None, :]
        if kind == 0:
            u, gb = _conv_in(h, g_mix, p["conv_w_in"][j])
            h = _conv_out(u, gb, h, p["conv_w_dw"][j], p["conv_w_out"][j], seq)
        elif kind == 1:
            cos, sin = _rope_tables(seq)
            q, k, vt = _mla_in(h, g_mix, p["mla_w_in"][j], p["mla_q_norm"][j][None, :], p["mla_w_qb"][j],
                               p["mla_kv_norm"][j][None, :], p["mla_w_k"][j], p["mla_w_vt"][j], cos, sin, seq)
            o = _attention(q, k, vt, batch, seq)
            h = _proj_residual(o, h, p["mla_w_out"][j])
        else:
            q, k, v, og, gates = _ml_in(h, g_mix, p["ml_w_in"][j], p["ml_b_gates"][j][None, :])
            gcol = gates.reshape(n, 2, 2 * ML_HEADS).transpose(1, 0, 2)
            grow = gcol.transpose(0, 2, 1)
            hs = _mlstm(q, k, v.T, gcol, grow, batch, seq)
            h = _ml_out(hs, og, p["ml_head_norm"][j][None, :], h, p["ml_w_out"][j])
        h = _moe(h, p["norm_ffn"][i][None, :], p["router_w"][i],
                 p["exp_w_gate"], p["exp_w_up"], p["exp_w_down"], i)
    return _final_norm(h, p["norm_final"][None, :]).reshape(batch, seq, d)


def kernel(x_prompt, x_sample, conv_w_in, conv_w_dw, conv_w_out, mla_w_in, mla_q_norm, mla_w_qb, mla_kv_norm, mla_w_kvb, mla_w_out, ml_w_in, ml_b_gates, ml_head_norm, ml_w_out, norm_mix, norm_ffn, router_w, exp_w_gate, exp_w_up, exp_w_down, norm_final):
    mla = [_prep_mla(mla_w_in[j], mla_w_qb[j], mla_w_kvb[j]) for j in range(mla_w_in.shape[0])]
    p = dict(
        conv_w_in=conv_w_in.astype(BF16), conv_w_dw=conv_w_dw, conv_w_out=conv_w_out.astype(BF16),
        mla_w_in=[m[0] for m in mla], mla_q_norm=mla_q_norm, mla_w_qb=[m[1] for m in mla],
        mla_kv_norm=mla_kv_norm, mla_w_k=[m[2] for m in mla], mla_w_vt=[m[3] for m in mla],
        mla_w_out=mla_w_out.astype(BF16),
        ml_w_in=[_prep_ml_in(ml_w_in[j]) for j in range(ml_w_in.shape[0])], ml_b_gates=ml_b_gates,
        ml_head_norm=ml_head_norm, ml_w_out=ml_w_out.astype(BF16),
        norm_mix=norm_mix, norm_ffn=norm_ffn, router_w=router_w,
        exp_w_gate=exp_w_gate.astype(BF16), exp_w_up=exp_w_up.astype(BF16),
        exp_w_down=exp_w_down.astype(BF16), norm_final=norm_final)
    return (_trunk(x_prompt, p), _trunk(x_sample, p))
```

```python
import functools

import jax
import jax.numpy as jnp
from jax import lax
from jax.experimental import pallas as pl
from jax.experimental.pallas import tpu as pltpu

F32 = jnp.float32
BF16 = jnp.bfloat16

D_MODEL = 1024
DEPTH = 4
N_MIXERS = 3
EPS = 1e-6
CONV_WIDTH = 3
MLA_HEADS = 8
Q_LORA = 384
KV_LORA = 256
QK_NOPE = 128
QK_ROPE = 64
V_HEAD = 128
ROPE_THETA = 10000.0
ML_HEADS = 8
ML_QK = 64
ML_V = 128
N_EXPERTS = 16
EC_CAPACITY = 2
D_EXPERT = 1024

LANES_V7X = 128
BF16_SUBLANES_V7X = 16
VMEM_BYTES_V7X = 64 * 1024 * 1024
VMEM_LIMIT_BYTES = VMEM_BYTES_V7X - 8 * 1024 * 1024

TOKEN_TILE = 512
MLA_TOKEN_TILE = 1024
ATTN_Q_TILE = 1024
ML_CHUNK_TILE = 256
COMBINE_WINDOW = 128
GATHER_ROWS = 32768
GATHER_OPERAND_ROWS = 32768

MLA_HEAD_PAD = 2 * LANES_V7X
VT_ROWS = V_HEAD + BF16_SUBLANES_V7X
LOG2_E = 1.4426950408889634
ML_QK_PAD = LANES_V7X


def _params(*sem):
    return pltpu.CompilerParams(dimension_semantics=sem, vmem_limit_bytes=VMEM_LIMIT_BYTES)


def _rms(x, g):
    ms = jnp.mean(x * x, axis=-1, keepdims=True)
    return x * lax.rsqrt(ms + EPS) * g


def _tile(n, pref):
    t = min(n, pref)
    assert n % t == 0, (n, t)
    return t


def _conv_in_kernel(x_ref, g_ref, w_ref, u_ref, gb_ref):
    d = D_MODEL
    xn = _rms(x_ref[...], g_ref[...]).astype(BF16)
    gb = jnp.dot(xn, w_ref[:, 0:d], preferred_element_type=F32)
    gc = jnp.dot(xn, w_ref[:, d:2 * d], preferred_element_type=F32)
    xv = jnp.dot(xn, w_ref[:, 2 * d:3 * d], preferred_element_type=F32)
    gb_ref[...] = gb.astype(BF16)
    u_ref[...] = (gc * xv).astype(BF16)


def _conv_in(x, g, w):
    n, d = x.shape
    tm = _tile(n, TOKEN_TILE)
    return pl.pallas_call(
        _conv_in_kernel,
        grid=(n // tm,),
        in_specs=[pl.BlockSpec((tm, d), lambda i: (i, 0)),
                  pl.BlockSpec((1, d), lambda i: (0, 0)),
                  pl.BlockSpec((d, 3 * d), lambda i: (0, 0))],
        out_specs=[pl.BlockSpec((tm, d), lambda i: (i, 0)),
                   pl.BlockSpec((tm, d), lambda i: (i, 0))],
        out_shape=[jax.ShapeDtypeStruct((n, d), BF16), jax.ShapeDtypeStruct((n, d), BF16)],
        compiler_params=_params("parallel"),
        name="conv_in",
    )(x, g, w)


def _conv_out_kernel(u_ref, up_ref, un_ref, gb_ref, h_ref, wdw_ref, w_ref, o_ref, *, tiles_per_seq):
    tm = u_ref.shape[0]
    pos = pl.program_id(0) % tiles_per_seq
    u = u_ref[...].astype(F32)
    halo = BF16_SUBLANES_V7X
    prev_row = jnp.where(pos == 0, 0.0, up_ref[halo - 1:halo, :].astype(F32))
    next_row = jnp.where(pos == tiles_per_seq - 1, 0.0, un_ref[0:1, :].astype(F32))
    row = lax.broadcasted_iota(jnp.int32, (tm, 1), 0)
    u_up = jnp.where(row == 0, prev_row, pltpu.roll(u, 1, axis=0))
    u_dn = jnp.where(row == tm - 1, next_row, pltpu.roll(u, tm - 1, axis=0))
    conv = u_up * wdw_ref[0:1, :] + u * wdw_ref[1:2, :] + u_dn * wdw_ref[2:3, :]
    g = (gb_ref[...].astype(F32) * conv).astype(BF16)
    o_ref[...] = h_ref[...] + jnp.dot(g, w_ref[...], preferred_element_type=F32)


def _conv_out(u, gb, h, w_dw, w_out, seq):
    n, d = u.shape
    tm = _tile(seq, TOKEN_TILE)
    halo = BF16_SUBLANES_V7X
    r = tm // halo
    nblk = n // halo
    return pl.pallas_call(
        functools.partial(_conv_out_kernel, tiles_per_seq=seq // tm),
        grid=(n // tm,),
        in_specs=[pl.BlockSpec((tm, d), lambda i: (i, 0)),
                  pl.BlockSpec((halo, d), lambda i: (jnp.maximum(i * r - 1, 0), 0)),
                  pl.BlockSpec((halo, d), lambda i: (jnp.minimum((i + 1) * r, nblk - 1), 0)),
                  pl.BlockSpec((tm, d), lambda i: (i, 0)),
                  pl.BlockSpec((tm, d), lambda i: (i, 0)),
                  pl.BlockSpec((CONV_WIDTH, d), lambda i: (0, 0)),
                  pl.BlockSpec((d, d), lambda i: (0, 0))],
        out_specs=pl.BlockSpec((tm, d), lambda i: (i, 0)),
        out_shape=jax.ShapeDtypeStruct((n, d), F32),
        compiler_params=_params("parallel"),
        name="conv_out",
    )(u, u, u, gb, h, w_dw, w_out)


def _rope_group(x, c, s):
    return x * c + pltpu.roll(x, LANES_V7X // 2, axis=1) * s


def _mla_in_kernel(x_ref, g_ref, win_ref, qn_ref, kvn_ref, wqb_ref, wk_ref, wvt_ref, cos_ref, sin_ref,
                   q_ref, k_ref, vt_ref):
    hp = MLA_HEAD_PAD
    tm = x_ref.shape[0]
    scale = (QK_NOPE + QK_ROPE) ** -0.5 * LOG2_E
    xn = _rms(x_ref[...], g_ref[...]).astype(BF16)
    lat = jnp.dot(xn, win_ref[...], preferred_element_type=F32)
    qn = _rms(lat[:, 0:Q_LORA], qn_ref[...]).astype(BF16)
    kvn = _rms(lat[:, Q_LORA:Q_LORA + KV_LORA], kvn_ref[...]).astype(BF16)
    c = cos_ref[...]
    s = sin_ref[...]
    k_pe = _rope_group(lat[:, Q_LORA + KV_LORA:], c, s).astype(BF16)
    extra = lax.broadcasted_iota(jnp.int32, (VT_ROWS - V_HEAD, tm), 0)
    ones_rows = jnp.where(extra == 0, 1.0, 0.0).astype(BF16)
    for h in range(MLA_HEADS):
        qh = jnp.dot(qn, wqb_ref[:, h * hp:(h + 1) * hp], preferred_element_type=F32)
        q_ref[:, h * hp:h * hp + QK_NOPE] = (qh[:, 0:QK_NOPE] * scale).astype(BF16)
        q_ref[:, h * hp + QK_NOPE:(h + 1) * hp] = (_rope_group(qh[:, QK_NOPE:], c, s) * scale).astype(BF16)
        kh = jnp.dot(kvn, wk_ref[:, h * QK_NOPE:(h + 1) * QK_NOPE], preferred_element_type=F32)
        k_ref[:, h * hp:h * hp + QK_NOPE] = kh.astype(BF16)
        k_ref[:, h * hp + QK_NOPE:(h + 1) * hp] = k_pe
        vt = lax.dot_general(wvt_ref[h * V_HEAD:(h + 1) * V_HEAD, :], kvn, (((1,), (1,)), ((), ())),
                             preferred_element_type=F32)
        vt_ref[0, h, 0:V_HEAD, :] = vt.astype(BF16)
        vt_ref[0, h, V_HEAD:VT_ROWS, :] = ones_rows


def _mla_in(x, g, w_in, q_norm, w_qb, kv_norm, w_k, w_vt, cos, sin, seq):
    n, d = x.shape
    tm = _tile(seq, min(MLA_TOKEN_TILE, max(seq // 8, LANES_V7X)))
    tps = seq // tm
    hq = MLA_HEADS * MLA_HEAD_PAD
    const = lambda i: (0, 0)
    return pl.pallas_call(
        _mla_in_kernel,
        grid=(n // tm,),
        in_specs=[pl.BlockSpec((tm, d), lambda i: (i, 0)),
                  pl.BlockSpec((1, d), const),
                  pl.BlockSpec(w_in.shape, const),
                  pl.BlockSpec((1, Q_LORA), const),
                  pl.BlockSpec((1, KV_LORA), const),
                  pl.BlockSpec(w_qb.shape, const),
                  pl.BlockSpec(w_k.shape, const),
                  pl.BlockSpec(w_vt.shape, const),
                  pl.BlockSpec((tm, LANES_V7X), lambda i: (i % tps, 0)),
                  pl.BlockSpec((tm, LANES_V7X), lambda i: (i % tps, 0))],
        out_specs=[pl.BlockSpec((tm, hq), lambda i: (i, 0)),
                   pl.BlockSpec((tm, hq), lambda i: (i, 0)),
                   pl.BlockSpec((1, MLA_HEADS, VT_ROWS, tm), lambda i: (i, 0, 0, 0))],
        out_shape=[jax.ShapeDtypeStruct((n, hq), BF16),
                   jax.ShapeDtypeStruct((n, hq), BF16),
                   jax.ShapeDtypeStruct((n // tm, MLA_HEADS, VT_ROWS, tm), BF16)],
        compiler_params=_params("parallel"),
        name="mla_in",
    )(x, g, w_in, q_norm, kv_norm, w_qb, w_k, w_vt, cos, sin)


def _attn_kernel(q_ref, k_ref, vt_ref, o_ref, s0, s1, p0, p1, acc_ref):
    tq = q_ref.shape[0]
    nk, _, _, tk = vt_ref.shape
    assert nk % 2 == 0
    q = q_ref[...]

    def scores(j, s_ref):
        start = pl.multiple_of(j * tk, tk)
        k = k_ref[pl.ds(start, tk), :]
        st = lax.dot_general(k, q, (((1,), (1,)), ((), ())), preferred_element_type=F32)
        s_ref[...] = st
        return jnp.max(st, axis=0, keepdims=True)

    def exps(s_ref, p_ref, m, tile_max):
        m_new = jnp.maximum(m, tile_max)
        p_ref[...] = jnp.exp2(s_ref[...] - m_new).astype(BF16)
        return m_new, jnp.exp2(m - m_new)

    def values(j, p_ref, alpha):
        acc_ref[...] = alpha * acc_ref[...] + jnp.dot(vt_ref[j, 0], p_ref[...], preferred_element_type=F32)

    acc_ref[...] = jnp.zeros_like(acc_ref)
    x0 = scores(0, s0)
    x1 = scores(1, s1)
    m, alpha = exps(s0, p0, jnp.full((1, tq), -jnp.inf, F32), x0)

    steps = nk - 2
    unroll = max(u for u in (2, 4, 6, 8, 10) if steps % u == 0) if steps else 2

    def group(i, carry):
        m, alpha, x1 = carry
        for t in range(0, unroll, 2):
            s = unroll * i + t + 1
            x0 = scores(s + 1, s0)
            values(s - 1, p0, alpha)
            m, alpha = exps(s1, p1, m, x1)
            x1 = scores(s + 2, s1)
            values(s, p1, alpha)
            m, alpha = exps(s0, p0, m, x0)
        return m, alpha, x1

    m, alpha, x1 = lax.fori_loop(0, steps // unroll, group, (m, alpha, x1))
    values(nk - 2, p0, alpha)
    m, alpha = exps(s1, p1, m, x1)
    values(nk - 1, p1, alpha)
    acc = acc_ref[...]
    out_t = acc[0:V_HEAD, :] / acc[V_HEAD:V_HEAD + 1, :]
    o_ref[...] = out_t.T.astype(BF16)


def _attention(q, k, vt, batch, seq):
    n = q.shape[0]
    tq = _tile(seq, ATTN_Q_TILE)
    tk = vt.shape[-1]
    nq = seq // tq
    nk = seq // tk
    hp = MLA_HEAD_PAD
    return pl.pallas_call(
        _attn_kernel,
        grid=(batch, MLA_HEADS, nq),
        in_specs=[pl.BlockSpec((tq, hp), lambda b, h, i: (b * nq + i, h)),
                  pl.BlockSpec((seq, hp), lambda b, h, i: (b, h)),
                  pl.BlockSpec((nk, 1, VT_ROWS, tk), lambda b, h, i: (b, h, 0, 0))],
        out_specs=pl.BlockSpec((tq, V_HEAD), lambda b, h, i: (b * nq + i, h)),
        out_shape=jax.ShapeDtypeStruct((n, MLA_HEADS * V_HEAD), BF16),
        scratch_shapes=[pltpu.VMEM((tk, tq), F32), pltpu.VMEM((tk, tq), F32),
                        pltpu.VMEM((tk, tq), BF16), pltpu.VMEM((tk, tq), BF16),
                        pltpu.VMEM((VT_ROWS, tq), F32)],
        compiler_params=_params("parallel", "parallel", "arbitrary"),
        name="mla_attention",
    )(q, k, vt)


def _proj_residual_kernel(a_ref, h_ref, w_ref, o_ref):
    o_ref[...] = h_ref[...] + jnp.dot(a_ref[...], w_ref[...], preferred_element_type=F32)


def _proj_residual(a, h, w):
    n, d = h.shape
    tm = _tile(n, TOKEN_TILE)
    return pl.pallas_call(
        _proj_residual_kernel,
        grid=(n // tm,),
        in_specs=[pl.BlockSpec((tm, a.shape[1]), lambda i: (i, 0)),
                  pl.BlockSpec((tm, d), lambda i: (i, 0)),
                  pl.BlockSpec(w.shape, lambda i: (0, 0))],
        out_specs=pl.BlockSpec((tm, d), lambda i: (i, 0)),
        out_shape=jax.ShapeDtypeStruct((n, d), F32),
        compiler_params=_params("parallel"),
        name="proj_residual",
    )(a, h, w)


def _ml_in_kernel(x_ref, g_ref, w_ref, b_ref, q_ref, k_ref, v_ref, o_ref, gt_ref):
    qw = ML_HEADS * ML_QK_PAD
    vw = ML_HEADS * ML_V
    xn = _rms(x_ref[...], g_ref[...]).astype(BF16)
    q = jnp.dot(xn, w_ref[:, 0:qw], preferred_element_type=F32)
    q_ref[...] = (q * (ML_QK ** -0.5)).astype(BF16)
    k_ref[...] = jnp.dot(xn, w_ref[:, qw:2 * qw], preferred_element_type=F32).astype(BF16)
    v_ref[...] = jnp.dot(xn, w_ref[:, 2 * qw:2 * qw + vw], preferred_element_type=F32).astype(BF16)
    o_ref[...] = jnp.dot(xn, w_ref[:, 2 * qw + vw:2 * qw + 2 * vw], preferred_element_type=F32).astype(BF16)
    gt_ref[...] = jnp.dot(xn, w_ref[:, 2 * qw + 2 * vw:], preferred_element_type=F32) + b_ref[...]


def _ml_in(x, g, w, b_gates):
    n, d = x.shape
    tm = _tile(n, TOKEN_TILE)
    qw = ML_HEADS * ML_QK_PAD
    vw = ML_HEADS * ML_V
    ng = 4 * ML_HEADS
    row = lambda i: (i, 0)
    const = lambda i: (0, 0)
    return pl.pallas_call(
        _ml_in_kernel,
        grid=(n // tm,),
        in_specs=[pl.BlockSpec((tm, d), row), pl.BlockSpec((1, d), const),
                  pl.BlockSpec(w.shape, const), pl.BlockSpec((1, ng), const)],
        out_specs=[pl.BlockSpec((tm, qw), row), pl.BlockSpec((tm, qw), row),
                   pl.BlockSpec((tm, vw), row), pl.BlockSpec((tm, vw), row),
                   pl.BlockSpec((tm, ng), row)],
        out_shape=[jax.ShapeDtypeStruct((n, qw), BF16), jax.ShapeDtypeStruct((n, qw), BF16),
                   jax.ShapeDtypeStruct((n, vw), BF16), jax.ShapeDtypeStruct((n, vw), BF16),
                   jax.ShapeDtypeStruct((n, ng), F32)],
        compiler_params=_params("parallel"),
        name="mlstm_in",
    )(x, g, w, b_gates)


def _split3(x):
    hi = x.astype(BF16)
    r1 = x - hi.astype(F32)
    mid = r1.astype(BF16)
    lo = (r1 - mid.astype(F32)).astype(BF16)
    return hi, mid, lo


def _mlstm_kernel(q_ref, k_ref, vt_ref, gc_ref, gr_ref, o_ref, c_scr, m_scr):
    lc = q_ref.shape[0]
    nh = ML_HEADS
    kp = ML_QK_PAD
    dv = ML_V
    fwd = pl.program_id(0) == 0

    @pl.when(pl.program_id(2) == 0)
    def _():
        c_scr[...] = jnp.zeros_like(c_scr)
        m_scr[...] = jnp.zeros_like(m_scr)

    gcol = gc_ref[0]
    grow = gr_ref[0]
    li_col = gcol[:, 0:nh]
    lf_col = jax.nn.log_sigmoid(gcol[:, nh:2 * nh])
    li_row = grow[0:nh, :]
    lf_row = jax.nn.log_sigmoid(grow[nh:2 * nh, :])

    r = lax.broadcasted_iota(jnp.int32, (lc, lc), 0)
    cidx = lax.broadcasted_iota(jnp.int32, (lc, lc), 1)
    sign = jnp.where(fwd, 1, -1)
    tri = jnp.where((r - cidx) * sign >= 0, 1.0, 0.0).astype(BF16)
    allowed_t = (cidx - r) * sign >= 0

    b_col = jnp.zeros((lc, nh), F32)
    for piece in _split3(lf_col):
        b_col = b_col + jnp.dot(tri, piece, preferred_element_type=F32)
    b_row = jnp.zeros((nh, lc), F32)
    for piece in _split3(lf_row):
        b_row = b_row + lax.dot_general(piece, tri, (((1,), (1,)), ((), ())), preferred_element_type=F32)
    a_all = jnp.sum(lf_row, axis=-1, keepdims=True)
    src_col = li_col - b_col

    sub = lax.broadcasted_iota(jnp.int32, (dv, lc), 0)
    ones_rows = jnp.where(sub == 0, 1.0, 0.0).astype(BF16)
    nt = (((1,), (1,)), ((), ()))

    for h in range(nh):
        q = q_ref[:, h * kp:(h + 1) * kp]
        k = k_ref[:, h * kp:(h + 1) * kp]
        vt_ext = jnp.concatenate([vt_ref[h * dv:(h + 1) * dv, :], ones_rows], axis=0)
        b_r = b_row[h:h + 1, :]
        li_r = li_row[h:h + 1, :]
        a = a_all[h:h + 1, :]
        m_in = m_scr[h][0:1, 0:1]
        ct_ext = c_scr[h]

        w_end = a - b_r + li_r
        g = jnp.max(w_end, axis=-1, keepdims=True)
        e_r = jnp.exp(w_end - g)

        dmat = jnp.where(allowed_t, b_r + src_col[:, h:h + 1], -jnp.inf)
        inter_log = b_r + m_in
        m_j = jnp.maximum(inter_log, jnp.max(dmat, axis=0, keepdims=True))
        inter = jnp.exp(inter_log - m_j)
        p = jnp.exp(dmat - m_j)
        s = lax.dot_general(k, q, nt, preferred_element_type=F32)
        qk = (s * p).astype(BF16)
        nd = (jnp.dot(vt_ext, qk, preferred_element_type=F32)
              + inter * lax.dot_general(ct_ext.astype(BF16), q, nt, preferred_element_type=F32))
        den = nd[dv:dv + 1, :]
        out_t = nd[0:dv, :] / jnp.maximum(jnp.abs(den), jnp.exp(-m_j))
        o_ref[0, :, h * dv:(h + 1) * dv] = out_t.T

        m_new = jnp.maximum(a + m_in, g)
        fdec = jnp.exp(a + m_in - m_new)
        iin = jnp.exp(g - m_new)
        vte = (vt_ext.astype(F32) * e_r).astype(BF16)
        c_scr[h] = fdec * ct_ext + iin * jnp.dot(vte, k, preferred_element_type=F32)
        m_scr[h] = jnp.broadcast_to(m_new, m_scr.shape[1:])


def _mlstm(q, k, vt, gcol, grow, batch, seq):
    n = q.shape[0]
    lc = _tile(seq, ML_CHUNK_TILE)
    nc = seq // lc
    qw = ML_HEADS * ML_QK_PAD
    vw = ML_HEADS * ML_V

    def chunk(d, b, c):
        return b * nc + c + d * (nc - 1 - 2 * c)

    return pl.pallas_call(
        _mlstm_kernel,
        grid=(2, batch, nc),
        in_specs=[pl.BlockSpec((lc, qw), lambda d, b, c: (chunk(d, b, c), 0)),
                  pl.BlockSpec((lc, qw), lambda d, b, c: (chunk(d, b, c), 0)),
                  pl.BlockSpec((vw, lc), lambda d, b, c: (0, chunk(d, b, c))),
                  pl.BlockSpec((1, lc, 2 * ML_HEADS), lambda d, b, c: (d, chunk(d, b, c), 0)),
                  pl.BlockSpec((1, 2 * ML_HEADS, lc), lambda d, b, c: (d, 0, chunk(d, b, c)))],
        out_specs=pl.BlockSpec((1, lc, vw), lambda d, b, c: (d, chunk(d, b, c), 0)),
        out_shape=jax.ShapeDtypeStruct((2, n, vw), F32),
        scratch_shapes=[pltpu.VMEM((ML_HEADS, 2 * ML_V, ML_QK_PAD), F32),
                        pltpu.VMEM((ML_HEADS, 8, LANES_V7X), F32)],
        compiler_params=_params("parallel", "parallel", "arbitrary"),
        name="mlstm_scan",
    )(q, k, vt, gcol, grow)


def _ml_out_kernel(hs_ref, og_ref, hn_ref, h_ref, w_ref, o_ref):
    dv = ML_V
    hs = hs_ref[0] + hs_ref[1]
    parts = []
    for hd in range(ML_HEADS):
        x = hs[:, hd * dv:(hd + 1) * dv]
        parts.append(x * lax.rsqrt(jnp.mean(x * x, axis=-1, keepdims=True) + EPS))
    y = jnp.concatenate(parts, axis=1) * hn_ref[...] * jax.nn.sigmoid(og_ref[...].astype(F32))
    o_ref[...] = h_ref[...] + jnp.dot(y.astype(BF16), w_ref[...], preferred_element_type=F32)


def _ml_out(hs, og, head_norm, h, w):
    n, d = h.shape
    vw = ML_HEADS * ML_V
    tm = _tile(n, TOKEN_TILE)
    row = lambda i: (i, 0)
    const = lambda i: (0, 0)
    return pl.pallas_call(
        _ml_out_kernel,
        grid=(n // tm,),
        in_specs=[pl.BlockSpec((2, tm, vw), lambda i: (0, i, 0)),
                  pl.BlockSpec((tm, vw), row), pl.BlockSpec((1, vw), const),
                  pl.BlockSpec((tm, d), row), pl.BlockSpec(w.shape, const)],
        out_specs=pl.BlockSpec((tm, d), row),
        out_shape=jax.ShapeDtypeStruct((n, d), F32),
        compiler_params=_params("parallel"),
        name="mlstm_out",
    )(hs, og, head_norm, h, w)


def _router_kernel(x_ref, g_ref, w_ref, wlo_ref, xn_ref, aff_ref, *, nreal):
    @pl.when(pl.program_id(0) < nreal)
    def _():
        xf = _rms(x_ref[...], g_ref[...])
        xn = xf.astype(BF16)
        xn_ref[...] = xn
        xlo = (xf - xn.astype(F32)).astype(BF16)
        logits = (jnp.dot(xn, w_ref[...], preferred_element_type=F32)
                  + jnp.dot(xlo, w_ref[...], preferred_element_type=F32)
                  + jnp.dot(xn, wlo_ref[...], preferred_element_type=F32))
        z = jnp.exp(logits - jnp.max(logits, axis=-1, keepdims=True))
        aff_ref[...] = z / jnp.sum(z, axis=-1, keepdims=True)

    @pl.when(pl.program_id(0) >= nreal)
    def _():
        xn_ref[...] = jnp.zeros_like(xn_ref)


def _router(x, g, w_f32):
    n, d = x.shape
    tm = _tile(n, TOKEN_TILE)
    nreal = n // tm
    npad = max(n, GATHER_OPERAND_ROWS)
    row = lambda i: (jnp.minimum(i, nreal - 1), 0)
    const = lambda i: (0, 0)
    w = w_f32.astype(BF16)
    w_lo = (w_f32 - w.astype(F32)).astype(BF16)
    return pl.pallas_call(
        functools.partial(_router_kernel, nreal=nreal),
        grid=(npad // tm,),
        in_specs=[pl.BlockSpec((tm, d), row), pl.BlockSpec((1, d), const),
                  pl.BlockSpec(w.shape, const), pl.BlockSpec(w.shape, const)],
        out_specs=[pl.BlockSpec((tm, d), lambda i: (i, 0)), pl.BlockSpec((tm, N_EXPERTS), row)],
        out_shape=[jax.ShapeDtypeStruct((npad, d), BF16), jax.ShapeDtypeStruct((n, N_EXPERTS), F32)],
        compiler_params=_params("arbitrary"),
        name="moe_router",
    )(x, g, w, w_lo)


def _ffn_kernel(*refs, nchunk, epc):
    x_refs = refs[:nchunk]
    gate_ref, wg_ref, wu_ref, wd_ref, o_ref = refs[nchunk:]

    def compute(x):
        g = jnp.dot(x, wg_ref[0, 0], preferred_element_type=F32)
        u = jnp.dot(x, wu_ref[0, 0], preferred_element_type=F32)
        hid = (g * jax.nn.sigmoid(g) * u).astype(BF16)
        y = jnp.dot(hid, wd_ref[0, 0], preferred_element_type=F32)
        o_ref[0] = (y * gate_ref[0]).astype(BF16)

    if nchunk == 1:
        compute(x_refs[0][0])
    else:
        chunk = pl.program_id(0) // epc
        for c in range(nchunk):
            @pl.when(chunk == c)
            def _(c=c):
                compute(x_refs[c][0])


def _expert_ffn(xes, gate, wg, wu, wd, layer):
    nchunk = len(xes)
    epc, cap, d = xes[0].shape
    e = nchunk * epc
    f = wg.shape[-1]
    tm = _tile(cap, TOKEN_TILE)
    nj = cap // tm
    tok = lambda i, j: (i, j, 0)
    wmap = lambda i, j: (layer, i, 0, 0)

    def chunk_map(c):
        def index(i, j):
            jj = jnp.where(i < c * epc, 0, jnp.where(i >= (c + 1) * epc, nj - 1, j))
            return (jnp.clip(i - c * epc, 0, epc - 1), jj, 0)
        return index

    return pl.pallas_call(
        functools.partial(_ffn_kernel, nchunk=nchunk, epc=epc),
        grid=(e, nj),
        in_specs=[pl.BlockSpec((1, tm, d), chunk_map(c)) for c in range(nchunk)]
        + [pl.BlockSpec((1, tm, 1), tok),
           pl.BlockSpec((1, 1, d, f), wmap), pl.BlockSpec((1, 1, d, f), wmap),
           pl.BlockSpec((1, 1, f, d), wmap)],
        out_specs=pl.BlockSpec((1, tm, d), tok),
        out_shape=jax.ShapeDtypeStruct((e, cap, d), BF16),
        compiler_params=_params("parallel", "arbitrary"),
        name="moe_ffn",
    )(*xes, gate, wg, wu, wd)


def _threshold_kernel(a_ref, t_ref, need_ref, *, cap):
    bits = pltpu.bitcast(a_ref[...], jnp.int32)
    ne = bits.shape[0]

    def count(mask):
        return jnp.sum(jnp.where(mask, 1.0, 0.0), axis=1, keepdims=True)

    t = jnp.zeros((ne, 1), jnp.int32)
    for k in range(30, -1, -1):
        cand = t | (1 << k)
        t = jnp.where(count(bits >= cand) >= cap, cand, t)
    need = cap - count(bits > t)
    t_ref[...] = jnp.broadcast_to(t, t_ref.shape)
    need_ref[...] = jnp.broadcast_to(need, need_ref.shape)


def _threshold(aff_t, cap):
    e, n = aff_t.shape
    return pl.pallas_call(
        functools.partial(_threshold_kernel, cap=cap),
        grid=(1,),
        in_specs=[pl.BlockSpec((e, n), lambda i: (0, 0))],
        out_specs=[pl.BlockSpec((e, LANES_V7X), lambda i: (0, 0)), pl.BlockSpec((e, LANES_V7X), lambda i: (0, 0))],
        out_shape=[jax.ShapeDtypeStruct((e, LANES_V7X), jnp.int32), jax.ShapeDtypeStruct((e, LANES_V7X), F32)],
        compiler_params=_params("arbitrary"),
        name="moe_threshold",
    )(aff_t)


def _positions_kernel(aff_ref, thr_ref, need_ref, pos_ref, gate_ref, cnt_ref, rc, tc):
    tb = aff_ref.shape[0]

    @pl.when(pl.program_id(0) == 0)
    def _():
        rc[...] = jnp.zeros_like(rc)
        tc[...] = jnp.zeros_like(tc)

    a = aff_ref[...]
    bits = pltpu.bitcast(a, jnp.int32)
    thr = thr_ref[...]
    r = lax.broadcasted_iota(jnp.int32, (tb, tb), 0)
    c = lax.broadcasted_iota(jnp.int32, (tb, tb), 1)
    tril = jnp.where(r >= c, 1.0, 0.0).astype(BF16)
    eq = jnp.where(bits == thr, 1.0, 0.0)
    eq_incl = jnp.dot(tril, eq.astype(BF16), preferred_element_type=F32)
    tie_rank = tc[0:1, :] + eq_incl - eq
    sel = jnp.where(bits > thr, 1.0, jnp.where(tie_rank < need_ref[...], eq, 0.0))
    sel_incl = jnp.dot(tril, sel.astype(BF16), preferred_element_type=F32)
    taken = sel > 0.5
    pos = (rc[0:1, :] + sel_incl - 1.0).astype(jnp.int32)
    pos_ref[...] = jnp.where(taken, pos, -1)
    gate_ref[...] = jnp.where(taken, a, 0.0)
    cnt_ref[0] = jnp.broadcast_to(rc[0:1, :], cnt_ref.shape[1:]).astype(jnp.int32)
    rc[...] = rc[...] + jnp.sum(sel, axis=0, keepdims=True)
    tc[...] = tc[...] + jnp.sum(eq, axis=0, keepdims=True)


def _positions(aff, thr_row, need_row):
    n, e = aff.shape
    tb = _tile(n, TOKEN_TILE)
    nb = n // tb
    row = lambda i: (i, 0)
    const = lambda i: (0, 0)
    return pl.pallas_call(
        _positions_kernel,
        grid=(nb,),
        in_specs=[pl.BlockSpec((tb, e), row), pl.BlockSpec((1, e), const), pl.BlockSpec((1, e), const)],
        out_specs=[pl.BlockSpec((tb, e), row), pl.BlockSpec((tb, e), row),
                   pl.BlockSpec((1, 8, e), lambda i: (i, 0, 0))],
        out_shape=[jax.ShapeDtypeStruct((n, e), jnp.int32), jax.ShapeDtypeStruct((n, e), F32),
                   jax.ShapeDtypeStruct((nb, 8, e), jnp.int32)],
        scratch_shapes=[pltpu.VMEM((8, e), F32), pltpu.VMEM((8, e), F32)],
        compiler_params=_params("arbitrary"),
        name="moe_positions",
    )(aff, thr_row, need_row)


def _compact_kernel(lo_ref, pt_ref, gt_ref, idx_ref, gate_ref):
    ne, tb = pt_ref.shape
    _, nrow, lanes = idx_ref.shape
    cap = nrow * lanes
    win = COMBINE_WINDOW
    align = BF16_SUBLANES_V7X
    b = pl.program_id(0)

    @pl.when(b == 0)
    def _():
        idx_ref[...] = jnp.zeros_like(idx_ref)
        gate_ref[...] = jnp.zeros_like(gate_ref)

    tok = b * tb + lax.broadcasted_iota(jnp.int32, (1, tb), 1)
    digits = [(tok >> 8).astype(F32).astype(BF16), (tok & 255).astype(F32).astype(BF16)]
    pad = jnp.zeros((3, tb), BF16)
    slot = lax.broadcasted_iota(jnp.int32, (win, tb), 0)
    zeros = jnp.zeros((1, lanes), F32)
    nt = (((1,), (1,)), ((), ()))

    def window(e, w, start0):
        nominal = start0 + w * win
        start = jnp.minimum(nominal, cap - win)
        rel = pt_ref[e:e + 1, :] - start
        rel = jnp.where(rel >= nominal - start, rel, -1)
        sel_t = jnp.where(slot == rel, 1.0, 0.0).astype(BF16)
        lhs = jnp.concatenate(digits + list(_split3(gt_ref[e:e + 1, :])) + [pad], axis=0)
        got = lax.dot_general(lhs, sel_t, nt, preferred_element_type=F32)
        ids = got[0:1, :] * 256.0 + got[1:2, :]
        gts = got[2:3, :] + got[3:4, :] + got[4:5, :]
        off = start % lanes
        r0 = start // lanes
        r1 = jnp.minimum(r0 + 1, nrow - 1)
        ids2 = pltpu.roll(jnp.concatenate([ids, zeros], axis=1), off, axis=1)
        gts2 = pltpu.roll(jnp.concatenate([gts, zeros], axis=1), off, axis=1)
        idx_ref[e, pl.ds(r0, 1), :] += ids2[:, 0:lanes].astype(jnp.int32)
        idx_ref[e, pl.ds(r1, 1), :] += ids2[:, lanes:].astype(jnp.int32)
        gate_ref[e, pl.ds(r0, 1), :] += gts2[:, 0:lanes]
        gate_ref[e, pl.ds(r1, 1), :] += gts2[:, lanes:]

    starts = [lo_ref[b, e] // align * align for e in range(ne)]
    for e in range(ne):
        window(e, 0, starts[e])
    for e in range(ne):
        nwin = (lo_ref[b + 1, e] - starts[e] + win - 1) // win
        lax.fori_loop(1, nwin, lambda w, c, e=e: (window(e, w, starts[e]), c)[1], 0)


def _compact(lo, pos_t, gate_t, cap):
    e, n = pos_t.shape
    assert COMBINE_WINDOW == LANES_V7X
    tb = _tile(n, TOKEN_TILE)
    shape3 = (e, cap // LANES_V7X, LANES_V7X)
    grid_spec = pltpu.PrefetchScalarGridSpec(
        num_scalar_prefetch=1,
        grid=(n // tb,),
        in_specs=[pl.BlockSpec((e, tb), lambda i, lo_ref: (0, i)),
                  pl.BlockSpec((e, tb), lambda i, lo_ref: (0, i))],
        out_specs=[pl.BlockSpec(shape3, lambda i, lo_ref: (0, 0, 0)),
                   pl.BlockSpec(shape3, lambda i, lo_ref: (0, 0, 0))])
    return pl.pallas_call(
        _compact_kernel,
        grid_spec=grid_spec,
        out_shape=[jax.ShapeDtypeStruct(shape3, jnp.int32), jax.ShapeDtypeStruct(shape3, F32)],
        compiler_params=_params("arbitrary"),
        name="moe_compact",
    )(lo, pos_t, gate_t)


def _combine_kernel(lo_ref, h_ref, idx_ref, ye_hbm, o_ref, buf, sem):
    tb, d = h_ref.shape
    ne, nrow, lanes = idx_ref.shape
    cap = nrow * lanes
    win = COMBINE_WINDOW
    align = BF16_SUBLANES_V7X
    b = pl.program_id(0)
    nb = pl.num_programs(0)
    tok = b * tb + lax.broadcasted_iota(jnp.int32, (tb, 1), 0)
    lane = lax.broadcasted_iota(jnp.int32, (1, win), 1)

    def clamp(start):
        return jnp.minimum(start, cap - win)

    def first_start(blk, e):
        return clamp(lo_ref[e, blk] // align * align)

    def fetch(e, start):
        return pltpu.make_async_copy(ye_hbm.at[e, pl.ds(pl.multiple_of(start, align), win), :],
                                     buf.at[pl.ds(e * win, win)], sem.at[e])

    def select(e, start, row_lo, row_hi):
        r0 = start // lanes
        two = jnp.concatenate([idx_ref[e, pl.ds(r0, 1), :],
                               idx_ref[e, pl.ds(jnp.minimum(r0 + 1, nrow - 1), 1), :]], axis=1)
        ids = pltpu.roll(two, (2 * lanes - start % lanes) % (2 * lanes), axis=1)[:, 0:win]
        rid = start + lane
        ids = jnp.where((rid >= row_lo) & (rid < row_hi), ids, -1)
        return jnp.where(tok == ids, 1.0, 0.0).astype(BF16)

    @pl.when(b == 0)
    def _():
        for e in range(ne):
            fetch(e, first_start(0, e)).start()

    o_ref[...] = h_ref[...]
    group = 4
    total = None
    for e0 in range(0, ne, group):
        first = []
        for e in range(e0, e0 + group):
            lo = lo_ref[e, b]
            hi = lo_ref[e, b + 1]
            start0 = lo // align * align
            first.append((e, lo, hi, start0, clamp(start0)))
        for e, lo, hi, start0, start in first:
            fetch(e, start).wait()
        sel = jnp.concatenate([select(e, start, lo, hi) for e, lo, hi, start0, start in first], axis=1)
        part = jnp.dot(sel, buf[e0 * win:(e0 + group) * win, :], preferred_element_type=F32)
        total = part if total is None else total + part

        for e, lo, hi, start0, start in first:
            def more(w, carry, e=e, lo=lo, hi=hi, start0=start0):
                nominal = start0 + w * win
                st = clamp(nominal)
                cp = fetch(e, st)
                cp.start()
                cp.wait()
                o_ref[...] += jnp.dot(select(e, st, jnp.maximum(lo, nominal), hi), buf[e * win:(e + 1) * win, :],
                                      preferred_element_type=F32)
                return carry

            nwin = (hi - start0 + win - 1) // win
            lax.fori_loop(1, nwin, more, 0)

            @pl.when(b + 1 < nb)
            def _(e=e):
                fetch(e, first_start(b + 1, e)).start()

    o_ref[...] += total


def _combine(h, lo, idx3, ye):
    n, d = h.shape
    e, cap, _ = ye.shape
    assert cap >= COMBINE_WINDOW and cap % LANES_V7X == 0 and e % 4 == 0 and COMBINE_WINDOW <= LANES_V7X
    tb = _tile(n, TOKEN_TILE)
    nb = n // tb
    grid_spec = pltpu.PrefetchScalarGridSpec(
        num_scalar_prefetch=1,
        grid=(nb,),
        in_specs=[pl.BlockSpec((tb, d), lambda i, lo_ref: (i, 0)),
                  pl.BlockSpec(idx3.shape, lambda i, lo_ref: (0, 0, 0)),
                  pl.BlockSpec(memory_space=pl.ANY)],
        out_specs=pl.BlockSpec((tb, d), lambda i, lo_ref: (i, 0)),
        scratch_shapes=[pltpu.VMEM((e * COMBINE_WINDOW, d), BF16), pltpu.SemaphoreType.DMA((e,))])
    return pl.pallas_call(
        _combine_kernel,
        grid_spec=grid_spec,
        out_shape=jax.ShapeDtypeStruct((n, d), F32),
        compiler_params=_params("arbitrary"),
        name="moe_combine",
    )(lo, h, idx3, ye)


def _moe(h, g, w_router, wg, wu, wd, layer):
    n, d = h.shape
    cap = EC_CAPACITY * n // N_EXPERTS
    xn, aff = _router(h, g, w_router)
    thr, need = _threshold(aff.T, cap)
    pos, gate, cnt = _positions(aff, thr[:, 0][None, :], need[:, 0][None, :])
    lo = jnp.concatenate([cnt[:, 0, :], jnp.full((1, N_EXPERTS), cap, jnp.int32)], axis=0)
    idx3, gate3 = _compact(lo, pos.T, gate.T, cap)
    idx = idx3.reshape(N_EXPERTS, cap)
    nchunk = max(1, idx.size // GATHER_ROWS)
    epc = N_EXPERTS // nchunk
    xes = [xn[idx[c * epc:(c + 1) * epc]] for c in range(nchunk)]
    ye = _expert_ffn(xes, gate3.reshape(N_EXPERTS, cap, 1), wg, wu, wd, layer)
    return _combine(h, lo.T, idx3, ye)


def _final_norm_kernel(x_ref, g_ref, o_ref):
    o_ref[...] = _rms(x_ref[...], g_ref[...])


def _final_norm(x, g):
    n, d = x.shape
    tm = _tile(n, TOKEN_TILE)
    return pl.pallas_call(
        _final_norm_kernel,
        grid=(n // tm,),
        in_specs=[pl.BlockSpec((tm, d), lambda i: (i, 0)), pl.BlockSpec((1, d), lambda i: (0, 0))],
        out_specs=pl.BlockSpec((tm, d), lambda i: (i, 0)),
        out_shape=jax.ShapeDtypeStruct((n, d), F32),
        compiler_params=_params("parallel"),
        name="final_norm",
    )(x, g)


def _rope_group_cols(w_pe):
    half = QK_ROPE // 2
    z = jnp.zeros((w_pe.shape[0], LANES_V7X // 2 - half), w_pe.dtype)
    return jnp.concatenate([w_pe[:, :half], z, w_pe[:, half:], z], axis=1)


def _rope_tables(seq):
    half = QK_ROPE // 2
    pos = jnp.arange(seq, dtype=F32)
    inv = ROPE_THETA ** (-jnp.arange(0, QK_ROPE, 2, dtype=F32) / QK_ROPE)
    ang = pos[:, None] * inv[None, :]
    c, s = jnp.cos(ang), jnp.sin(ang)
    z = jnp.zeros((seq, LANES_V7X // 2 - half), F32)
    return jnp.concatenate([c, z, c, z], axis=1), jnp.concatenate([-s, z, s, z], axis=1)


def _prep_mla(w_in, w_qb, w_kvb):
    lat = Q_LORA + KV_LORA
    w_in_p = jnp.concatenate([w_in[:, :lat], _rope_group_cols(w_in[:, lat:])], axis=1).astype(BF16)
    hd = QK_NOPE + QK_ROPE
    cols = []
    for h in range(MLA_HEADS):
        cols.append(w_qb[:, h * hd:h * hd + QK_NOPE])
        cols.append(_rope_group_cols(w_qb[:, h * hd + QK_NOPE:(h + 1) * hd]))
    w_kv = w_kvb.reshape(KV_LORA, MLA_HEADS, QK_NOPE + V_HEAD)
    w_k = w_kv[:, :, :QK_NOPE].reshape(KV_LORA, MLA_HEADS * QK_NOPE).astype(BF16)
    w_vt = w_kv[:, :, QK_NOPE:].reshape(KV_LORA, MLA_HEADS * V_HEAD).T.astype(BF16)
    return w_in_p, jnp.concatenate(cols, axis=1).astype(BF16), w_k, w_vt


def _prep_ml_in(w):
    qd = ML_HEADS * ML_QK
    cols = []
    for base in (0, qd):
        for h in range(ML_HEADS):
            cols.append(w[:, base + h * ML_QK:base + (h + 1) * ML_QK])
            cols.append(jnp.zeros((w.shape[0], ML_QK_PAD - ML_QK), w.dtype))
    cols.append(w[:, 2 * qd:])
    return jnp.concatenate(cols, axis=1).astype(BF16)


def _trunk(x, p):
    batch, seq, d = x.shape
    n = batch * seq
    h = x.reshape(n, d)
    for i in range(DEPTH):
        j = i // N_MIXERS
        kind = i % N_MIXERS
        g_mix = p["norm_mix"][i][None, :]
        if kind == 0:
            u, gb = _conv_in(h, g_mix, p["conv_w_in"][j])
            h = _conv_out(u, gb, h, p["conv_w_dw"][j], p["conv_w_out"][j], seq)
        elif kind == 1:
            cos, sin = _rope_tables(seq)
            q, k, vt = _mla_in(h, g_mix, p["mla_w_in"][j], p["mla_q_norm"][j][None, :], p["mla_w_qb"][j],
                               p["mla_kv_norm"][j][None, :], p["mla_w_k"][j], p["mla_w_vt"][j], cos, sin, seq)
            o = _attention(q, k, vt, batch, seq)
            h = _proj_residual(o, h, p["mla_w_out"][j])
        else:
            q, k, v, og, gates = _ml_in(h, g_mix, p["ml_w_in"][j], p["ml_b_gates"][j][None, :])
            gcol = gates.reshape(n, 2, 2 * ML_HEADS).transpose(1, 0, 2)
            grow = gcol.transpose(0, 2, 1)
            hs = _mlstm(q, k, v.T, gcol, grow, batch, seq)
            h = _ml_out(hs, og, p["ml_head_norm"][j][None, :], h, p["ml_w_out"][j])
        h = _moe(h, p["norm_ffn"][i][None, :], p["router_w"][i],
                 p["exp_w_gate"], p["exp_w_up"], p["exp_w_down"], i)
    return _final_norm(h, p["norm_final"][None, :]).reshape(batch, seq, d)


def kernel(x_prompt, x_sample, conv_w_in, conv_w_dw, conv_w_out, mla_w_in, mla_q_norm, mla_w_qb, mla_kv_norm, mla_w_kvb, mla_w_out, ml_w_in, ml_b_gates, ml_head_norm, ml_w_out, norm_mix, norm_ffn, router_w, exp_w_gate, exp_w_up, exp_w_down, norm_final):
    mla = [_prep_mla(mla_w_in[j], mla_w_qb[j], mla_w_kvb[j]) for j in range(mla_w_in.shape[0])]
    p = dict(
        conv_w_in=conv_w_in.astype(BF16), conv_w_dw=conv_w_dw, conv_w_out=conv_w_out.astype(BF16),
        mla_w_in=[m[0] for m in mla], mla_q_norm=mla_q_norm, mla_w_qb=[m[1] for m in mla],
        mla_kv_norm=mla_kv_norm, mla_w_k=[m[2] for m in mla], mla_w_vt=[m[3] for m in mla],
        mla_w_out=mla_w_out.astype(BF16),
        ml_w_in=[_prep_ml_in(ml_w_in[j]) for j in range(ml_w_in.shape[0])], ml_b_gates=ml_b_gates,
        ml_head_norm=ml_head_norm, ml_w_out=ml_w_out.astype(BF16),
        norm_mix=norm_mix, norm_ffn=norm_ffn, router_w=router_w,
        exp_w_gate=exp_w_gate.astype(BF16), exp_w_up=exp_w_up.astype(BF16),
        exp_w_down=exp_w_down.astype(BF16), norm_final=norm_final)
    return (_trunk(x_prompt, p), _trunk(x_sample, p))
```

```python
import functools

import jax
import jax.numpy as jnp
from jax import lax
from jax.experimental import pallas as pl
from jax.experimental.pallas import tpu as pltpu

F32 = jnp.float32
BF16 = jnp.bfloat16

D_MODEL = 1024
DEPTH = 4
N_MIXERS = 3
EPS = 1e-6
CONV_WIDTH = 3
MLA_HEADS = 8
Q_LORA = 384
KV_LORA = 256
QK_NOPE = 128
QK_ROPE = 64
V_HEAD = 128
ROPE_THETA = 10000.0
ML_HEADS = 8
ML_QK = 64
ML_V = 128
N_EXPERTS = 16
EC_CAPACITY = 2
D_EXPERT = 1024

LANES_V7X = 128
BF16_SUBLANES_V7X = 16
VMEM_BYTES_V7X = 64 * 1024 * 1024
VMEM_LIMIT_BYTES = VMEM_BYTES_V7X - 8 * 1024 * 1024

TOKEN_TILE = 512
MLA_TOKEN_TILE = 1024
ATTN_Q_TILE = 1024
ML_CHUNK_TILE = 256
COMBINE_WINDOW = 128
GATHER_ROWS = 32768
GATHER_OPERAND_ROWS = 32768

MLA_HEAD_PAD = 2 * LANES_V7X
VT_ROWS = V_HEAD + BF16_SUBLANES_V7X
LOG2_E = 1.4426950408889634
ML_QK_PAD = LANES_V7X


def _params(*sem):
    return pltpu.CompilerParams(dimension_semantics=sem, vmem_limit_bytes=VMEM_LIMIT_BYTES)


def _rms(x, g):
    ms = jnp.mean(x * x, axis=-1, keepdims=True)
    return x * lax.rsqrt(ms + EPS) * g


def _tile(n, pref):
    t = min(n, pref)
    assert n % t == 0, (n, t)
    return t


def _conv_in_kernel(x_ref, g_ref, w_ref, u_ref, gb_ref):
    d = D_MODEL
    xn = _rms(x_ref[...], g_ref[...]).astype(BF16)
    gb = jnp.dot(xn, w_ref[:, 0:d], preferred_element_type=F32)
    gc = jnp.dot(xn, w_ref[:, d:2 * d], preferred_element_type=F32)
    xv = jnp.dot(xn, w_ref[:, 2 * d:3 * d], preferred_element_type=F32)
    gb_ref[...] = gb.astype(BF16)
    u_ref[...] = (gc * xv).astype(BF16)


def _conv_in(x, g, w):
    n, d = x.shape
    tm = _tile(n, TOKEN_TILE)
    return pl.pallas_call(
        _conv_in_kernel,
        grid=(n // tm,),
        in_specs=[pl.BlockSpec((tm, d), lambda i: (i, 0)),
                  pl.BlockSpec((1, d), lambda i: (0, 0)),
                  pl.BlockSpec((d, 3 * d), lambda i: (0, 0))],
        out_specs=[pl.BlockSpec((tm, d), lambda i: (i, 0)),
                   pl.BlockSpec((tm, d), lambda i: (i, 0))],
        out_shape=[jax.ShapeDtypeStruct((n, d), BF16), jax.ShapeDtypeStruct((n, d), BF16)],
        compiler_params=_params("parallel"),
        name="conv_in",
    )(x, g, w)


def _conv_out_kernel(u_ref, up_ref, un_ref, gb_ref, h_ref, wdw_ref, w_ref, o_ref, *, tiles_per_seq):
    tm = u_ref.shape[0]
    pos = pl.program_id(0) % tiles_per_seq
    u = u_ref[...].astype(F32)
    halo = BF16_SUBLANES_V7X
    prev_row = jnp.where(pos == 0, 0.0, up_ref[halo - 1:halo, :].astype(F32))
    next_row = jnp.where(pos == tiles_per_seq - 1, 0.0, un_ref[0:1, :].astype(F32))
    row = lax.broadcasted_iota(jnp.int32, (tm, 1), 0)
    u_up = jnp.where(row == 0, prev_row, pltpu.roll(u, 1, axis=0))
    u_dn = jnp.where(row == tm - 1, next_row, pltpu.roll(u, tm - 1, axis=0))
    conv = u_up * wdw_ref[0:1, :] + u * wdw_ref[1:2, :] + u_dn * wdw_ref[2:3, :]
    g = (gb_ref[...].astype(F32) * conv).astype(BF16)
    o_ref[...] = h_ref[...] + jnp.dot(g, w_ref[...], preferred_element_type=F32)


def _conv_out(u, gb, h, w_dw, w_out, seq):
    n, d = u.shape
    tm = _tile(seq, TOKEN_TILE)
    halo = BF16_SUBLANES_V7X
    r = tm // halo
    nblk = n // halo
    return pl.pallas_call(
        functools.partial(_conv_out_kernel, tiles_per_seq=seq // tm),
        grid=(n // tm,),
        in_specs=[pl.BlockSpec((tm, d), lambda i: (i, 0)),
                  pl.BlockSpec((halo, d), lambda i: (jnp.maximum(i * r - 1, 0), 0)),
                  pl.BlockSpec((halo, d), lambda i: (jnp.minimum((i + 1) * r, nblk - 1), 0)),
                  pl.BlockSpec((tm, d), lambda i: (i, 0)),
                  pl.BlockSpec((tm, d), lambda i: (i, 0)),
                  pl.BlockSpec((CONV_WIDTH, d), lambda i: (0, 0)),
                  pl.BlockSpec((d, d), lambda i: (0, 0))],
        out_specs=pl.BlockSpec((tm, d), lambda i: (i, 0)),
        out_shape=jax.ShapeDtypeStruct((n, d), F32),
        compiler_params=_params("parallel"),
        name="conv_out",
    )(u, u, u, gb, h, w_dw, w_out)


def _rope_group(x, c, s):
    return x * c + pltpu.roll(x, LANES_V7X // 2, axis=1) * s


def _mla_in_kernel(x_ref, g_ref, win_ref, qn_ref, kvn_ref, wqb_ref, wk_ref, wvt_ref, cos_ref, sin_ref,
                   q_ref, k_ref, vt_ref):
    hp = MLA_HEAD_PAD
    tm = x_ref.shape[0]
    scale = (QK_NOPE + QK_ROPE) ** -0.5 * LOG2_E
    xn = _rms(x_ref[...], g_ref[...]).astype(BF16)
    lat = jnp.dot(xn, win_ref[...], preferred_element_type=F32)
    qn = _rms(lat[:, 0:Q_LORA], qn_ref[...]).astype(BF16)
    kvn = _rms(lat[:, Q_LORA:Q_LORA + KV_LORA], kvn_ref[...]).astype(BF16)
    c = cos_ref[...]
    s = sin_ref[...]
    k_pe = _rope_group(lat[:, Q_LORA + KV_LORA:], c, s).astype(BF16)
    extra = lax.broadcasted_iota(jnp.int32, (VT_ROWS - V_HEAD, tm), 0)
    ones_rows = jnp.where(extra == 0, 1.0, 0.0).astype(BF16)
    for h in range(MLA_HEADS):
        qh = jnp.dot(qn, wqb_ref[:, h * hp:(h + 1) * hp], preferred_element_type=F32)
        q_ref[:, h * hp:h * hp + QK_NOPE] = (qh[:, 0:QK_NOPE] * scale).astype(BF16)
        q_ref[:, h * hp + QK_NOPE:(h + 1) * hp] = (_rope_group(qh[:, QK_NOPE:], c, s) * scale).astype(BF16)
        kh = jnp.dot(kvn, wk_ref[:, h * QK_NOPE:(h + 1) * QK_NOPE], preferred_element_type=F32)
        k_ref[:, h * hp:h * hp + QK_NOPE] = kh.astype(BF16)
        k_ref[:, h * hp + QK_NOPE:(h + 1) * hp] = k_pe
        vt = lax.dot_general(wvt_ref[h * V_HEAD:(h + 1) * V_HEAD, :], kvn, (((1,), (1,)), ((), ())),
                             preferred_element_type=F32)
        vt_ref[0, h, 0:V_HEAD, :] = vt.astype(BF16)
        vt_ref[0, h, V_HEAD:VT_ROWS, :] = ones_rows


def _mla_in(x, g, w_in, q_norm, w_qb, kv_norm, w_k, w_vt, cos, sin, seq):
    n, d = x.shape
    tm = _tile(seq, min(MLA_TOKEN_TILE, max(seq // 8, LANES_V7X)))
    tps = seq // tm
    hq = MLA_HEADS * MLA_HEAD_PAD
    const = lambda i: (0, 0)
    return pl.pallas_call(
        _mla_in_kernel,
        grid=(n // tm,),
        in_specs=[pl.BlockSpec((tm, d), lambda i: (i, 0)),
                  pl.BlockSpec((1, d), const),
                  pl.BlockSpec(w_in.shape, const),
                  pl.BlockSpec((1, Q_LORA), const),
                  pl.BlockSpec((1, KV_LORA), const),
                  pl.BlockSpec(w_qb.shape, const),
                  pl.BlockSpec(w_k.shape, const),
                  pl.BlockSpec(w_vt.shape, const),
                  pl.BlockSpec((tm, LANES_V7X), lambda i: (i % tps, 0)),
                  pl.BlockSpec((tm, LANES_V7X), lambda i: (i % tps, 0))],
        out_specs=[pl.BlockSpec((tm, hq), lambda i: (i, 0)),
                   pl.BlockSpec((tm, hq), lambda i: (i, 0)),
                   pl.BlockSpec((1, MLA_HEADS, VT_ROWS, tm), lambda i: (i, 0, 0, 0))],
        out_shape=[jax.ShapeDtypeStruct((n, hq), BF16),
                   jax.ShapeDtypeStruct((n, hq), BF16),
                   jax.ShapeDtypeStruct((n // tm, MLA_HEADS, VT_ROWS, tm), BF16)],
        compiler_params=_params("parallel"),
        name="mla_in",
    )(x, g, w_in, q_norm, kv_norm, w_qb, w_k, w_vt, cos, sin)


def _attn_kernel(q_ref, k_ref, vt_ref, o_ref, s0, s1, p0, p1, acc_ref):
    tq = q_ref.shape[0]
    nk, _, _, tk = vt_ref.shape
    assert nk % 2 == 0
    q = q_ref[...]

    def scores(j, s_ref):
        start = pl.multiple_of(j * tk, tk)
        k = k_ref[pl.ds(start, tk), :]
        st = lax.dot_general(k, q, (((1,), (1,)), ((), ())), preferred_element_type=F32)
        s_ref[...] = st
        return jnp.max(st, axis=0, keepdims=True)

    def exps(s_ref, p_ref, m, tile_max):
        m_new = jnp.maximum(m, tile_max)
        p_ref[...] = jnp.exp2(s_ref[...] - m_new).astype(BF16)
        return m_new, jnp.exp2(m - m_new)

    def values(j, p_ref, alpha):
        acc_ref[...] = alpha * acc_ref[...] + jnp.dot(vt_ref[j, 0], p_ref[...], preferred_element_type=F32)

    acc_ref[...] = jnp.zeros_like(acc_ref)
    x0 = scores(0, s0)
    x1 = scores(1, s1)
    m, alpha = exps(s0, p0, jnp.full((1, tq), -jnp.inf, F32), x0)

    steps = nk - 2
    unroll = max(u for u in (2, 4, 6, 8, 10) if steps % u == 0) if steps else 2

    def group(i, carry):
        m, alpha, x1 = carry
        for t in range(0, unroll, 2):
            s = unroll * i + t + 1
            x0 = scores(s + 1, s0)
            values(s - 1, p0, alpha)
            m, alpha = exps(s1, p1, m, x1)
            x1 = scores(s + 2, s1)
            values(s, p1, alpha)
            m, alpha = exps(s0, p0, m, x0)
        return m, alpha, x1

    m, alpha, x1 = lax.fori_loop(0, steps // unroll, group, (m, alpha, x1))
    values(nk - 2, p0, alpha)
    m, alpha = exps(s1, p1, m, x1)
    values(nk - 1, p1, alpha)
    acc = acc_ref[...]
    out_t = acc[0:V_HEAD, :] / acc[V_HEAD:V_HEAD + 1, :]
    o_ref[...] = out_t.T.astype(BF16)


def _attention(q, k, vt, batch, seq):
    n = q.shape[0]
    tq = _tile(seq, ATTN_Q_TILE)
    tk = vt.shape[-1]
    nq = seq // tq
    nk = seq // tk
    hp = MLA_HEAD_PAD
    return pl.pallas_call(
        _attn_kernel,
        grid=(batch, MLA_HEADS, nq),
        in_specs=[pl.BlockSpec((tq, hp), lambda b, h, i: (b * nq + i, h)),
                  pl.BlockSpec((seq, hp), lambda b, h, i: (b, h)),
                  pl.BlockSpec((nk, 1, VT_ROWS, tk), lambda b, h, i: (b, h, 0, 0))],
        out_specs=pl.BlockSpec((tq, V_HEAD), lambda b, h, i: (b * nq + i, h)),
        out_shape=jax.ShapeDtypeStruct((n, MLA_HEADS * V_HEAD), BF16),
        scratch_shapes=[pltpu.VMEM((tk, tq), F32), pltpu.VMEM((tk, tq), F32),
                        pltpu.VMEM((tk, tq), BF16), pltpu.VMEM((tk, tq), BF16),
                        pltpu.VMEM((VT_ROWS, tq), F32)],
        compiler_params=_params("parallel", "parallel", "arbitrary"),
        name="mla_attention",
    )(q, k, vt)


def _proj_residual_kernel(a_ref, h_ref, w_ref, o_ref):
    o_ref[...] = h_ref[...] + jnp.dot(a_ref[...], w_ref[...], preferred_element_type=F32)


def _proj_residual(a, h, w):
    n, d = h.shape
    tm = _tile(n, TOKEN_TILE)
    return pl.pallas_call(
        _proj_residual_kernel,
        grid=(n // tm,),
        in_specs=[pl.BlockSpec((tm, a.shape[1]), lambda i: (i, 0)),
                  pl.BlockSpec((tm, d), lambda i: (i, 0)),
                  pl.BlockSpec(w.shape, lambda i: (0, 0))],
        out_specs=pl.BlockSpec((tm, d), lambda i: (i, 0)),
        out_shape=jax.ShapeDtypeStruct((n, d), F32),
        compiler_params=_params("parallel"),
        name="proj_residual",
    )(a, h, w)


def _ml_in_kernel(x_ref, g_ref, w_ref, b_ref, q_ref, k_ref, v_ref, o_ref, gt_ref):
    qw = ML_HEADS * ML_QK_PAD
    vw = ML_HEADS * ML_V
    xn = _rms(x_ref[...], g_ref[...]).astype(BF16)
    q = jnp.dot(xn, w_ref[:, 0:qw], preferred_element_type=F32)
    q_ref[...] = (q * (ML_QK ** -0.5)).astype(BF16)
    k_ref[...] = jnp.dot(xn, w_ref[:, qw:2 * qw], preferred_element_type=F32).astype(BF16)
    v_ref[...] = jnp.dot(xn, w_ref[:, 2 * qw:2 * qw + vw], preferred_element_type=F32).astype(BF16)
    o_ref[...] = jnp.dot(xn, w_ref[:, 2 * qw + vw:2 * qw + 2 * vw], preferred_element_type=F32).astype(BF16)
    gt_ref[...] = jnp.dot(xn, w_ref[:, 2 * qw + 2 * vw:], preferred_element_type=F32) + b_ref[...]


def _ml_in(x, g, w, b_gates):
    n, d = x.shape
    tm = _tile(n, TOKEN_TILE)
    qw = ML_HEADS * ML_QK_PAD
    vw = ML_HEADS * ML_V
    ng = 4 * ML_HEADS
    row = lambda i: (i, 0)
    const = lambda i: (0, 0)
    return pl.pallas_call(
        _ml_in_kernel,
        grid=(n // tm,),
        in_specs=[pl.BlockSpec((tm, d), row), pl.BlockSpec((1, d), const),
                  pl.BlockSpec(w.shape, const), pl.BlockSpec((1, ng), const)],
        out_specs=[pl.BlockSpec((tm, qw), row), pl.BlockSpec((tm, qw), row),
                   pl.BlockSpec((tm, vw), row), pl.BlockSpec((tm, vw), row),
                   pl.BlockSpec((tm, ng), row)],
        out_shape=[jax.ShapeDtypeStruct((n, qw), BF16), jax.ShapeDtypeStruct((n, qw), BF16),
                   jax.ShapeDtypeStruct((n, vw), BF16), jax.ShapeDtypeStruct((n, vw), BF16),
                   jax.ShapeDtypeStruct((n, ng), F32)],
        compiler_params=_params("parallel"),
        name="mlstm_in",
    )(x, g, w, b_gates)


def _split3(x):
    hi = x.astype(BF16)
    r1 = x - hi.astype(F32)
    mid = r1.astype(BF16)
    lo = (r1 - mid.astype(F32)).astype(BF16)
    return hi, mid, lo


def _mlstm_kernel(q_ref, k_ref, vt_ref, gc_ref, gr_ref, o_ref, c_scr, m_scr):
    lc = q_ref.shape[0]
    nh = ML_HEADS
    kp = ML_QK_PAD
    dv = ML_V
    fwd = pl.program_id(0) == 0

    @pl.when(pl.program_id(2) == 0)
    def _():
        c_scr[...] = jnp.zeros_like(c_scr)
        m_scr[...] = jnp.zeros_like(m_scr)

    gcol = gc_ref[0]
    grow = gr_ref[0]
    li_col = gcol[:, 0:nh]
    lf_col = jax.nn.log_sigmoid(gcol[:, nh:2 * nh])
    li_row = grow[0:nh, :]
    lf_row = jax.nn.log_sigmoid(grow[nh:2 * nh, :])

    r = lax.broadcasted_iota(jnp.int32, (lc, lc), 0)
    cidx = lax.broadcasted_iota(jnp.int32, (lc, lc), 1)
    sign = jnp.where(fwd, 1, -1)
    tri = jnp.where((r - cidx) * sign >= 0, 1.0, 0.0).astype(BF16)
    allowed_t = (cidx - r) * sign >= 0

    b_col = jnp.zeros((lc, nh), F32)
    for piece in _split3(lf_col):
        b_col = b_col + jnp.dot(tri, piece, preferred_element_type=F32)
    b_row = jnp.zeros((nh, lc), F32)
    for piece in _split3(lf_row):
        b_row = b_row + lax.dot_general(piece, tri, (((1,), (1,)), ((), ())), preferred_element_type=F32)
    a_all = jnp.sum(lf_row, axis=-1, keepdims=True)
    src_col = li_col - b_col

    sub = lax.broadcasted_iota(jnp.int32, (dv, lc), 0)
    ones_rows = jnp.where(sub == 0, 1.0, 0.0).astype(BF16)
    nt = (((1,), (1,)), ((), ()))

    for h in range(nh):
        q = q_ref[:, h * kp:(h + 1) * kp]
        k = k_ref[:, h * kp:(h + 1) * kp]
        vt_ext = jnp.concatenate([vt_ref[h * dv:(h + 1) * dv, :], ones_rows], axis=0)
        b_r = b_row[h:h + 1, :]
        li_r = li_row[h:h + 1, :]
        a = a_all[h:h + 1, :]
        m_in = m_scr[h][0:1, 0:1]
        ct_ext = c_scr[h]

        w_end = a - b_r + li_r
        g = jnp.max(w_end, axis=-1, keepdims=True)
        e_r = jnp.exp(w_end - g)

        dmat = jnp.where(allowed_t, b_r + src_col[:, h:h + 1], -jnp.inf)
        inter_log = b_r + m_in
        m_j = jnp.maximum(inter_log, jnp.max(dmat, axis=0, keepdims=True))
        inter = jnp.exp(inter_log - m_j)
        p = jnp.exp(dmat - m_j)
        s = lax.dot_general(k, q, nt, preferred_element_type=F32)
        qk = (s * p).astype(BF16)
        nd = (jnp.dot(vt_ext, qk, preferred_element_type=F32)
              + inter * lax.dot_general(ct_ext.astype(BF16), q, nt, preferred_element_type=F32))
        den = nd[dv:dv + 1, :]
        out_t = nd[0:dv, :] / jnp.maximum(jnp.abs(den), jnp.exp(-m_j))
        o_ref[0, :, h * dv:(h + 1) * dv] = out_t.T

        m_new = jnp.maximum(a + m_in, g)
        fdec = jnp.exp(a + m_in - m_new)
        iin = jnp.exp(g - m_new)
        vte = (vt_ext.astype(F32) * e_r).astype(BF16)
        c_scr[h] = fdec * ct_ext + iin * jnp.dot(vte, k, preferred_element_type=F32)
        m_scr[h] = jnp.broadcast_to(m_new, m_scr.shape[1:])


def _mlstm(q, k, vt, gcol, grow, batch, seq):
    n = q.shape[0]
    lc = _tile(seq, ML_CHUNK_TILE)
    nc = seq // lc
    qw = ML_HEADS * ML_QK_PAD
    vw = ML_HEADS * ML_V

    def chunk(d, b, c):
        return b * nc + c + d * (nc - 1 - 2 * c)

    return pl.pallas_call(
        _mlstm_kernel,
        grid=(2, batch, nc),
        in_specs=[pl.BlockSpec((lc, qw), lambda d, b, c: (chunk(d, b, c), 0)),
                  pl.BlockSpec((lc, qw), lambda d, b, c: (chunk(d, b, c), 0)),
                  pl.BlockSpec((vw, lc), lambda d, b, c: (0, chunk(d, b, c))),
                  pl.BlockSpec((1, lc, 2 * ML_HEADS), lambda d, b, c: (d, chunk(d, b, c), 0)),
                  pl.BlockSpec((1, 2 * ML_HEADS, lc), lambda d, b, c: (d, 0, chunk(d, b, c)))],
        out_specs=pl.BlockSpec((1, lc, vw), lambda d, b, c: (d, chunk(d, b, c), 0)),
        out_shape=jax.ShapeDtypeStruct((2, n, vw), F32),
        scratch_shapes=[pltpu.VMEM((ML_HEADS, 2 * ML_V, ML_QK_PAD), F32),
                        pltpu.VMEM((ML_HEADS, 8, LANES_V7X), F32)],
        compiler_params=_params("parallel", "parallel", "arbitrary"),
        name="mlstm_scan",
    )(q, k, vt, gcol, grow)


def _ml_out_kernel(hs_ref, og_ref, hn_ref, h_ref, w_ref, o_ref):
    dv = ML_V
    hs = hs_ref[0] + hs_ref[1]
    parts = []
    for hd in range(ML_HEADS):
        x = hs[:, hd * dv:(hd + 1) * dv]
        parts.append(x * lax.rsqrt(jnp.mean(x * x, axis=-1, keepdims=True) + EPS))
    y = jnp.concatenate(parts, axis=1) * hn_ref[...] * jax.nn.sigmoid(og_ref[...].astype(F32))
    o_ref[...] = h_ref[...] + jnp.dot(y.astype(BF16), w_ref[...], preferred_element_type=F32)


def _ml_out(hs, og, head_norm, h, w):
    n, d = h.shape
    vw = ML_HEADS * ML_V
    tm = _tile(n, TOKEN_TILE)
    row = lambda i: (i, 0)
    const = lambda i: (0, 0)
    return pl.pallas_call(
        _ml_out_kernel,
        grid=(n // tm,),
        in_specs=[pl.BlockSpec((2, tm, vw), lambda i: (0, i, 0)),
                  pl.BlockSpec((tm, vw), row), pl.BlockSpec((1, vw), const),
                  pl.BlockSpec((tm, d), row), pl.BlockSpec(w.shape, const)],
        out_specs=pl.BlockSpec((tm, d), row),
        out_shape=jax.ShapeDtypeStruct((n, d), F32),
        compiler_params=_params("parallel"),
        name="mlstm_out",
    )(hs, og, head_norm, h, w)


def _router_kernel(x_ref, g_ref, w_ref, wlo_ref, xn_ref, aff_ref, *, nreal):
    @pl.when(pl.program_id(0) < nreal)
    def _():
        xf = _rms(x_ref[...], g_ref[...])
        xn = xf.astype(BF16)
        xn_ref[...] = xn
        xlo = (xf - xn.astype(F32)).astype(BF16)
        logits = (jnp.dot(xn, w_ref[...], preferred_element_type=F32)
                  + jnp.dot(xlo, w_ref[...], preferred_element_type=F32)
                  + jnp.dot(xn, wlo_ref[...], preferred_element_type=F32))
        z = jnp.exp(logits - jnp.max(logits, axis=-1, keepdims=True))
        aff_ref[...] = z / jnp.sum(z, axis=-1, keepdims=True)

    @pl.when(pl.program_id(0) >= nreal)
    def _():
        xn_ref[...] = jnp.zeros_like(xn_ref)


def _router(x, g, w_f32):
    n, d = x.shape
    tm = _tile(n, TOKEN_TILE)
    nreal = n // tm
    npad = max(n, GATHER_OPERAND_ROWS)
    row = lambda i: (jnp.minimum(i, nreal - 1), 0)
    const = lambda i: (0, 0)
    w = w_f32.astype(BF16)
    w_lo = (w_f32 - w.astype(F32)).astype(BF16)
    return pl.pallas_call(
        functools.partial(_router_kernel, nreal=nreal),
        grid=(npad // tm,),
        in_specs=[pl.BlockSpec((tm, d), row), pl.BlockSpec((1, d), const),
                  pl.BlockSpec(w.shape, const), pl.BlockSpec(w.shape, const)],
        out_specs=[pl.BlockSpec((tm, d), lambda i: (i, 0)), pl.BlockSpec((tm, N_EXPERTS), row)],
        out_shape=[jax.ShapeDtypeStruct((npad, d), BF16), jax.ShapeDtypeStruct((n, N_EXPERTS), F32)],
        compiler_params=_params("arbitrary"),
        name="moe_router",
    )(x, g, w, w_lo)


def _ffn_kernel(*refs, nchunk, epc):
    x_refs = refs[:nchunk]
    gate_ref, wg_ref, wu_ref, wd_ref, o_ref = refs[nchunk:]

    def compute(x):
        g = jnp.dot(x, wg_ref[0, 0], preferred_element_type=F32)
        u = jnp.dot(x, wu_ref[0, 0], preferred_element_type=F32)
        hid = (g * jax.nn.sigmoid(g) * u).astype(BF16)
        y = jnp.dot(hid, wd_ref[0, 0], preferred_element_type=F32)
        o_ref[0] = (y * gate_col()).astype(BF16)

    def gate_col():
        tm = o_ref.shape[1]
        groups = tm // LANES_V7X
        g = gate_ref[0, pl.ds(pl.program_id(1) * groups, groups), :]
        eye = (lax.broadcasted_iota(jnp.int32, (LANES_V7X, LANES_V7X), 0)
               == lax.broadcasted_iota(jnp.int32, (LANES_V7X, LANES_V7X), 1))
        return jnp.concatenate([jnp.sum(jnp.where(eye, g[k:k + 1, :], 0.0), axis=1, keepdims=True)
                                for k in range(groups)], axis=0)

    if nchunk == 1:
        compute(x_refs[0][0])
    else:
        chunk = pl.program_id(0) // epc
        for c in range(nchunk):
            @pl.when(chunk == c)
            def _(c=c):
                compute(x_refs[c][0])


def _expert_ffn(xes, gate, wg, wu, wd, layer):
    nchunk = len(xes)
    epc, cap, d = xes[0].shape
    e = nchunk * epc
    f = wg.shape[-1]
    tm = _tile(cap, TOKEN_TILE)
    nj = cap // tm
    tok = lambda i, j: (i, j, 0)
    wmap = lambda i, j: (layer, i, 0, 0)

    def chunk_map(c):
        def index(i, j):
            jj = jnp.where(i < c * epc, 0, jnp.where(i >= (c + 1) * epc, nj - 1, j))
            return (jnp.clip(i - c * epc, 0, epc - 1), jj, 0)
        return index

    return pl.pallas_call(
        functools.partial(_ffn_kernel, nchunk=nchunk, epc=epc),
        grid=(e, nj),
        in_specs=[pl.BlockSpec((1, tm, d), chunk_map(c)) for c in range(nchunk)]
        + [pl.BlockSpec((1, cap // LANES_V7X, LANES_V7X), lambda i, j: (i, 0, 0)),
           pl.BlockSpec((1, 1, d, f), wmap), pl.BlockSpec((1, 1, d, f), wmap),
           pl.BlockSpec((1, 1, f, d), wmap)],
        out_specs=pl.BlockSpec((1, tm, d), tok),
        out_shape=jax.ShapeDtypeStruct((e, cap, d), BF16),
        compiler_params=_params("parallel", "arbitrary"),
        name="moe_ffn",
    )(*xes, gate, wg, wu, wd)


def _threshold_kernel(a_ref, t_ref, need_ref, *, cap):
    bits = pltpu.bitcast(a_ref[...], jnp.int32)
    ne = bits.shape[0]

    def count(mask):
        return jnp.sum(jnp.where(mask, 1.0, 0.0), axis=1, keepdims=True)

    t = jnp.zeros((ne, 1), jnp.int32)
    for k in range(30, -1, -1):
        cand = t | (1 << k)
        t = jnp.where(count(bits >= cand) >= cap, cand, t)
    need = cap - count(bits > t)
    t_ref[...] = jnp.broadcast_to(t, t_ref.shape)
    need_ref[...] = jnp.broadcast_to(need, need_ref.shape)


def _threshold(aff_t, cap):
    e, n = aff_t.shape
    return pl.pallas_call(
        functools.partial(_threshold_kernel, cap=cap),
        grid=(1,),
        in_specs=[pl.BlockSpec((e, n), lambda i: (0, 0))],
        out_specs=[pl.BlockSpec((e, LANES_V7X), lambda i: (0, 0)), pl.BlockSpec((e, LANES_V7X), lambda i: (0, 0))],
        out_shape=[jax.ShapeDtypeStruct((e, LANES_V7X), jnp.int32), jax.ShapeDtypeStruct((e, LANES_V7X), F32)],
        compiler_params=_params("arbitrary"),
        name="moe_threshold",
    )(aff_t)


def _positions_kernel(aff_ref, thr_ref, need_ref, pos_ref, gate_ref, cnt_ref, rc, tc):
    tb = aff_ref.shape[0]

    @pl.when(pl.program_id(0) == 0)
    def _():
        rc[...] = jnp.zeros_like(rc)
        tc[...] = jnp.zeros_like(tc)

    a = aff_ref[...]
    bits = pltpu.bitcast(a, jnp.int32)
    thr = thr_ref[...]
    r = lax.broadcasted_iota(jnp.int32, (tb, tb), 0)
    c = lax.broadcasted_iota(jnp.int32, (tb, tb), 1)
    tril = jnp.where(r >= c, 1.0, 0.0).astype(BF16)
    eq = jnp.where(bits == thr, 1.0, 0.0)
    eq_incl = jnp.dot(tril, eq.astype(BF16), preferred_element_type=F32)
    tie_rank = tc[0:1, :] + eq_incl - eq
    sel = jnp.where(bits > thr, 1.0, jnp.where(tie_rank < need_ref[...], eq, 0.0))
    sel_incl = jnp.dot(tril, sel.astype(BF16), preferred_element_type=F32)
    taken = sel > 0.5
    pos = (rc[0:1, :] + sel_incl - 1.0).astype(jnp.int32)
    pos_ref[...] = jnp.where(taken, pos, -1)
    gate_ref[...] = jnp.where(taken, a, 0.0)
    cnt_ref[0] = jnp.broadcast_to(rc[0:1, :], cnt_ref.shape[1:]).astype(jnp.int32)
    rc[...] = rc[...] + jnp.sum(sel, axis=0, keepdims=True)
    tc[...] = tc[...] + jnp.sum(eq, axis=0, keepdims=True)


def _positions(aff, thr_row, need_row):
    n, e = aff.shape
    tb = _tile(n, TOKEN_TILE)
    nb = n // tb
    row = lambda i: (i, 0)
    const = lambda i: (0, 0)
    return pl.pallas_call(
        _positions_kernel,
        grid=(nb,),
        in_specs=[pl.BlockSpec((tb, e), row), pl.BlockSpec((1, e), const), pl.BlockSpec((1, e), const)],
        out_specs=[pl.BlockSpec((tb, e), row), pl.BlockSpec((tb, e), row),
                   pl.BlockSpec((1, 8, e), lambda i: (i, 0, 0))],
        out_shape=[jax.ShapeDtypeStruct((n, e), jnp.int32), jax.ShapeDtypeStruct((n, e), F32),
                   jax.ShapeDtypeStruct((nb, 8, e), jnp.int32)],
        scratch_shapes=[pltpu.VMEM((8, e), F32), pltpu.VMEM((8, e), F32)],
        compiler_params=_params("arbitrary"),
        name="moe_positions",
    )(aff, thr_row, need_row)


def _compact_kernel(lo_ref, pt_ref, gt_ref, idx_ref, gate_ref):
    ne, tb = pt_ref.shape
    _, nrow, lanes = idx_ref.shape
    cap = nrow * lanes
    win = COMBINE_WINDOW
    align = BF16_SUBLANES_V7X
    b = pl.program_id(0)

    @pl.when(b == 0)
    def _():
        idx_ref[...] = jnp.zeros_like(idx_ref)
        gate_ref[...] = jnp.zeros_like(gate_ref)

    tok = b * tb + lax.broadcasted_iota(jnp.int32, (1, tb), 1)
    digits = [(tok >> 8).astype(F32).astype(BF16), (tok & 255).astype(F32).astype(BF16)]
    pad = jnp.zeros((3, tb), BF16)
    slot = lax.broadcasted_iota(jnp.int32, (win, tb), 0)
    zeros = jnp.zeros((1, lanes), F32)
    nt = (((1,), (1,)), ((), ()))

    def window(e, w, start0):
        nominal = start0 + w * win
        start = jnp.minimum(nominal, cap - win)
        rel = pt_ref[e:e + 1, :] - start
        rel = jnp.where(rel >= nominal - start, rel, -1)
        sel_t = jnp.where(slot == rel, 1.0, 0.0).astype(BF16)
        lhs = jnp.concatenate(digits + list(_split3(gt_ref[e:e + 1, :])) + [pad], axis=0)
        got = lax.dot_general(lhs, sel_t, nt, preferred_element_type=F32)
        ids = got[0:1, :] * 256.0 + got[1:2, :]
        gts = got[2:3, :] + got[3:4, :] + got[4:5, :]
        off = start % lanes
        r0 = start // lanes
        r1 = jnp.minimum(r0 + 1, nrow - 1)
        ids2 = pltpu.roll(jnp.concatenate([ids, zeros], axis=1), off, axis=1)
        gts2 = pltpu.roll(jnp.concatenate([gts, zeros], axis=1), off, axis=1)
        idx_ref[e, pl.ds(r0, 1), :] += ids2[:, 0:lanes].astype(jnp.int32)
        idx_ref[e, pl.ds(r1, 1), :] += ids2[:, lanes:].astype(jnp.int32)
        gate_ref[e, pl.ds(r0, 1), :] += gts2[:, 0:lanes]
        gate_ref[e, pl.ds(r1, 1), :] += gts2[:, lanes:]

    starts = [lo_ref[b, e] // align * align for e in range(ne)]
    for e in range(ne):
        window(e, 0, starts[e])
    for e in range(ne):
        nwin = (lo_ref[b + 1, e] - starts[e] + win - 1) // win
        lax.fori_loop(1, nwin, lambda w, c, e=e: (window(e, w, starts[e]), c)[1], 0)


def _compact(lo, pos_t, gate_t, cap):
    e, n = pos_t.shape
    assert COMBINE_WINDOW == LANES_V7X
    tb = _tile(n, TOKEN_TILE)
    shape3 = (e, cap // LANES_V7X, LANES_V7X)
    grid_spec = pltpu.PrefetchScalarGridSpec(
        num_scalar_prefetch=1,
        grid=(n // tb,),
        in_specs=[pl.BlockSpec((e, tb), lambda i, lo_ref: (0, i)),
                  pl.BlockSpec((e, tb), lambda i, lo_ref: (0, i))],
        out_specs=[pl.BlockSpec(shape3, lambda i, lo_ref: (0, 0, 0)),
                   pl.BlockSpec(shape3, lambda i, lo_ref: (0, 0, 0))])
    return pl.pallas_call(
        _compact_kernel,
        grid_spec=grid_spec,
        out_shape=[jax.ShapeDtypeStruct(shape3, jnp.int32), jax.ShapeDtypeStruct(shape3, F32)],
        compiler_params=_params("arbitrary"),
        name="moe_compact",
    )(lo, pos_t, gate_t)


def _combine_kernel(lo_ref, h_ref, idx_ref, ye_hbm, *rest, final):
    (gf_ref, o_ref, buf, sem) = rest if final else (None,) + rest
    tb, d = h_ref.shape
    ne, nrow, lanes = idx_ref.shape
    cap = nrow * lanes
    win = COMBINE_WINDOW
    align = BF16_SUBLANES_V7X
    b = pl.program_id(0)
    nb = pl.num_programs(0)
    tok = b * tb + lax.broadcasted_iota(jnp.int32, (tb, 1), 0)
    lane = lax.broadcasted_iota(jnp.int32, (1, win), 1)

    def clamp(start):
        return jnp.minimum(start, cap - win)

    def first_start(blk, e):
        return clamp(lo_ref[e, blk] // align * align)

    def fetch(e, start):
        return pltpu.make_async_copy(ye_hbm.at[e, pl.ds(pl.multiple_of(start, align), win), :],
                                     buf.at[pl.ds(e * win, win)], sem.at[e])

    def select(e, start, row_lo, row_hi):
        r0 = start // lanes
        two = jnp.concatenate([idx_ref[e, pl.ds(r0, 1), :],
                               idx_ref[e, pl.ds(jnp.minimum(r0 + 1, nrow - 1), 1), :]], axis=1)
        ids = pltpu.roll(two, (2 * lanes - start % lanes) % (2 * lanes), axis=1)[:, 0:win]
        rid = start + lane
        ids = jnp.where((rid >= row_lo) & (rid < row_hi), ids, -1)
        return jnp.where(tok == ids, 1.0, 0.0).astype(BF16)

    @pl.when(b == 0)
    def _():
        for e in range(ne):
            fetch(e, first_start(0, e)).start()

    o_ref[...] = h_ref[...]
    group = 4
    total = None
    for e0 in range(0, ne, group):
        first = []
        for e in range(e0, e0 + group):
            lo = lo_ref[e, b]
            hi = lo_ref[e, b + 1]
            start0 = lo // align * align
            first.append((e, lo, hi, start0, clamp(start0)))
        for e, lo, hi, start0, start in first:
            fetch(e, start).wait()
        sel = jnp.concatenate([select(e, start, lo, hi) for e, lo, hi, start0, start in first], axis=1)
        part = jnp.dot(sel, buf[e0 * win:(e0 + group) * win, :], preferred_element_type=F32)
        total = part if total is None else total + part

        for e, lo, hi, start0, start in first:
            def more(w, carry, e=e, lo=lo, hi=hi, start0=start0):
                nominal = start0 + w * win
                st = clamp(nominal)
                cp = fetch(e, st)
                cp.start()
                cp.wait()
                o_ref[...] += jnp.dot(select(e, st, jnp.maximum(lo, nominal), hi), buf[e * win:(e + 1) * win, :],
                                      preferred_element_type=F32)
                return carry

            nwin = (hi - start0 + win - 1) // win
            lax.fori_loop(1, nwin, more, 0)

            @pl.when(b + 1 < nb)
            def _(e=e):
                fetch(e, first_start(b + 1, e)).start()

    if final:
        o_ref[...] = _rms(o_ref[...] + total, gf_ref[...])
    else:
        o_ref[...] += total


def _combine(h, lo, idx3, ye, g_final=None):
    n, d = h.shape
    e, cap, _ = ye.shape
    assert cap >= COMBINE_WINDOW and cap % LANES_V7X == 0 and e % 4 == 0 and COMBINE_WINDOW <= LANES_V7X
    tb = _tile(n, TOKEN_TILE)
    nb = n // tb
    final = g_final is not None
    grid_spec = pltpu.PrefetchScalarGridSpec(
        num_scalar_prefetch=1,
        grid=(nb,),
        in_specs=[pl.BlockSpec((tb, d), lambda i, lo_ref: (i, 0)),
                  pl.BlockSpec(idx3.shape, lambda i, lo_ref: (0, 0, 0)),
                  pl.BlockSpec(memory_space=pl.ANY)]
        + ([pl.BlockSpec((1, d), lambda i, lo_ref: (0, 0))] if final else []),
        out_specs=pl.BlockSpec((tb, d), lambda i, lo_ref: (i, 0)),
        scratch_shapes=[pltpu.VMEM((e * COMBINE_WINDOW, d), BF16), pltpu.SemaphoreType.DMA((e,))])
    return pl.pallas_call(
        functools.partial(_combine_kernel, final=final),
        grid_spec=grid_spec,
        out_shape=jax.ShapeDtypeStruct((n, d), F32),
        compiler_params=_params("arbitrary"),
        name="moe_combine",
    )(lo, h, idx3, ye, *([g_final] if final else []))


def _moe(h, g, w_router, wg, wu, wd, layer, g_final=None):
    n, d = h.shape
    cap = EC_CAPACITY * n // N_EXPERTS
    xn, aff = _router(h, g, w_router)
    thr, need = _threshold(aff.T, cap)
    pos, gate, cnt = _positions(aff, thr[:, 0][None, :], need[:, 0][None, :])
    lo = jnp.concatenate([cnt[:, 0, :], jnp.full((1, N_EXPERTS), cap, jnp.int32)], axis=0)
    idx3, gate3 = _compact(lo, pos.T, gate.T, cap)
    idx = idx3.reshape(N_EXPERTS, cap)
    nchunk = max(1, idx.size // GATHER_ROWS)
    epc = N_EXPERTS // nchunk
    xes = [xn[idx[c * epc:(c + 1) * epc]] for c in range(nchunk)]
    ye = _expert_ffn(xes, gate3, wg, wu, wd, layer)
    return _combine(h, lo.T, idx3, ye, g_final)


def _rope_group_cols(w_pe):
    half = QK_ROPE // 2
    z = jnp.zeros((w_pe.shape[0], LANES_V7X // 2 - half), w_pe.dtype)
    return jnp.concatenate([w_pe[:, :half], z, w_pe[:, half:], z], axis=1)


def _rope_tables(seq):
    half = QK_ROPE // 2
    pos = jnp.arange(seq, dtype=F32)
    inv = ROPE_THETA ** (-jnp.arange(0, QK_ROPE, 2, dtype=F32) / QK_ROPE)
    ang = pos[:, None] * inv[None, :]
    c, s = jnp.cos(ang), jnp.sin(ang)
    z = jnp.zeros((seq, LANES_V7X // 2 - half), F32)
    return jnp.concatenate([c, z, c, z], axis=1), jnp.concatenate([-s, z, s, z], axis=1)


def _prep_mla(w_in, w_qb, w_kvb):
    lat = Q_LORA + KV_LORA
    w_in_p = jnp.concatenate([w_in[:, :lat], _rope_group_cols(w_in[:, lat:])], axis=1).astype(BF16)
    hd = QK_NOPE + QK_ROPE
    cols = []
    for h in range(MLA_HEADS):
        cols.append(w_qb[:, h * hd:h * hd + QK_NOPE])
        cols.append(_rope_group_cols(w_qb[:, h * hd + QK_NOPE:(h + 1) * hd]))
    w_kv = w_kvb.reshape(KV_LORA, MLA_HEADS, QK_NOPE + V_HEAD)
    w_k = w_kv[:, :, :QK_NOPE].reshape(KV_LORA, MLA_HEADS * QK_NOPE).astype(BF16)
    w_vt = w_kv[:, :, QK_NOPE:].reshape(KV_LORA, MLA_HEADS * V_HEAD).T.astype(BF16)
    return w_in_p, jnp.concatenate(cols, axis=1).astype(BF16), w_k, w_vt


def _prep_ml_in(w):
    qd = ML_HEADS * ML_QK
    cols = []
    for base in (0, qd):
        for h in range(ML_HEADS):
            cols.append(w[:, base + h * ML_QK:base + (h + 1) * ML_QK])
            cols.append(jnp.zeros((w.shape[0], ML_QK_PAD - ML_QK), w.dtype))
    cols.append(w[:, 2 * qd:])
    return jnp.concatenate(cols, axis=1).astype(BF16)


def _trunk(x, p):
    batch, seq, d = x.shape
    n = batch * seq
    h = x.reshape(n, d)
    for i in range(DEPTH):
        j = i // N_MIXERS
        kind = i % N_MIXERS
        g_mix = p["norm_mix"][i][None, :]
        if kind == 0:
            u, gb = _conv_in(h, g_mix, p["conv_w_in"][j])
            h = _conv_out(u, gb, h, p["conv_w_dw"][j], p["conv_w_out"][j], seq)
        elif kind == 1:
            cos, sin = _rope_tables(seq)
            q, k, vt = _mla_in(h, g_mix, p["mla_w_in"][j], p["mla_q_norm"][j][None, :], p["mla_w_qb"][j],
                               p["mla_kv_norm"][j][None, :], p["mla_w_k"][j], p["mla_w_vt"][j], cos, sin, seq)
            o = _attention(q, k, vt, batch, seq)
            h = _proj_residual(o, h, p["mla_w_out"][j])
        else:
            q, k, v, og, gates = _ml_in(h, g_mix, p["ml_w_in"][j], p["ml_b_gates"][j][None, :])
            gcol = gates.reshape(n, 2, 2 * ML_HEADS).transpose(1, 0, 2)
            grow = gcol.transpose(0, 2, 1)
            hs = _mlstm(q, k, v.T, gcol, grow, batch, seq)
            h = _ml_out(hs, og, p["ml_head_norm"][j][None, :], h, p["ml_w_out"][j])
        g_final = p["norm_final"][None, :] if i == DEPTH - 1 else None
        h = _moe(h, p["norm_ffn"][i][None, :], p["router_w"][i],
                 p["exp_w_gate"], p["exp_w_up"], p["exp_w_down"], i, g_final)
    return h.reshape(batch, seq, d)


def kernel(x_prompt, x_sample, conv_w_in, conv_w_dw, conv_w_out, mla_w_in, mla_q_norm, mla_w_qb, mla_kv_norm, mla_w_kvb, mla_w_out, ml_w_in, ml_b_gates, ml_head_norm, ml_w_out, norm_mix, norm_ffn, router_w, exp_w_gate, exp_w_up, exp_w_down, norm_final):
    mla = [_prep_mla(mla_w_in[j], mla_w_qb[j], mla_w_kvb[j]) for j in range(mla_w_in.shape[0])]
    p = dict(
        conv_w_in=conv_w_in.astype(BF16), conv_w_dw=conv_w_dw, conv_w_out=conv_w_out.astype(BF16),
        mla_w_in=[m[0] for m in mla], mla_q_norm=mla_q_norm, mla_w_qb=[m[1] for m in mla],
        mla_kv_norm=mla_kv_norm, mla_w_k=[m[2] for m in mla], mla_w_vt=[m[3] for m in mla],
        mla_w_out=mla_w_out.astype(BF16),
        ml_w_in=[_prep_ml_in(ml_w_in[j]) for j in range(ml_w_in.shape[0])], ml_b_gates=ml_b_gates,
        ml_head_norm=ml_head_norm, ml_w_out=ml_w_out.astype(BF16),
        norm_mix=norm_mix, norm_ffn=norm_ffn, router_w=router_w,
        exp_w_gate=exp_w_gate.astype(BF16), exp_w_up=exp_w_up.astype(BF16),
        exp_w_down=exp_w_down.astype(BF16), norm_final=norm_final)
    return (_trunk(x_prompt, p), _trunk(x_sample, p))
```

```python
import functools

import jax
import jax.numpy as jnp
from jax import lax
from jax.experimental import pallas as pl
from jax.experimental.pallas import tpu as pltpu

F32 = jnp.float32
BF16 = jnp.bfloat16

D_MODEL = 1024
DEPTH = 4
N_MIXERS = 3
EPS = 1e-6
CONV_WIDTH = 3
MLA_HEADS = 8
Q_LORA = 384
KV_LORA = 256
QK_NOPE = 128
QK_ROPE = 64
V_HEAD = 128
ROPE_THETA = 10000.0
ML_HEADS = 8
ML_QK = 64
ML_V = 128
N_EXPERTS = 16
EC_CAPACITY = 2
D_EXPERT = 1024

LANES_V7X = 128
BF16_SUBLANES_V7X = 16
VMEM_BYTES_V7X = 64 * 1024 * 1024
VMEM_LIMIT_BYTES = VMEM_BYTES_V7X - 8 * 1024 * 1024

TOKEN_TILE = 512
MLA_TOKEN_TILE = 1024
ATTN_Q_TILE = 1024
ML_CHUNK_TILE = 256
COMBINE_WINDOW = 128
GATHER_ROWS = 32768
GATHER_OPERAND_ROWS = 32768

MLA_HEAD_PAD = 2 * LANES_V7X
VT_ROWS = V_HEAD + BF16_SUBLANES_V7X
LOG2_E = 1.4426950408889634
ML_QK_PAD = LANES_V7X


def _params(*sem):
    return pltpu.CompilerParams(dimension_semantics=sem, vmem_limit_bytes=VMEM_LIMIT_BYTES)


def _rms(x, g):
    ms = jnp.mean(x * x, axis=-1, keepdims=True)
    return x * lax.rsqrt(ms + EPS) * g


def _tile(n, pref):
    t = min(n, pref)
    assert n % t == 0, (n, t)
    return t


def _conv_in_kernel(x_ref, g_ref, w_ref, u_ref, gb_ref):
    d = D_MODEL
    xn = _rms(x_ref[...], g_ref[...]).astype(BF16)
    gb = jnp.dot(xn, w_ref[:, 0:d], preferred_element_type=F32)
    gc = jnp.dot(xn, w_ref[:, d:2 * d], preferred_element_type=F32)
    xv = jnp.dot(xn, w_ref[:, 2 * d:3 * d], preferred_element_type=F32)
    gb_ref[...] = gb.astype(BF16)
    u_ref[...] = (gc * xv).astype(BF16)


def _conv_in(x, g, w):
    n, d = x.shape
    tm = _tile(n, TOKEN_TILE)
    return pl.pallas_call(
        _conv_in_kernel,
        grid=(n // tm,),
        in_specs=[pl.BlockSpec((tm, d), lambda i: (i, 0)),
                  pl.BlockSpec((1, d), lambda i: (0, 0)),
                  pl.BlockSpec((d, 3 * d), lambda i: (0, 0))],
        out_specs=[pl.BlockSpec((tm, d), lambda i: (i, 0)),
                   pl.BlockSpec((tm, d), lambda i: (i, 0))],
        out_shape=[jax.ShapeDtypeStruct((n, d), BF16), jax.ShapeDtypeStruct((n, d), BF16)],
        compiler_params=_params("parallel"),
        name="conv_in",
    )(x, g, w)


def _conv_out_kernel(u_ref, up_ref, un_ref, gb_ref, h_ref, wdw_ref, w_ref, o_ref, *, tiles_per_seq):
    tm = u_ref.shape[0]
    pos = pl.program_id(0) % tiles_per_seq
    u = u_ref[...].astype(F32)
    halo = BF16_SUBLANES_V7X
    prev_row = jnp.where(pos == 0, 0.0, up_ref[halo - 1:halo, :].astype(F32))
    next_row = jnp.where(pos == tiles_per_seq - 1, 0.0, un_ref[0:1, :].astype(F32))
    row = lax.broadcasted_iota(jnp.int32, (tm, 1), 0)
    u_up = jnp.where(row == 0, prev_row, pltpu.roll(u, 1, axis=0))
    u_dn = jnp.where(row == tm - 1, next_row, pltpu.roll(u, tm - 1, axis=0))
    conv = u_up * wdw_ref[0:1, :] + u * wdw_ref[1:2, :] + u_dn * wdw_ref[2:3, :]
    g = (gb_ref[...].astype(F32) * conv).astype(BF16)
    o_ref[...] = h_ref[...] + jnp.dot(g, w_ref[...], preferred_element_type=F32)


def _conv_out(u, gb, h, w_dw, w_out, seq):
    n, d = u.shape
    tm = _tile(seq, TOKEN_TILE)
    halo = BF16_SUBLANES_V7X
    r = tm // halo
    nblk = n // halo
    return pl.pallas_call(
        functools.partial(_conv_out_kernel, tiles_per_seq=seq // tm),
        grid=(n // tm,),
        in_specs=[pl.BlockSpec((tm, d), lambda i: (i, 0)),
                  pl.BlockSpec((halo, d), lambda i: (jnp.maximum(i * r - 1, 0), 0)),
                  pl.BlockSpec((halo, d), lambda i: (jnp.minimum((i + 1) * r, nblk - 1), 0)),
                  pl.BlockSpec((tm, d), lambda i: (i, 0)),
                  pl.BlockSpec((tm, d), lambda i: (i, 0)),
                  pl.BlockSpec((CONV_WIDTH, d), lambda i: (0, 0)),
                  pl.BlockSpec((d, d), lambda i: (0, 0))],
        out_specs=pl.BlockSpec((tm, d), lambda i: (i, 0)),
        out_shape=jax.ShapeDtypeStruct((n, d), F32),
        compiler_params=_params("parallel"),
        name="conv_out",
    )(u, u, u, gb, h, w_dw, w_out)


def _rope_group(x, c, s):
    return x * c + pltpu.roll(x, LANES_V7X // 2, axis=1) * s


def _mla_in_kernel(x_ref, g_ref, win_ref, qn_ref, kvn_ref, wqb_ref, wk_ref, wvt_ref, cos_ref, sin_ref,
                   q_ref, k_ref, vt_ref):
    hp = MLA_HEAD_PAD
    tm = x_ref.shape[0]
    scale = (QK_NOPE + QK_ROPE) ** -0.5 * LOG2_E
    xn = _rms(x_ref[...], g_ref[...]).astype(BF16)
    lat = jnp.dot(xn, win_ref[...], preferred_element_type=F32)
    qn = _rms(lat[:, 0:Q_LORA], qn_ref[...]).astype(BF16)
    kvn = _rms(lat[:, Q_LORA:Q_LORA + KV_LORA], kvn_ref[...]).astype(BF16)
    c = cos_ref[...]
    s = sin_ref[...]
    k_pe = _rope_group(lat[:, Q_LORA + KV_LORA:], c, s).astype(BF16)
    extra = lax.broadcasted_iota(jnp.int32, (VT_ROWS - V_HEAD, tm), 0)
    ones_rows = jnp.where(extra == 0, 1.0, 0.0).astype(BF16)
    for h in range(MLA_HEADS):
        qh = jnp.dot(qn, wqb_ref[:, h * hp:(h + 1) * hp], preferred_element_type=F32)
        q_ref[:, h * hp:h * hp + QK_NOPE] = (qh[:, 0:QK_NOPE] * scale).astype(BF16)
        q_ref[:, h * hp + QK_NOPE:(h + 1) * hp] = (_rope_group(qh[:, QK_NOPE:], c, s) * scale).astype(BF16)
        kh = jnp.dot(kvn, wk_ref[:, h * QK_NOPE:(h + 1) * QK_NOPE], preferred_element_type=F32)
        k_ref[:, h * hp:h * hp + QK_NOPE] = kh.astype(BF16)
        k_ref[:, h * hp + QK_NOPE:(h + 1) * hp] = k_pe
        vt = lax.dot_general(wvt_ref[h * V_HEAD:(h + 1) * V_HEAD, :], kvn, (((1,), (1,)), ((), ())),
                             preferred_element_type=F32)
        vt_ref[0, h, 0:V_HEAD, :] = vt.astype(BF16)
        vt_ref[0, h, V_HEAD:VT_ROWS, :] = ones_rows


def _mla_in(x, g, w_in, q_norm, w_qb, kv_norm, w_k, w_vt, cos, sin, seq):
    n, d = x.shape
    tm = _tile(seq, min(MLA_TOKEN_TILE, max(seq // 8, LANES_V7X)))
    tps = seq // tm
    hq = MLA_HEADS * MLA_HEAD_PAD
    const = lambda i: (0, 0)
    return pl.pallas_call(
        _mla_in_kernel,
        grid=(n // tm,),
        in_specs=[pl.BlockSpec((tm, d), lambda i: (i, 0)),
                  pl.BlockSpec((1, d), const),
                  pl.BlockSpec(w_in.shape, const),
                  pl.BlockSpec((1, Q_LORA), const),
                  pl.BlockSpec((1, KV_LORA), const),
                  pl.BlockSpec(w_qb.shape, const),
                  pl.BlockSpec(w_k.shape, const),
                  pl.BlockSpec(w_vt.shape, const),
                  pl.BlockSpec((tm, LANES_V7X), lambda i: (i % tps, 0)),
                  pl.BlockSpec((tm, LANES_V7X), lambda i: (i % tps, 0))],
        out_specs=[pl.BlockSpec((tm, hq), lambda i: (i, 0)),
                   pl.BlockSpec((tm, hq), lambda i: (i, 0)),
                   pl.BlockSpec((1, MLA_HEADS, VT_ROWS, tm), lambda i: (i, 0, 0, 0))],
        out_shape=[jax.ShapeDtypeStruct((n, hq), BF16),
                   jax.ShapeDtypeStruct((n, hq), BF16),
                   jax.ShapeDtypeStruct((n // tm, MLA_HEADS, VT_ROWS, tm), BF16)],
        compiler_params=_params("parallel"),
        name="mla_in",
    )(x, g, w_in, q_norm, kv_norm, w_qb, w_k, w_vt, cos, sin)


def _attn_kernel(q_ref, k_ref, vt_ref, o_ref, s0, s1, p0, p1, acc_ref):
    tq = q_ref.shape[0]
    nk, _, _, tk = vt_ref.shape
    assert nk % 2 == 0
    q = q_ref[...]

    def scores(j, s_ref):
        start = pl.multiple_of(j * tk, tk)
        k = k_ref[pl.ds(start, tk), :]
        st = lax.dot_general(k, q, (((1,), (1,)), ((), ())), preferred_element_type=F32)
        s_ref[...] = st
        return jnp.max(st, axis=0, keepdims=True)

    def exps(s_ref, p_ref, m, tile_max):
        m_new = jnp.maximum(m, tile_max)
        p_ref[...] = jnp.exp2(s_ref[...] - m_new).astype(BF16)
        return m_new, jnp.exp2(m - m_new)

    def values(j, p_ref, alpha):
        acc_ref[...] = alpha * acc_ref[...] + jnp.dot(vt_ref[j, 0], p_ref[...], preferred_element_type=F32)

    acc_ref[...] = jnp.zeros_like(acc_ref)
    x0 = scores(0, s0)
    x1 = scores(1, s1)
    m, alpha = exps(s0, p0, jnp.full((1, tq), -jnp.inf, F32), x0)

    steps = nk - 2
    unroll = max(u for u in (2, 4, 6, 8, 10) if steps % u == 0) if steps else 2

    def group(i, carry):
        m, alpha, x1 = carry
        for t in range(0, unroll, 2):
            s = unroll * i + t + 1
            x0 = scores(s + 1, s0)
            values(s - 1, p0, alpha)
            m, alpha = exps(s1, p1, m, x1)
            x1 = scores(s + 2, s1)
            values(s, p1, alpha)
            m, alpha = exps(s0, p0, m, x0)
        return m, alpha, x1

    m, alpha, x1 = lax.fori_loop(0, steps // unroll, group, (m, alpha, x1))
    values(nk - 2, p0, alpha)
    m, alpha = exps(s1, p1, m, x1)
    values(nk - 1, p1, alpha)
    acc = acc_ref[...]
    out_t = acc[0:V_HEAD, :] / acc[V_HEAD:V_HEAD + 1, :]
    o_ref[...] = out_t.T.astype(BF16)


def _attention(q, k, vt, batch, seq):
    n = q.shape[0]
    tq = _tile(seq, ATTN_Q_TILE)
    tk = vt.shape[-1]
    nq = seq // tq
    nk = seq // tk
    hp = MLA_HEAD_PAD
    return pl.pallas_call(
        _attn_kernel,
        grid=(batch, MLA_HEADS, nq),
        in_specs=[pl.BlockSpec((tq, hp), lambda b, h, i: (b * nq + i, h)),
                  pl.BlockSpec((seq, hp), lambda b, h, i: (b, h)),
                  pl.BlockSpec((nk, 1, VT_ROWS, tk), lambda b, h, i: (b, h, 0, 0))],
        out_specs=pl.BlockSpec((tq, V_HEAD), lambda b, h, i: (b * nq + i, h)),
        out_shape=jax.ShapeDtypeStruct((n, MLA_HEADS * V_HEAD), BF16),
        scratch_shapes=[pltpu.VMEM((tk, tq), F32), pltpu.VMEM((tk, tq), F32),
                        pltpu.VMEM((tk, tq), BF16), pltpu.VMEM((tk, tq), BF16),
                        pltpu.VMEM((VT_ROWS, tq), F32)],
        compiler_params=_params("parallel", "parallel", "arbitrary"),
        name="mla_attention",
    )(q, k, vt)


def _proj_residual_kernel(a_ref, h_ref, w_ref, o_ref):
    o_ref[...] = h_ref[...] + jnp.dot(a_ref[...], w_ref[...], preferred_element_type=F32)


def _proj_residual(a, h, w):
    n, d = h.shape
    tm = _tile(n, TOKEN_TILE)
    return pl.pallas_call(
        _proj_residual_kernel,
        grid=(n // tm,),
        in_specs=[pl.BlockSpec((tm, a.shape[1]), lambda i: (i, 0)),
                  pl.BlockSpec((tm, d), lambda i: (i, 0)),
                  pl.BlockSpec(w.shape, lambda i: (0, 0))],
        out_specs=pl.BlockSpec((tm, d), lambda i: (i, 0)),
        out_shape=jax.ShapeDtypeStruct((n, d), F32),
        compiler_params=_params("parallel"),
        name="proj_residual",
    )(a, h, w)


def _ml_in_kernel(x_ref, g_ref, w_ref, b_ref, q_ref, k_ref, v_ref, o_ref, gt_ref):
    qw = ML_HEADS * ML_QK_PAD
    vw = ML_HEADS * ML_V
    xn = _rms(x_ref[...], g_ref[...]).astype(BF16)
    q = jnp.dot(xn, w_ref[:, 0:qw], preferred_element_type=F32)
    q_ref[...] = (q * (ML_QK ** -0.5)).astype(BF16)
    k_ref[...] = jnp.dot(xn, w_ref[:, qw:2 * qw], preferred_element_type=F32).astype(BF16)
    v_ref[...] = jnp.dot(xn, w_ref[:, 2 * qw:2 * qw + vw], preferred_element_type=F32).astype(BF16)
    o_ref[...] = jnp.dot(xn, w_ref[:, 2 * qw + vw:2 * qw + 2 * vw], preferred_element_type=F32).astype(BF16)
    gt_ref[...] = jnp.dot(xn, w_ref[:, 2 * qw + 2 * vw:], preferred_element_type=F32) + b_ref[...]


def _ml_in(x, g, w, b_gates):
    n, d = x.shape
    tm = _tile(n, TOKEN_TILE)
    qw = ML_HEADS * ML_QK_PAD
    vw = ML_HEADS * ML_V
    ng = 4 * ML_HEADS
    row = lambda i: (i, 0)
    const = lambda i: (0, 0)
    return pl.pallas_call(
        _ml_in_kernel,
        grid=(n // tm,),
        in_specs=[pl.BlockSpec((tm, d), row), pl.BlockSpec((1, d), const),
                  pl.BlockSpec(w.shape, const), pl.BlockSpec((1, ng), const)],
        out_specs=[pl.BlockSpec((tm, qw), row), pl.BlockSpec((tm, qw), row),
                   pl.BlockSpec((tm, vw), row), pl.BlockSpec((tm, vw), row),
                   pl.BlockSpec((tm, ng), row)],
        out_shape=[jax.ShapeDtypeStruct((n, qw), BF16), jax.ShapeDtypeStruct((n, qw), BF16),
                   jax.ShapeDtypeStruct((n, vw), BF16), jax.ShapeDtypeStruct((n, vw), BF16),
                   jax.ShapeDtypeStruct((n, ng), F32)],
        compiler_params=_params("parallel"),
        name="mlstm_in",
    )(x, g, w, b_gates)


def _split3(x):
    hi = x.astype(BF16)
    r1 = x - hi.astype(F32)
    mid = r1.astype(BF16)
    lo = (r1 - mid.astype(F32)).astype(BF16)
    return hi, mid, lo


def _mlstm_kernel(q_ref, k_ref, vt_ref, gc_ref, gr_ref, o_ref, c_scr, m_scr):
    lc = q_ref.shape[0]
    nh = ML_HEADS
    kp = ML_QK_PAD
    dv = ML_V
    fwd = pl.program_id(0) == 0

    @pl.when(pl.program_id(2) == 0)
    def _():
        c_scr[...] = jnp.zeros_like(c_scr)
        m_scr[...] = jnp.zeros_like(m_scr)

    gcol = gc_ref[0]
    grow = gr_ref[0]
    li_col = gcol[:, 0:nh]
    lf_col = jax.nn.log_sigmoid(gcol[:, nh:2 * nh])
    li_row = grow[0:nh, :]
    lf_row = jax.nn.log_sigmoid(grow[nh:2 * nh, :])

    r = lax.broadcasted_iota(jnp.int32, (lc, lc), 0)
    cidx = lax.broadcasted_iota(jnp.int32, (lc, lc), 1)
    sign = jnp.where(fwd, 1, -1)
    tri = jnp.where((r - cidx) * sign >= 0, 1.0, 0.0).astype(BF16)
    allowed_t = (cidx - r) * sign >= 0

    b_col = jnp.zeros((lc, nh), F32)
    for piece in _split3(lf_col):
        b_col = b_col + jnp.dot(tri, piece, preferred_element_type=F32)
    b_row = jnp.zeros((nh, lc), F32)
    for piece in _split3(lf_row):
        b_row = b_row + lax.dot_general(piece, tri, (((1,), (1,)), ((), ())), preferred_element_type=F32)
    a_all = jnp.sum(lf_row, axis=-1, keepdims=True)
    src_col = li_col - b_col

    sub = lax.broadcasted_iota(jnp.int32, (dv, lc), 0)
    ones_rows = jnp.where(sub == 0, 1.0, 0.0).astype(BF16)
    nt = (((1,), (1,)), ((), ()))

    for h in range(nh):
        q = q_ref[:, h * kp:(h + 1) * kp]
        k = k_ref[:, h * kp:(h + 1) * kp]
        vt_ext = jnp.concatenate([vt_ref[h * dv:(h + 1) * dv, :], ones_rows], axis=0)
        b_r = b_row[h:h + 1, :]
        li_r = li_row[h:h + 1, :]
        a = a_all[h:h + 1, :]
        m_in = m_scr[h][0:1, 0:1]
        ct_ext = c_scr[h]

        w_end = a - b_r + li_r
        g = jnp.max(w_end, axis=-1, keepdims=True)
        e_r = jnp.exp(w_end - g)

        dmat = jnp.where(allowed_t, b_r + src_col[:, h:h + 1], -jnp.inf)
        inter_log = b_r + m_in
        m_j = jnp.maximum(inter_log, jnp.max(dmat, axis=0, keepdims=True))
        inter = jnp.exp(inter_log - m_j)
        p = jnp.exp(dmat - m_j)
        s = lax.dot_general(k, q, nt, preferred_element_type=F32)
        qk = (s * p).astype(BF16)
        nd = (jnp.dot(vt_ext, qk, preferred_element_type=F32)
              + inter * lax.dot_general(ct_ext.astype(BF16), q, nt, preferred_element_type=F32))
        den = nd[dv:dv + 1, :]
        out_t = nd[0:dv, :] / jnp.maximum(jnp.abs(den), jnp.exp(-m_j))
        o_ref[0, :, h * dv:(h + 1) * dv] = out_t.T

        m_new = jnp.maximum(a + m_in, g)
        fdec = jnp.exp(a + m_in - m_new)
        iin = jnp.exp(g - m_new)
        vte = (vt_ext.astype(F32) * e_r).astype(BF16)
        c_scr[h] = fdec * ct_ext + iin * jnp.dot(vte, k, preferred_element_type=F32)
        m_scr[h] = jnp.broadcast_to(m_new, m_scr.shape[1:])


def _mlstm(q, k, vt, gcol, grow, batch, seq):
    n = q.shape[0]
    lc = _tile(seq, ML_CHUNK_TILE)
    nc = seq // lc
    qw = ML_HEADS * ML_QK_PAD
    vw = ML_HEADS * ML_V

    def chunk(d, b, c):
        return b * nc + c + d * (nc - 1 - 2 * c)

    return pl.pallas_call(
        _mlstm_kernel,
        grid=(2, batch, nc),
        in_specs=[pl.BlockSpec((lc, qw), lambda d, b, c: (chunk(d, b, c), 0)),
                  pl.BlockSpec((lc, qw), lambda d, b, c: (chunk(d, b, c), 0)),
                  pl.BlockSpec((vw, lc), lambda d, b, c: (0, chunk(d, b, c))),
                  pl.BlockSpec((1, lc, 2 * ML_HEADS), lambda d, b, c: (d, chunk(d, b, c), 0)),
                  pl.BlockSpec((1, 2 * ML_HEADS, lc), lambda d, b, c: (d, 0, chunk(d, b, c)))],
        out_specs=pl.BlockSpec((1, lc, vw), lambda d, b, c: (d, chunk(d, b, c), 0)),
        out_shape=jax.ShapeDtypeStruct((2, n, vw), F32),
        scratch_shapes=[pltpu.VMEM((ML_HEADS, 2 * ML_V, ML_QK_PAD), F32),
                        pltpu.VMEM((ML_HEADS, 8, LANES_V7X), F32)],
        compiler_params=_params("parallel", "parallel", "arbitrary"),
        name="mlstm_scan",
    )(q, k, vt, gcol, grow)


def _ml_out_kernel(hs_ref, og_ref, hn_ref, h_ref, w_ref, o_ref):
    dv = ML_V
    hs = hs_ref[0] + hs_ref[1]
    parts = []
    for hd in range(ML_HEADS):
        x = hs[:, hd * dv:(hd + 1) * dv]
        parts.append(x * lax.rsqrt(jnp.mean(x * x, axis=-1, keepdims=True) + EPS))
    y = jnp.concatenate(parts, axis=1) * hn_ref[...] * jax.nn.sigmoid(og_ref[...].astype(F32))
    o_ref[...] = h_ref[...] + jnp.dot(y.astype(BF16), w_ref[...], preferred_element_type=F32)


def _ml_out(hs, og, head_norm, h, w):
    n, d = h.shape
    vw = ML_HEADS * ML_V
    tm = _tile(n, TOKEN_TILE)
    row = lambda i: (i, 0)
    const = lambda i: (0, 0)
    return pl.pallas_call(
        _ml_out_kernel,
        grid=(n // tm,),
        in_specs=[pl.BlockSpec((2, tm, vw), lambda i: (0, i, 0)),
                  pl.BlockSpec((tm, vw), row), pl.BlockSpec((1, vw), const),
                  pl.BlockSpec((tm, d), row), pl.BlockSpec(w.shape, const)],
        out_specs=pl.BlockSpec((tm, d), row),
        out_shape=jax.ShapeDtypeStruct((n, d), F32),
        compiler_params=_params("parallel"),
        name="mlstm_out",
    )(hs, og, head_norm, h, w)


def _router_kernel(x_ref, g_ref, w_ref, wlo_ref, xn_ref, aff_ref, *, nreal):
    @pl.when(pl.program_id(0) < nreal)
    def _():
        xf = _rms(x_ref[...], g_ref[...])
        xn = xf.astype(BF16)
        xn_ref[...] = xn
        xlo = (xf - xn.astype(F32)).astype(BF16)
        logits = (jnp.dot(xn, w_ref[...], preferred_element_type=F32)
                  + jnp.dot(xlo, w_ref[...], preferred_element_type=F32)
                  + jnp.dot(xn, wlo_ref[...], preferred_element_type=F32))
        z = jnp.exp(logits - jnp.max(logits, axis=-1, keepdims=True))
        aff_ref[...] = z / jnp.sum(z, axis=-1, keepdims=True)

    @pl.when(pl.program_id(0) >= nreal)
    def _():
        xn_ref[...] = jnp.zeros_like(xn_ref)


def _router(x, g, w_f32):
    n, d = x.shape
    tm = _tile(n, TOKEN_TILE)
    nreal = n // tm
    npad = max(n, GATHER_OPERAND_ROWS)
    row = lambda i: (jnp.minimum(i, nreal - 1), 0)
    const = lambda i: (0, 0)
    w = w_f32.astype(BF16)
    w_lo = (w_f32 - w.astype(F32)).astype(BF16)
    return pl.pallas_call(
        functools.partial(_router_kernel, nreal=nreal),
        grid=(npad // tm,),
        in_specs=[pl.BlockSpec((tm, d), row), pl.BlockSpec((1, d), const),
                  pl.BlockSpec(w.shape, const), pl.BlockSpec(w.shape, const)],
        out_specs=[pl.BlockSpec((tm, d), lambda i: (i, 0)), pl.BlockSpec((tm, N_EXPERTS), row)],
        out_shape=[jax.ShapeDtypeStruct((npad, d), BF16), jax.ShapeDtypeStruct((n, N_EXPERTS), F32)],
        compiler_params=_params("arbitrary"),
        name="moe_router",
    )(x, g, w, w_lo)


def _ffn_kernel(*refs, nchunk, epc):
    x_refs = refs[:nchunk]
    gate_ref, wg_ref, wu_ref, wd_ref, o_ref = refs[nchunk:]

    def compute(x):
        g = jnp.dot(x, wg_ref[0, 0].astype(BF16), preferred_element_type=F32)
        u = jnp.dot(x, wu_ref[0, 0].astype(BF16), preferred_element_type=F32)
        hid = (g * jax.nn.sigmoid(g) * u).astype(BF16)
        y = jnp.dot(hid, wd_ref[0, 0].astype(BF16), preferred_element_type=F32)
        o_ref[0] = (y * gate_col()).astype(BF16)

    def gate_col():
        tm = o_ref.shape[1]
        groups = tm // LANES_V7X
        g = gate_ref[0, pl.ds(pl.program_id(1) * groups, groups), :]
        eye = (lax.broadcasted_iota(jnp.int32, (LANES_V7X, LANES_V7X), 0)
               == lax.broadcasted_iota(jnp.int32, (LANES_V7X, LANES_V7X), 1))
        return jnp.concatenate([jnp.sum(jnp.where(eye, g[k:k + 1, :], 0.0), axis=1, keepdims=True)
                                for k in range(groups)], axis=0)

    if nchunk == 1:
        compute(x_refs[0][0])
    else:
        chunk = pl.program_id(0) // epc
        for c in range(nchunk):
            @pl.when(chunk == c)
            def _(c=c):
                compute(x_refs[c][0])


def _expert_ffn(xes, gate, wg, wu, wd, layer):
    nchunk = len(xes)
    epc, cap, d = xes[0].shape
    e = nchunk * epc
    f = wg.shape[-1]
    tm = _tile(cap, TOKEN_TILE)
    nj = cap // tm
    tok = lambda i, j: (i, j, 0)
    wmap = lambda i, j: (layer, i, 0, 0)

    def chunk_map(c):
        def index(i, j):
            jj = jnp.where(i < c * epc, 0, jnp.where(i >= (c + 1) * epc, nj - 1, j))
            return (jnp.clip(i - c * epc, 0, epc - 1), jj, 0)
        return index

    return pl.pallas_call(
        functools.partial(_ffn_kernel, nchunk=nchunk, epc=epc),
        grid=(e, nj),
        in_specs=[pl.BlockSpec((1, tm, d), chunk_map(c)) for c in range(nchunk)]
        + [pl.BlockSpec((1, cap // LANES_V7X, LANES_V7X), lambda i, j: (i, 0, 0)),
           pl.BlockSpec((1, 1, d, f), wmap), pl.BlockSpec((1, 1, d, f), wmap),
           pl.BlockSpec((1, 1, f, d), wmap)],
        out_specs=pl.BlockSpec((1, tm, d), tok),
        out_shape=jax.ShapeDtypeStruct((e, cap, d), BF16),
        compiler_params=_params("parallel", "arbitrary"),
        name="moe_ffn",
    )(*xes, gate, wg, wu, wd)


def _threshold_kernel(a_ref, t_ref, need_ref, *, cap):
    bits = pltpu.bitcast(a_ref[...], jnp.int32)
    ne = bits.shape[0]

    def count(mask):
        return jnp.sum(jnp.where(mask, 1.0, 0.0), axis=1, keepdims=True)

    t = jnp.zeros((ne, 1), jnp.int32)
    for k in range(30, -1, -1):
        cand = t | (1 << k)
        t = jnp.where(count(bits >= cand) >= cap, cand, t)
    need = cap - count(bits > t)
    t_ref[...] = jnp.broadcast_to(t, t_ref.shape)
    need_ref[...] = jnp.broadcast_to(need, need_ref.shape)


def _threshold(aff_t, cap):
    e, n = aff_t.shape
    return pl.pallas_call(
        functools.partial(_threshold_kernel, cap=cap),
        grid=(1,),
        in_specs=[pl.BlockSpec((e, n), lambda i: (0, 0))],
        out_specs=[pl.BlockSpec((e, LANES_V7X), lambda i: (0, 0)), pl.BlockSpec((e, LANES_V7X), lambda i: (0, 0))],
        out_shape=[jax.ShapeDtypeStruct((e, LANES_V7X), jnp.int32), jax.ShapeDtypeStruct((e, LANES_V7X), F32)],
        compiler_params=_params("arbitrary"),
        name="moe_threshold",
    )(aff_t)


def _positions_kernel(aff_ref, thr_ref, need_ref, pos_ref, gate_ref, cnt_ref, rc, tc):
    tb = aff_ref.shape[0]

    @pl.when(pl.program_id(0) == 0)
    def _():
        rc[...] = jnp.zeros_like(rc)
        tc[...] = jnp.zeros_like(tc)

    a = aff_ref[...]
    bits = pltpu.bitcast(a, jnp.int32)
    thr = thr_ref[...]
    r = lax.broadcasted_iota(jnp.int32, (tb, tb), 0)
    c = lax.broadcasted_iota(jnp.int32, (tb, tb), 1)
    tril = jnp.where(r >= c, 1.0, 0.0).astype(BF16)
    eq = jnp.where(bits == thr, 1.0, 0.0)
    eq_incl = jnp.dot(tril, eq.astype(BF16), preferred_element_type=F32)
    tie_rank = tc[0:1, :] + eq_incl - eq
    sel = jnp.where(bits > thr, 1.0, jnp.where(tie_rank < need_ref[...], eq, 0.0))
    sel_incl = jnp.dot(tril, sel.astype(BF16), preferred_element_type=F32)
    taken = sel > 0.5
    pos = (rc[0:1, :] + sel_incl - 1.0).astype(jnp.int32)
    pos_ref[...] = jnp.where(taken, pos, -1)
    gate_ref[...] = jnp.where(taken, a, 0.0)
    cnt_ref[0] = jnp.broadcast_to(rc[0:1, :], cnt_ref.shape[1:]).astype(jnp.int32)
    rc[...] = rc[...] + jnp.sum(sel, axis=0, keepdims=True)
    tc[...] = tc[...] + jnp.sum(eq, axis=0, keepdims=True)


def _positions(aff, thr_row, need_row):
    n, e = aff.shape
    tb = _tile(n, TOKEN_TILE)
    nb = n // tb
    row = lambda i: (i, 0)
    const = lambda i: (0, 0)
    return pl.pallas_call(
        _positions_kernel,
        grid=(nb,),
        in_specs=[pl.BlockSpec((tb, e), row), pl.BlockSpec((1, e), const), pl.BlockSpec((1, e), const)],
        out_specs=[pl.BlockSpec((tb, e), row), pl.BlockSpec((tb, e), row),
                   pl.BlockSpec((1, 8, e), lambda i: (i, 0, 0))],
        out_shape=[jax.ShapeDtypeStruct((n, e), jnp.int32), jax.ShapeDtypeStruct((n, e), F32),
                   jax.ShapeDtypeStruct((nb, 8, e), jnp.int32)],
        scratch_shapes=[pltpu.VMEM((8, e), F32), pltpu.VMEM((8, e), F32)],
        compiler_params=_params("arbitrary"),
        name="moe_positions",
    )(aff, thr_row, need_row)


def _compact_kernel(lo_ref, pt_ref, gt_ref, idx_ref, gate_ref):
    ne, tb = pt_ref.shape
    _, nrow, lanes = idx_ref.shape
    cap = nrow * lanes
    win = COMBINE_WINDOW
    align = BF16_SUBLANES_V7X
    b = pl.program_id(0)

    @pl.when(b == 0)
    def _():
        idx_ref[...] = jnp.zeros_like(idx_ref)
        gate_ref[...] = jnp.zeros_like(gate_ref)

    tok = b * tb + lax.broadcasted_iota(jnp.int32, (1, tb), 1)
    digits = [(tok >> 8).astype(F32).astype(BF16), (tok & 255).astype(F32).astype(BF16)]
    pad = jnp.zeros((3, tb), BF16)
    slot = lax.broadcasted_iota(jnp.int32, (win, tb), 0)
    zeros = jnp.zeros((1, lanes), F32)
    nt = (((1,), (1,)), ((), ()))

    def window(e, w, start0):
        nominal = start0 + w * win
        start = jnp.minimum(nominal, cap - win)
        rel = pt_ref[e:e + 1, :] - start
        rel = jnp.where(rel >= nominal - start, rel, -1)
        sel_t = jnp.where(slot == rel, 1.0, 0.0).astype(BF16)
        lhs = jnp.concatenate(digits + list(_split3(gt_ref[e:e + 1, :])) + [pad], axis=0)
        got = lax.dot_general(lhs, sel_t, nt, preferred_element_type=F32)
        ids = got[0:1, :] * 256.0 + got[1:2, :]
        gts = got[2:3, :] + got[3:4, :] + got[4:5, :]
        off = start % lanes
        r0 = start // lanes
        r1 = jnp.minimum(r0 + 1, nrow - 1)
        ids2 = pltpu.roll(jnp.concatenate([ids, zeros], axis=1), off, axis=1)
        gts2 = pltpu.roll(jnp.concatenate([gts, zeros], axis=1), off, axis=1)
        idx_ref[e, pl.ds(r0, 1), :] += ids2[:, 0:lanes].astype(jnp.int32)
        idx_ref[e, pl.ds(r1, 1), :] += ids2[:, lanes:].astype(jnp.int32)
        gate_ref[e, pl.ds(r0, 1), :] += gts2[:, 0:lanes]
        gate_ref[e, pl.ds(r1, 1), :] += gts2[:, lanes:]

    starts = [lo_ref[b, e] // align * align for e in range(ne)]
    for e in range(ne):
        window(e, 0, starts[e])
    for e in range(ne):
        nwin = (lo_ref[b + 1, e] - starts[e] + win - 1) // win
        lax.fori_loop(1, nwin, lambda w, c, e=e: (window(e, w, starts[e]), c)[1], 0)


def _compact(lo, pos_t, gate_t, cap):
    e, n = pos_t.shape
    assert COMBINE_WINDOW == LANES_V7X
    tb = _tile(n, TOKEN_TILE)
    shape3 = (e, cap // LANES_V7X, LANES_V7X)
    grid_spec = pltpu.PrefetchScalarGridSpec(
        num_scalar_prefetch=1,
        grid=(n // tb,),
        in_specs=[pl.BlockSpec((e, tb), lambda i, lo_ref: (0, i)),
                  pl.BlockSpec((e, tb), lambda i, lo_ref: (0, i))],
        out_specs=[pl.BlockSpec(shape3, lambda i, lo_ref: (0, 0, 0)),
                   pl.BlockSpec(shape3, lambda i, lo_ref: (0, 0, 0))])
    return pl.pallas_call(
        _compact_kernel,
        grid_spec=grid_spec,
        out_shape=[jax.ShapeDtypeStruct(shape3, jnp.int32), jax.ShapeDtypeStruct(shape3, F32)],
        compiler_params=_params("arbitrary"),
        name="moe_compact",
    )(lo, pos_t, gate_t)


def _combine_kernel(lo_ref, h_ref, idx_ref, ye_hbm, *rest, final):
    (gf_ref, o_ref, buf, sem) = rest if final else (None,) + rest
    tb, d = h_ref.shape
    ne, nrow, lanes = idx_ref.shape
    cap = nrow * lanes
    win = COMBINE_WINDOW
    align = BF16_SUBLANES_V7X
    b = pl.program_id(0)
    nb = pl.num_programs(0)
    tok = b * tb + lax.broadcasted_iota(jnp.int32, (tb, 1), 0)
    lane = lax.broadcasted_iota(jnp.int32, (1, win), 1)

    def clamp(start):
        return jnp.minimum(start, cap - win)

    def first_start(blk, e):
        return clamp(lo_ref[e, blk] // align * align)

    def fetch(e, start):
        return pltpu.make_async_copy(ye_hbm.at[e, pl.ds(pl.multiple_of(start, align), win), :],
                                     buf.at[pl.ds(e * win, win)], sem.at[e])

    def select(e, start, row_lo, row_hi):
        r0 = start // lanes
        two = jnp.concatenate([idx_ref[e, pl.ds(r0, 1), :],
                               idx_ref[e, pl.ds(jnp.minimum(r0 + 1, nrow - 1), 1), :]], axis=1)
        ids = pltpu.roll(two, (2 * lanes - start % lanes) % (2 * lanes), axis=1)[:, 0:win]
        rid = start + lane
        ids = jnp.where((rid >= row_lo) & (rid < row_hi), ids, -1)
        return jnp.where(tok == ids, 1.0, 0.0).astype(BF16)

    @pl.when(b == 0)
    def _():
        for e in range(ne):
            fetch(e, first_start(0, e)).start()

    o_ref[...] = h_ref[...]
    group = 4
    total = None
    for e0 in range(0, ne, group):
        first = []
        for e in range(e0, e0 + group):
            lo = lo_ref[e, b]
            hi = lo_ref[e, b + 1]
            start0 = lo // align * align
            first.append((e, lo, hi, start0, clamp(start0)))
        for e, lo, hi, start0, start in first:
            fetch(e, start).wait()
        sel = jnp.concatenate([select(e, start, lo, hi) for e, lo, hi, start0, start in first], axis=1)
        part = jnp.dot(sel, buf[e0 * win:(e0 + group) * win, :], preferred_element_type=F32)
        total = part if total is None else total + part

        for e, lo, hi, start0, start in first:
            def more(w, carry, e=e, lo=lo, hi=hi, start0=start0):
                nominal = start0 + w * win
                st = clamp(nominal)
                cp = fetch(e, st)
                cp.start()
                cp.wait()
                o_ref[...] += jnp.dot(select(e, st, jnp.maximum(lo, nominal), hi), buf[e * win:(e + 1) * win, :],
                                      preferred_element_type=F32)
                return carry

            nwin = (hi - start0 + win - 1) // win
            lax.fori_loop(1, nwin, more, 0)

            @pl.when(b + 1 < nb)
            def _(e=e):
                fetch(e, first_start(b + 1, e)).start()

    if final:
        o_ref[...] = _rms(o_ref[...] + total, gf_ref[...])
    else:
        o_ref[...] += total


def _combine(h, lo, idx3, ye, g_final=None):
    n, d = h.shape
    e, cap, _ = ye.shape
    assert cap >= COMBINE_WINDOW and cap % LANES_V7X == 0 and e % 4 == 0 and COMBINE_WINDOW <= LANES_V7X
    tb = _tile(n, TOKEN_TILE)
    nb = n // tb
    final = g_final is not None
    grid_spec = pltpu.PrefetchScalarGridSpec(
        num_scalar_prefetch=1,
        grid=(nb,),
        in_specs=[pl.BlockSpec((tb, d), lambda i, lo_ref: (i, 0)),
                  pl.BlockSpec(idx3.shape, lambda i, lo_ref: (0, 0, 0)),
                  pl.BlockSpec(memory_space=pl.ANY)]
        + ([pl.BlockSpec((1, d), lambda i, lo_ref: (0, 0))] if final else []),
        out_specs=pl.BlockSpec((tb, d), lambda i, lo_ref: (i, 0)),
        scratch_shapes=[pltpu.VMEM((e * COMBINE_WINDOW, d), BF16), pltpu.SemaphoreType.DMA((e,))])
    return pl.pallas_call(
        functools.partial(_combine_kernel, final=final),
        grid_spec=grid_spec,
        out_shape=jax.ShapeDtypeStruct((n, d), F32),
        compiler_params=_params("arbitrary"),
        name="moe_combine",
    )(lo, h, idx3, ye, *([g_final] if final else []))


def _moe(h, g, w_router, wg, wu, wd, layer, g_final=None):
    n, d = h.shape
    cap = EC_CAPACITY * n // N_EXPERTS
    xn, aff = _router(h, g, w_router)
    thr, need = _threshold(aff.T, cap)
    pos, gate, cnt = _positions(aff, thr[:, 0][None, :], need[:, 0][None, :])
    lo = jnp.concatenate([cnt[:, 0, :], jnp.full((1, N_EXPERTS), cap, jnp.int32)], axis=0)
    idx3, gate3 = _compact(lo, pos.T, gate.T, cap)
    idx = idx3.reshape(N_EXPERTS, cap)
    nchunk = max(1, idx.size // GATHER_ROWS)
    epc = N_EXPERTS // nchunk
    xes = [xn[idx[c * epc:(c + 1) * epc]] for c in range(nchunk)]
    ye = _expert_ffn(xes, gate3, wg, wu, wd, layer)
    return _combine(h, lo.T, idx3, ye, g_final)


def _rope_group_cols(w_pe):
    half = QK_ROPE // 2
    z = jnp.zeros((w_pe.shape[0], LANES_V7X // 2 - half), w_pe.dtype)
    return jnp.concatenate([w_pe[:, :half], z, w_pe[:, half:], z], axis=1)


def _rope_tables(seq):
    half = QK_ROPE // 2
    pos = jnp.arange(seq, dtype=F32)
    inv = ROPE_THETA ** (-jnp.arange(0, QK_ROPE, 2, dtype=F32) / QK_ROPE)
    ang = pos[:, None] * inv[None, :]
    c, s = jnp.cos(ang), jnp.sin(ang)
    z = jnp.zeros((seq, LANES_V7X // 2 - half), F32)
    return jnp.concatenate([c, z, c, z], axis=1), jnp.concatenate([-s, z, s, z], axis=1)


def _prep_mla(w_in, w_qb, w_kvb):
    lat = Q_LORA + KV_LORA
    w_in_p = jnp.concatenate([w_in[:, :lat], _rope_group_cols(w_in[:, lat:])], axis=1).astype(BF16)
    hd = QK_NOPE + QK_ROPE
    cols = []
    for h in range(MLA_HEADS):
        cols.append(w_qb[:, h * hd:h * hd + QK_NOPE])
        cols.append(_rope_group_cols(w_qb[:, h * hd + QK_NOPE:(h + 1) * hd]))
    w_kv = w_kvb.reshape(KV_LORA, MLA_HEADS, QK_NOPE + V_HEAD)
    w_k = w_kv[:, :, :QK_NOPE].reshape(KV_LORA, MLA_HEADS * QK_NOPE).astype(BF16)
    w_vt = w_kv[:, :, QK_NOPE:].reshape(KV_LORA, MLA_HEADS * V_HEAD).T.astype(BF16)
    return w_in_p, jnp.concatenate(cols, axis=1).astype(BF16), w_k, w_vt


def _prep_ml_in(w):
    qd = ML_HEADS * ML_QK
    cols = []
    for base in (0, qd):
        for h in range(ML_HEADS):
            cols.append(w[:, base + h * ML_QK:base + (h + 1) * ML_QK])
            cols.append(jnp.zeros((w.shape[0], ML_QK_PAD - ML_QK), w.dtype))
    cols.append(w[:, 2 * qd:])
    return jnp.concatenate(cols, axis=1).astype(BF16)


def _trunk(x, p):
    batch, seq, d = x.shape
    n = batch * seq
    h = x.reshape(n, d)
    for i in range(DEPTH):
        j = i // N_MIXERS
        kind = i % N_MIXERS
        g_mix = p["norm_mix"][i][None, :]
        if kind == 0:
            u, gb = _conv_in(h, g_mix, p["conv_w_in"][j])
            h = _conv_out(u, gb, h, p["conv_w_dw"][j], p["conv_w_out"][j], seq)
        elif kind == 1:
            cos, sin = _rope_tables(seq)
            q, k, vt = _mla_in(h, g_mix, p["mla_w_in"][j], p["mla_q_norm"][j][None, :], p["mla_w_qb"][j],
                               p["mla_kv_norm"][j][None, :], p["mla_w_k"][j], p["mla_w_vt"][j], cos, sin, seq)
            o = _attention(q, k, vt, batch, seq)
            h = _proj_residual(o, h, p["mla_w_out"][j])
        else:
            q, k, v, og, gates = _ml_in(h, g_mix, p["ml_w_in"][j], p["ml_b_gates"][j][None, :])
            gcol = gates.reshape(n, 2, 2 * ML_HEADS).transpose(1, 0, 2)
            grow = gcol.transpose(0, 2, 1)
            hs = _mlstm(q, k, v.T, gcol, grow, batch, seq)
            h = _ml_out(hs, og, p["ml_head_norm"][j][None, :], h, p["ml_w_out"][j])
        g_final = p["norm_final"][None, :] if i == DEPTH - 1 else None
        h = _moe(h, p["norm_ffn"][i][None, :], p["router_w"][i],
                 p["exp_w_gate"], p["exp_w_up"], p["exp_w_down"], i, g_final)
    return h.reshape(batch, seq, d)


def kernel(x_prompt, x_sample, conv_w_in, conv_w_dw, conv_w_out, mla_w_in, mla_q_norm, mla_w_qb, mla_kv_norm, mla_w_kvb, mla_w_out, ml_w_in, ml_b_gates, ml_head_norm, ml_w_out, norm_mix, norm_ffn, router_w, exp_w_gate, exp_w_up, exp_w_down, norm_final):
    mla = [_prep_mla(mla_w_in[j], mla_w_qb[j], mla_w_kvb[j]) for j in range(mla_w_in.shape[0])]
    p = dict(
        conv_w_in=conv_w_in.astype(BF16), conv_w_dw=conv_w_dw, conv_w_out=conv_w_out.astype(BF16),
        mla_w_in=[m[0] for m in mla], mla_q_norm=mla_q_norm, mla_w_qb=[m[1] for m in mla],
        mla_kv_norm=mla_kv_norm, mla_w_k=[m[2] for m in mla], mla_w_vt=[m[3] for m in mla],
        mla_w_out=mla_w_out.astype(BF16),
        ml_w_in=[_prep_ml_in(ml_w_in[j]) for j in range(ml_w_in.shape[0])], ml_b_gates=ml_b_gates,
        ml_head_norm=ml_head_norm, ml_w_out=ml_w_out.astype(BF16),
        norm_mix=norm_mix, norm_ffn=norm_ffn, router_w=router_w,
        exp_w_gate=exp_w_gate, exp_w_up=exp_w_up, exp_w_down=exp_w_down, norm_final=norm_final)
    return (_trunk(x_prompt, p), _trunk(x_sample, p))
```

```python
import functools

import jax
import jax.numpy as jnp
from jax import lax
from jax.experimental import pallas as pl
from jax.experimental.pallas import tpu as pltpu

F32 = jnp.float32
BF16 = jnp.bfloat16

D_MODEL = 1024
DEPTH = 4
N_MIXERS = 3
EPS = 1e-6
CONV_WIDTH = 3
MLA_HEADS = 8
Q_LORA = 384
KV_LORA = 256
QK_NOPE = 128
QK_ROPE = 64
V_HEAD = 128
ROPE_THETA = 10000.0
ML_HEADS = 8
ML_QK = 64
ML_V = 128
N_EXPERTS = 16
EC_CAPACITY = 2
D_EXPERT = 1024

LANES_V7X = 128
F32_SUBLANES_V7X = 8
BF16_SUBLANES_V7X = 16
VMEM_BYTES_V7X = 64 * 1024 * 1024
VMEM_LIMIT_BYTES = VMEM_BYTES_V7X - 8 * 1024 * 1024

TOKEN_TILE = 512
MLA_TOKEN_TILE = 1024
ATTN_Q_TILE = 1024
ML_CHUNK_TILE = 512
COMBINE_WINDOW = 128
GATHER_ROWS = 32768
GATHER_OPERAND_ROWS = 32768

MLA_HEAD_PAD = 2 * LANES_V7X
VT_ROWS = V_HEAD + BF16_SUBLANES_V7X
LOG2_E = 1.4426950408889634
ML_QK_PAD = LANES_V7X


def _params(*sem):
    return pltpu.CompilerParams(dimension_semantics=sem, vmem_limit_bytes=VMEM_LIMIT_BYTES)


def _rms(x, g):
    ms = jnp.mean(x * x, axis=-1, keepdims=True)
    return x * lax.rsqrt(ms + EPS) * g


def _tile(n, pref):
    t = min(n, pref)
    assert n % t == 0, (n, t)
    return t


def _conv_in_kernel(x_ref, g_ref, w_ref, u_ref, gb_ref):
    d = D_MODEL
    xn = _rms(x_ref[...], g_ref[...]).astype(BF16)
    gb = jnp.dot(xn, w_ref[:, 0:d], preferred_element_type=F32)
    gc = jnp.dot(xn, w_ref[:, d:2 * d], preferred_element_type=F32)
    xv = jnp.dot(xn, w_ref[:, 2 * d:3 * d], preferred_element_type=F32)
    gb_ref[...] = gb.astype(BF16)
    u_ref[...] = (gc * xv).astype(BF16)


def _conv_in(x, g, w):
    n, d = x.shape
    tm = _tile(n, TOKEN_TILE)
    return pl.pallas_call(
        _conv_in_kernel,
        grid=(n // tm,),
        in_specs=[pl.BlockSpec((tm, d), lambda i: (i, 0)),
                  pl.BlockSpec((1, d), lambda i: (0, 0)),
                  pl.BlockSpec((d, 3 * d), lambda i: (0, 0))],
        out_specs=[pl.BlockSpec((tm, d), lambda i: (i, 0)),
                   pl.BlockSpec((tm, d), lambda i: (i, 0))],
        out_shape=[jax.ShapeDtypeStruct((n, d), BF16), jax.ShapeDtypeStruct((n, d), BF16)],
        compiler_params=_params("parallel"),
        name="conv_in",
    )(x, g, w)


def _conv_out_kernel(u_ref, up_ref, un_ref, gb_ref, h_ref, wdw_ref, w_ref, o_ref, *, tiles_per_seq):
    tm = u_ref.shape[0]
    pos = pl.program_id(0) % tiles_per_seq
    u = u_ref[...].astype(F32)
    halo = BF16_SUBLANES_V7X
    prev_row = jnp.where(pos == 0, 0.0, up_ref[halo - 1:halo, :].astype(F32))
    next_row = jnp.where(pos == tiles_per_seq - 1, 0.0, un_ref[0:1, :].astype(F32))
    row = lax.broadcasted_iota(jnp.int32, (tm, 1), 0)
    u_up = jnp.where(row == 0, prev_row, pltpu.roll(u, 1, axis=0))
    u_dn = jnp.where(row == tm - 1, next_row, pltpu.roll(u, tm - 1, axis=0))
    conv = u_up * wdw_ref[0:1, :] + u * wdw_ref[1:2, :] + u_dn * wdw_ref[2:3, :]
    g = (gb_ref[...].astype(F32) * conv).astype(BF16)
    o_ref[...] = h_ref[...] + jnp.dot(g, w_ref[...], preferred_element_type=F32)


def _conv_out(u, gb, h, w_dw, w_out, seq):
    n, d = u.shape
    tm = _tile(seq, TOKEN_TILE)
    halo = BF16_SUBLANES_V7X
    r = tm // halo
    nblk = n // halo
    return pl.pallas_call(
        functools.partial(_conv_out_kernel, tiles_per_seq=seq // tm),
        grid=(n // tm,),
        in_specs=[pl.BlockSpec((tm, d), lambda i: (i, 0)),
                  pl.BlockSpec((halo, d), lambda i: (jnp.maximum(i * r - 1, 0), 0)),
                  pl.BlockSpec((halo, d), lambda i: (jnp.minimum((i + 1) * r, nblk - 1), 0)),
                  pl.BlockSpec((tm, d), lambda i: (i, 0)),
                  pl.BlockSpec((tm, d), lambda i: (i, 0)),
                  pl.BlockSpec((CONV_WIDTH, d), lambda i: (0, 0)),
                  pl.BlockSpec((d, d), lambda i: (0, 0))],
        out_specs=pl.BlockSpec((tm, d), lambda i: (i, 0)),
        out_shape=jax.ShapeDtypeStruct((n, d), F32),
        compiler_params=_params("parallel"),
        name="conv_out",
    )(u, u, u, gb, h, w_dw, w_out)


def _rope_group(x, c, s):
    return x * c + pltpu.roll(x, LANES_V7X // 2, axis=1) * s


def _mla_in_kernel(x_ref, g_ref, win_ref, qn_ref, kvn_ref, wqb_ref, wk_ref, wvt_ref, cos_ref, sin_ref,
                   q_ref, k_ref, vt_ref):
    hp = MLA_HEAD_PAD
    tm = x_ref.shape[0]
    scale = (QK_NOPE + QK_ROPE) ** -0.5 * LOG2_E
    xn = _rms(x_ref[...], g_ref[...]).astype(BF16)
    lat = jnp.dot(xn, win_ref[...], preferred_element_type=F32)
    qn = _rms(lat[:, 0:Q_LORA], qn_ref[...]).astype(BF16)
    kvn = _rms(lat[:, Q_LORA:Q_LORA + KV_LORA], kvn_ref[...]).astype(BF16)
    c = cos_ref[...]
    s = sin_ref[...]
    k_pe = _rope_group(lat[:, Q_LORA + KV_LORA:], c, s).astype(BF16)
    extra = lax.broadcasted_iota(jnp.int32, (VT_ROWS - V_HEAD, tm), 0)
    ones_rows = jnp.where(extra == 0, 1.0, 0.0).astype(BF16)
    for h in range(MLA_HEADS):
        qh = jnp.dot(qn, wqb_ref[:, h * hp:(h + 1) * hp], preferred_element_type=F32)
        q_ref[:, h * hp:h * hp + QK_NOPE] = (qh[:, 0:QK_NOPE] * scale).astype(BF16)
        q_ref[:, h * hp + QK_NOPE:(h + 1) * hp] = (_rope_group(qh[:, QK_NOPE:], c, s) * scale).astype(BF16)
        kh = jnp.dot(kvn, wk_ref[:, h * QK_NOPE:(h + 1) * QK_NOPE], preferred_element_type=F32)
        k_ref[:, h * hp:h * hp + QK_NOPE] = kh.astype(BF16)
        k_ref[:, h * hp + QK_NOPE:(h + 1) * hp] = k_pe
        vt = lax.dot_general(wvt_ref[h * V_HEAD:(h + 1) * V_HEAD, :], kvn, (((1,), (1,)), ((), ())),
                             preferred_element_type=F32)
        vt_ref[0, h, 0:V_HEAD, :] = vt.astype(BF16)
        vt_ref[0, h, V_HEAD:VT_ROWS, :] = ones_rows


def _mla_in(x, g, w_in, q_norm, w_qb, kv_norm, w_k, w_vt, cos, sin, seq):
    n, d = x.shape
    tm = _tile(seq, min(MLA_TOKEN_TILE, max(seq // 8, LANES_V7X)))
    tps = seq // tm
    hq = MLA_HEADS * MLA_HEAD_PAD
    const = lambda i: (0, 0)
    return pl.pallas_call(
        _mla_in_kernel,
        grid=(n // tm,),
        in_specs=[pl.BlockSpec((tm, d), lambda i: (i, 0)),
                  pl.BlockSpec((1, d), const),
                  pl.BlockSpec(w_in.shape, const),
                  pl.BlockSpec((1, Q_LORA), const),
                  pl.BlockSpec((1, KV_LORA), const),
                  pl.BlockSpec(w_qb.shape, const),
                  pl.BlockSpec(w_k.shape, const),
                  pl.BlockSpec(w_vt.shape, const),
                  pl.BlockSpec((tm, LANES_V7X), lambda i: (i % tps, 0)),
                  pl.BlockSpec((tm, LANES_V7X), lambda i: (i % tps, 0))],
        out_specs=[pl.BlockSpec((tm, hq), lambda i: (i, 0)),
                   pl.BlockSpec((tm, hq), lambda i: (i, 0)),
                   pl.BlockSpec((1, MLA_HEADS, VT_ROWS, tm), lambda i: (i, 0, 0, 0))],
        out_shape=[jax.ShapeDtypeStruct((n, hq), BF16),
                   jax.ShapeDtypeStruct((n, hq), BF16),
                   jax.ShapeDtypeStruct((n // tm, MLA_HEADS, VT_ROWS, tm), BF16)],
        compiler_params=_params("parallel"),
        name="mla_in",
    )(x, g, w_in, q_norm, kv_norm, w_qb, w_k, w_vt, cos, sin)


def _attn_kernel(q_ref, k_ref, vt_ref, o_ref, s0, s1, p0, p1, acc_ref):
    tq = q_ref.shape[0]
    nk, _, _, tk = vt_ref.shape
    assert nk % 2 == 0
    q = q_ref[...]

    def scores(j, s_ref):
        start = pl.multiple_of(j * tk, tk)
        k = k_ref[pl.ds(start, tk), :]
        st = lax.dot_general(k, q, (((1,), (1,)), ((), ())), preferred_element_type=F32)
        s_ref[...] = st
        return jnp.max(st, axis=0, keepdims=True)

    def exps(s_ref, p_ref, m, tile_max):
        m_new = jnp.maximum(m, tile_max)
        p_ref[...] = jnp.exp2(s_ref[...] - m_new).astype(BF16)
        return m_new, jnp.exp2(m - m_new)

    def values(j, p_ref, alpha):
        acc_ref[...] = alpha * acc_ref[...] + jnp.dot(vt_ref[j, 0], p_ref[...], preferred_element_type=F32)

    acc_ref[...] = jnp.zeros_like(acc_ref)
    x0 = scores(0, s0)
    x1 = scores(1, s1)
    m, alpha = exps(s0, p0, jnp.full((1, tq), -jnp.inf, F32), x0)

    steps = nk - 2
    unroll = max(u for u in (2, 4, 6, 8, 10) if steps % u == 0) if steps else 2

    def group(i, carry):
        m, alpha, x1 = carry
        for t in range(0, unroll, 2):
            s = unroll * i + t + 1
            x0 = scores(s + 1, s0)
            values(s - 1, p0, alpha)
            m, alpha = exps(s1, p1, m, x1)
            x1 = scores(s + 2, s1)
            values(s, p1, alpha)
            m, alpha = exps(s0, p0, m, x0)
        return m, alpha, x1

    m, alpha, x1 = lax.fori_loop(0, steps // unroll, group, (m, alpha, x1))
    values(nk - 2, p0, alpha)
    m, alpha = exps(s1, p1, m, x1)
    values(nk - 1, p1, alpha)
    acc = acc_ref[...]
    out_t = acc[0:V_HEAD, :] / acc[V_HEAD:V_HEAD + 1, :]
    o_ref[...] = out_t.T.astype(BF16)


def _attention(q, k, vt, batch, seq):
    n = q.shape[0]
    tq = _tile(seq, ATTN_Q_TILE)
    tk = vt.shape[-1]
    nq = seq // tq
    nk = seq // tk
    hp = MLA_HEAD_PAD
    return pl.pallas_call(
        _attn_kernel,
        grid=(batch, MLA_HEADS, nq),
        in_specs=[pl.BlockSpec((tq, hp), lambda b, h, i: (b * nq + i, h)),
                  pl.BlockSpec((seq, hp), lambda b, h, i: (b, h)),
                  pl.BlockSpec((nk, 1, VT_ROWS, tk), lambda b, h, i: (b, h, 0, 0))],
        out_specs=pl.BlockSpec((tq, V_HEAD), lambda b, h, i: (b * nq + i, h)),
        out_shape=jax.ShapeDtypeStruct((n, MLA_HEADS * V_HEAD), BF16),
        scratch_shapes=[pltpu.VMEM((tk, tq), F32), pltpu.VMEM((tk, tq), F32),
                        pltpu.VMEM((tk, tq), BF16), pltpu.VMEM((tk, tq), BF16),
                        pltpu.VMEM((VT_ROWS, tq), F32)],
        compiler_params=_params("parallel", "parallel", "arbitrary"),
        name="mla_attention",
    )(q, k, vt)


def _proj_residual_kernel(a_ref, h_ref, w_ref, o_ref):
    o_ref[...] = h_ref[...] + jnp.dot(a_ref[...], w_ref[...], preferred_element_type=F32)


def _proj_residual(a, h, w):
    n, d = h.shape
    tm = _tile(n, TOKEN_TILE)
    return pl.pallas_call(
        _proj_residual_kernel,
        grid=(n // tm,),
        in_specs=[pl.BlockSpec((tm, a.shape[1]), lambda i: (i, 0)),
                  pl.BlockSpec((tm, d), lambda i: (i, 0)),
                  pl.BlockSpec(w.shape, lambda i: (0, 0))],
        out_specs=pl.BlockSpec((tm, d), lambda i: (i, 0)),
        out_shape=jax.ShapeDtypeStruct((n, d), F32),
        compiler_params=_params("parallel"),
        name="proj_residual",
    )(a, h, w)


def _ml_in_kernel(x_ref, g_ref, w_ref, b_ref, q_ref, k_ref, v_ref, o_ref, gt_ref):
    qw = ML_HEADS * ML_QK_PAD
    vw = ML_HEADS * ML_V
    xn = _rms(x_ref[...], g_ref[...]).astype(BF16)
    q = jnp.dot(xn, w_ref[:, 0:qw], preferred_element_type=F32)
    q_ref[...] = (q * (ML_QK ** -0.5)).astype(BF16)
    k_ref[...] = jnp.dot(xn, w_ref[:, qw:2 * qw], preferred_element_type=F32).astype(BF16)
    v_ref[...] = jnp.dot(xn, w_ref[:, 2 * qw:2 * qw + vw], preferred_element_type=F32).astype(BF16)
    o_ref[...] = jnp.dot(xn, w_ref[:, 2 * qw + vw:2 * qw + 2 * vw], preferred_element_type=F32).astype(BF16)
    gt_ref[...] = jnp.dot(xn, w_ref[:, 2 * qw + 2 * vw:], preferred_element_type=F32) + b_ref[...]


def _ml_in(x, g, w, b_gates):
    n, d = x.shape
    tm = _tile(n, TOKEN_TILE)
    qw = ML_HEADS * ML_QK_PAD
    vw = ML_HEADS * ML_V
    ng = 4 * ML_HEADS
    row = lambda i: (i, 0)
    const = lambda i: (0, 0)
    return pl.pallas_call(
        _ml_in_kernel,
        grid=(n // tm,),
        in_specs=[pl.BlockSpec((tm, d), row), pl.BlockSpec((1, d), const),
                  pl.BlockSpec(w.shape, const), pl.BlockSpec((1, ng), const)],
        out_specs=[pl.BlockSpec((tm, qw), row), pl.BlockSpec((tm, qw), row),
                   pl.BlockSpec((tm, vw), row), pl.BlockSpec((tm, vw), row),
                   pl.BlockSpec((tm, ng), row)],
        out_shape=[jax.ShapeDtypeStruct((n, qw), BF16), jax.ShapeDtypeStruct((n, qw), BF16),
                   jax.ShapeDtypeStruct((n, vw), BF16), jax.ShapeDtypeStruct((n, vw), BF16),
                   jax.ShapeDtypeStruct((n, ng), F32)],
        compiler_params=_params("parallel"),
        name="mlstm_in",
    )(x, g, w, b_gates)


def _split3(x):
    hi = x.astype(BF16)
    r1 = x - hi.astype(F32)
    mid = r1.astype(BF16)
    lo = (r1 - mid.astype(F32)).astype(BF16)
    return hi, mid, lo


def _mlstm_kernel(q_ref, k_ref, vt_ref, gc_ref, gr_ref, o_ref, c_scr, m_scr):
    lc = q_ref.shape[0]
    nh = ML_HEADS
    kp = ML_QK_PAD
    dv = ML_V
    fwd = pl.program_id(0) == 0

    @pl.when(pl.program_id(2) == 0)
    def _():
        c_scr[...] = jnp.zeros_like(c_scr)
        m_scr[...] = jnp.zeros_like(m_scr)

    gcol = gc_ref[0]
    grow = gr_ref[0]
    li_col = gcol[:, 0:nh]
    lf_col = jax.nn.log_sigmoid(gcol[:, nh:2 * nh])
    li_row = grow[0:nh, :]
    lf_row = jax.nn.log_sigmoid(grow[nh:2 * nh, :])

    r = lax.broadcasted_iota(jnp.int32, (lc, lc), 0)
    cidx = lax.broadcasted_iota(jnp.int32, (lc, lc), 1)
    sign = jnp.where(fwd, 1, -1)
    tri = jnp.where((r - cidx) * sign >= 0, 1.0, 0.0).astype(BF16)
    allowed_t = (cidx - r) * sign >= 0

    b_col = jnp.zeros((lc, nh), F32)
    for piece in _split3(lf_col):
        b_col = b_col + jnp.dot(tri, piece, preferred_element_type=F32)
    b_row = jnp.zeros((nh, lc), F32)
    for piece in _split3(lf_row):
        b_row = b_row + lax.dot_general(piece, tri, (((1,), (1,)), ((), ())), preferred_element_type=F32)
    a_all = jnp.sum(lf_row, axis=-1, keepdims=True)
    src_col = li_col - b_col

    sub = lax.broadcasted_iota(jnp.int32, (dv, lc), 0)
    ones_rows = jnp.where(sub == 0, 1.0, 0.0).astype(BF16)
    nt = (((1,), (1,)), ((), ()))

    for h in range(nh):
        q = q_ref[:, h * kp:(h + 1) * kp]
        k = k_ref[:, h * kp:(h + 1) * kp]
        vt_ext = jnp.concatenate([vt_ref[h * dv:(h + 1) * dv, :], ones_rows], axis=0)
        b_r = b_row[h:h + 1, :]
        li_r = li_row[h:h + 1, :]
        a = a_all[h:h + 1, :]
        m_in = m_scr[h][0:1, 0:1]
        ct_ext = c_scr[h]

        w_end = a - b_r + li_r
        g = jnp.max(w_end, axis=-1, keepdims=True)
        e_r = jnp.exp(w_end - g)

        dmat = jnp.where(allowed_t, b_r + src_col[:, h:h + 1], -jnp.inf)
        inter_log = b_r + m_in
        m_j = jnp.maximum(inter_log, jnp.max(dmat, axis=0, keepdims=True))
        inter = jnp.exp(inter_log - m_j)
        p = jnp.exp(dmat - m_j)
        s = lax.dot_general(k, q, nt, preferred_element_type=F32)
        qk = (s * p).astype(BF16)
        nd = (jnp.dot(vt_ext, qk, preferred_element_type=F32)
              + inter * lax.dot_general(ct_ext.astype(BF16), q, nt, preferred_element_type=F32))
        den = nd[dv:dv + 1, :]
        out_t = nd[0:dv, :] / jnp.maximum(jnp.abs(den), jnp.exp(-m_j))
        o_ref[0, :, h * dv:(h + 1) * dv] = out_t.T

        m_new = jnp.maximum(a + m_in, g)
        fdec = jnp.exp(a + m_in - m_new)
        iin = jnp.exp(g - m_new)
        vte = (vt_ext.astype(F32) * e_r).astype(BF16)
        c_scr[h] = fdec * ct_ext + iin * jnp.dot(vte, k, preferred_element_type=F32)
        m_scr[h] = jnp.broadcast_to(m_new, m_scr.shape[1:])


def _mlstm(q, k, vt, gcol, grow, batch, seq):
    n = q.shape[0]
    lc = _tile(seq, ML_CHUNK_TILE)
    nc = seq // lc
    qw = ML_HEADS * ML_QK_PAD
    vw = ML_HEADS * ML_V

    def chunk(d, b, c):
        return b * nc + c + d * (nc - 1 - 2 * c)

    return pl.pallas_call(
        _mlstm_kernel,
        grid=(2, batch, nc),
        in_specs=[pl.BlockSpec((lc, qw), lambda d, b, c: (chunk(d, b, c), 0)),
                  pl.BlockSpec((lc, qw), lambda d, b, c: (chunk(d, b, c), 0)),
                  pl.BlockSpec((vw, lc), lambda d, b, c: (0, chunk(d, b, c))),
                  pl.BlockSpec((1, lc, 2 * ML_HEADS), lambda d, b, c: (d, chunk(d, b, c), 0)),
                  pl.BlockSpec((1, 2 * ML_HEADS, lc), lambda d, b, c: (d, 0, chunk(d, b, c)))],
        out_specs=pl.BlockSpec((1, lc, vw), lambda d, b, c: (d, chunk(d, b, c), 0)),
        out_shape=jax.ShapeDtypeStruct((2, n, vw), F32),
        scratch_shapes=[pltpu.VMEM((ML_HEADS, 2 * ML_V, ML_QK_PAD), F32),
                        pltpu.VMEM((ML_HEADS, F32_SUBLANES_V7X, LANES_V7X), F32)],
        compiler_params=_params("parallel", "parallel", "arbitrary"),
        name="mlstm_scan",
    )(q, k, vt, gcol, grow)


def _ml_out_kernel(hs_ref, og_ref, hn_ref, h_ref, w_ref, o_ref):
    dv = ML_V
    hs = hs_ref[0] + hs_ref[1]
    parts = []
    for hd in range(ML_HEADS):
        x = hs[:, hd * dv:(hd + 1) * dv]
        parts.append(x * lax.rsqrt(jnp.mean(x * x, axis=-1, keepdims=True) + EPS))
    y = jnp.concatenate(parts, axis=1) * hn_ref[...] * jax.nn.sigmoid(og_ref[...].astype(F32))
    o_ref[...] = h_ref[...] + jnp.dot(y.astype(BF16), w_ref[...], preferred_element_type=F32)


def _ml_out(hs, og, head_norm, h, w):
    n, d = h.shape
    vw = ML_HEADS * ML_V
    tm = _tile(n, TOKEN_TILE)
    row = lambda i: (i, 0)
    const = lambda i: (0, 0)
    return pl.pallas_call(
        _ml_out_kernel,
        grid=(n // tm,),
        in_specs=[pl.BlockSpec((2, tm, vw), lambda i: (0, i, 0)),
                  pl.BlockSpec((tm, vw), row), pl.BlockSpec((1, vw), const),
                  pl.BlockSpec((tm, d), row), pl.BlockSpec(w.shape, const)],
        out_specs=pl.BlockSpec((tm, d), row),
        out_shape=jax.ShapeDtypeStruct((n, d), F32),
        compiler_params=_params("parallel"),
        name="mlstm_out",
    )(hs, og, head_norm, h, w)


def _router_kernel(x_ref, g_ref, w_ref, wlo_ref, xn_ref, aff_ref, *, nreal):
    @pl.when(pl.program_id(0) < nreal)
    def _():
        xf = _rms(x_ref[...], g_ref[...])
        xn = xf.astype(BF16)
        xn_ref[...] = xn
        xlo = (xf - xn.astype(F32)).astype(BF16)
        logits = (jnp.dot(xn, w_ref[...], preferred_element_type=F32)
                  + jnp.dot(xlo, w_ref[...], preferred_element_type=F32)
                  + jnp.dot(xn, wlo_ref[...], preferred_element_type=F32))
        z = jnp.exp(logits - jnp.max(logits, axis=-1, keepdims=True))
        aff_ref[...] = z / jnp.sum(z, axis=-1, keepdims=True)

    @pl.when(pl.program_id(0) >= nreal)
    def _():
        xn_ref[...] = jnp.zeros_like(xn_ref)


def _router(x, g, w_f32):
    n, d = x.shape
    tm = _tile(n, TOKEN_TILE)
    nreal = n // tm
    npad = max(n, GATHER_OPERAND_ROWS)
    row = lambda i: (jnp.minimum(i, nreal - 1), 0)
    const = lambda i: (0, 0)
    w = w_f32.astype(BF16)
    w_lo = (w_f32 - w.astype(F32)).astype(BF16)
    return pl.pallas_call(
        functools.partial(_router_kernel, nreal=nreal),
        grid=(npad // tm,),
        in_specs=[pl.BlockSpec((tm, d), row), pl.BlockSpec((1, d), const),
                  pl.BlockSpec(w.shape, const), pl.BlockSpec(w.shape, const)],
        out_specs=[pl.BlockSpec((tm, d), lambda i: (i, 0)), pl.BlockSpec((tm, N_EXPERTS), row)],
        out_shape=[jax.ShapeDtypeStruct((npad, d), BF16), jax.ShapeDtypeStruct((n, N_EXPERTS), F32)],
        compiler_params=_params("arbitrary"),
        name="moe_router",
    )(x, g, w, w_lo)


def _ffn_kernel(*refs, nchunk, epc):
    x_refs = refs[:nchunk]
    gate_ref, wg_ref, wu_ref, wd_ref, o_ref = refs[nchunk:]

    def compute(x):
        g = jnp.dot(x, wg_ref[0, 0].astype(BF16), preferred_element_type=F32)
        u = jnp.dot(x, wu_ref[0, 0].astype(BF16), preferred_element_type=F32)
        hid = (g * jax.nn.sigmoid(g) * u).astype(BF16)
        y = jnp.dot(hid, wd_ref[0, 0].astype(BF16), preferred_element_type=F32)
        o_ref[0] = (y * gate_col()).astype(BF16)

    def gate_col():
        tm = o_ref.shape[1]
        groups = tm // LANES_V7X
        g = gate_ref[0, pl.ds(pl.program_id(1) * groups, groups), :]
        eye = (lax.broadcasted_iota(jnp.int32, (LANES_V7X, LANES_V7X), 0)
               == lax.broadcasted_iota(jnp.int32, (LANES_V7X, LANES_V7X), 1))
        return jnp.concatenate([jnp.sum(jnp.where(eye, g[k:k + 1, :], 0.0), axis=1, keepdims=True)
                                for k in range(groups)], axis=0)

    if nchunk == 1:
        compute(x_refs[0][0])
    else:
        chunk = pl.program_id(0) // epc
        for c in range(nchunk):
            @pl.when(chunk == c)
            def _(c=c):
                compute(x_refs[c][0])


def _expert_ffn(xes, gate, wg, wu, wd, layer):
    nchunk = len(xes)
    epc, cap, d = xes[0].shape
    e = nchunk * epc
    f = wg.shape[-1]
    tm = _tile(cap, TOKEN_TILE)
    nj = cap // tm
    tok = lambda i, j: (i, j, 0)
    wmap = lambda i, j: (layer, i, 0, 0)

    def chunk_map(c):
        def index(i, j):
            jj = jnp.where(i < c * epc, 0, jnp.where(i >= (c + 1) * epc, nj - 1, j))
            return (jnp.clip(i - c * epc, 0, epc - 1), jj, 0)
        return index

    return pl.pallas_call(
        functools.partial(_ffn_kernel, nchunk=nchunk, epc=epc),
        grid=(e, nj),
        in_specs=[pl.BlockSpec((1, tm, d), chunk_map(c)) for c in range(nchunk)]
        + [pl.BlockSpec((1, cap // LANES_V7X, LANES_V7X), lambda i, j: (i, 0, 0)),
           pl.BlockSpec((1, 1, d, f), wmap), pl.BlockSpec((1, 1, d, f), wmap),
           pl.BlockSpec((1, 1, f, d), wmap)],
        out_specs=pl.BlockSpec((1, tm, d), tok),
        out_shape=jax.ShapeDtypeStruct((e, cap, d), BF16),
        compiler_params=_params("parallel", "arbitrary"),
        name="moe_ffn",
    )(*xes, gate, wg, wu, wd)


def _threshold_kernel(a_ref, t_ref, need_ref, *, cap):
    bits = pltpu.bitcast(a_ref[...], jnp.int32)
    ne = bits.shape[0]

    def count(mask):
        return jnp.sum(jnp.where(mask, 1.0, 0.0), axis=1, keepdims=True)

    t = jnp.zeros((ne, 1), jnp.int32)
    for k in range(30, -1, -1):
        cand = t | (1 << k)
        t = jnp.where(count(bits >= cand) >= cap, cand, t)
    need = cap - count(bits > t)
    t_ref[...] = jnp.broadcast_to(t, t_ref.shape)
    need_ref[...] = jnp.broadcast_to(need, need_ref.shape)


def _threshold(aff_t, cap):
    e, n = aff_t.shape
    return pl.pallas_call(
        functools.partial(_threshold_kernel, cap=cap),
        grid=(1,),
        in_specs=[pl.BlockSpec((e, n), lambda i: (0, 0))],
        out_specs=[pl.BlockSpec((e, LANES_V7X), lambda i: (0, 0)), pl.BlockSpec((e, LANES_V7X), lambda i: (0, 0))],
        out_shape=[jax.ShapeDtypeStruct((e, LANES_V7X), jnp.int32), jax.ShapeDtypeStruct((e, LANES_V7X), F32)],
        compiler_params=_params("arbitrary"),
        name="moe_threshold",
    )(aff_t)


def _positions_kernel(aff_ref, thr_ref, need_ref, pos_ref, gate_ref, cnt_ref, rc, tc):
    tb = aff_ref.shape[0]

    @pl.when(pl.program_id(0) == 0)
    def _():
        rc[...] = jnp.zeros_like(rc)
        tc[...] = jnp.zeros_like(tc)

    a = aff_ref[...]
    bits = pltpu.bitcast(a, jnp.int32)
    thr = thr_ref[...]
    r = lax.broadcasted_iota(jnp.int32, (tb, tb), 0)
    c = lax.broadcasted_iota(jnp.int32, (tb, tb), 1)
    tril = jnp.where(r >= c, 1.0, 0.0).astype(BF16)
    eq = jnp.where(bits == thr, 1.0, 0.0)
    eq_incl = jnp.dot(tril, eq.astype(BF16), preferred_element_type=F32)
    tie_rank = tc[0:1, :] + eq_incl - eq
    sel = jnp.where(bits > thr, 1.0, jnp.where(tie_rank < need_ref[...], eq, 0.0))
    sel_incl = jnp.dot(tril, sel.astype(BF16), preferred_element_type=F32)
    taken = sel > 0.5
    pos = (rc[0:1, :] + sel_incl - 1.0).astype(jnp.int32)
    pos_ref[...] = jnp.where(taken, pos, -1)
    gate_ref[...] = jnp.where(taken, a, 0.0)
    cnt_ref[0] = jnp.broadcast_to(rc[0:1, :], cnt_ref.shape[1:]).astype(jnp.int32)
    rc[...] = rc[...] + jnp.sum(sel, axis=0, keepdims=True)
    tc[...] = tc[...] + jnp.sum(eq, axis=0, keepdims=True)


def _positions(aff, thr_row, need_row):
    n, e = aff.shape
    tb = _tile(n, TOKEN_TILE)
    nb = n // tb
    row = lambda i: (i, 0)
    const = lambda i: (0, 0)
    return pl.pallas_call(
        _positions_kernel,
        grid=(nb,),
        in_specs=[pl.BlockSpec((tb, e), row), pl.BlockSpec((1, e), const), pl.BlockSpec((1, e), const)],
        out_specs=[pl.BlockSpec((tb, e), row), pl.BlockSpec((tb, e), row),
                   pl.BlockSpec((1, F32_SUBLANES_V7X, e), lambda i: (i, 0, 0))],
        out_shape=[jax.ShapeDtypeStruct((n, e), jnp.int32), jax.ShapeDtypeStruct((n, e), F32),
                   jax.ShapeDtypeStruct((nb, F32_SUBLANES_V7X, e), jnp.int32)],
        scratch_shapes=[pltpu.VMEM((F32_SUBLANES_V7X, e), F32), pltpu.VMEM((F32_SUBLANES_V7X, e), F32)],
        compiler_params=_params("arbitrary"),
        name="moe_positions",
    )(aff, thr_row, need_row)


def _compact_kernel(lo_ref, pt_ref, gt_ref, idx_ref, gate_ref):
    ne, tb = pt_ref.shape
    _, nrow, lanes = idx_ref.shape
    cap = nrow * lanes
    win = COMBINE_WINDOW
    align = BF16_SUBLANES_V7X
    b = pl.program_id(0)

    @pl.when(b == 0)
    def _():
        idx_ref[...] = jnp.zeros_like(idx_ref)
        gate_ref[...] = jnp.zeros_like(gate_ref)

    tok = b * tb + lax.broadcasted_iota(jnp.int32, (1, tb), 1)
    digits = [(tok >> 8).astype(F32).astype(BF16), (tok & 255).astype(F32).astype(BF16)]
    pad = jnp.zeros((3, tb), BF16)
    slot = lax.broadcasted_iota(jnp.int32, (win, tb), 0)
    zeros = jnp.zeros((1, lanes), F32)
    nt = (((1,), (1,)), ((), ()))

    def window(e, w, start0):
        nominal = start0 + w * win
        start = jnp.minimum(nominal, cap - win)
        rel = pt_ref[e:e + 1, :] - start
        rel = jnp.where(rel >= nominal - start, rel, -1)
        sel_t = jnp.where(slot == rel, 1.0, 0.0).astype(BF16)
        lhs = jnp.concatenate(digits + list(_split3(gt_ref[e:e + 1, :])) + [pad], axis=0)
        got = lax.dot_general(lhs, sel_t, nt, preferred_element_type=F32)
        ids = got[0:1, :] * 256.0 + got[1:2, :]
        gts = got[2:3, :] + got[3:4, :] + got[4:5, :]
        off = start % lanes
        r0 = start // lanes
        r1 = jnp.minimum(r0 + 1, nrow - 1)
        ids2 = pltpu.roll(jnp.concatenate([ids, zeros], axis=1), off, axis=1)
        gts2 = pltpu.roll(jnp.concatenate([gts, zeros], axis=1), off, axis=1)
        idx_ref[e, pl.ds(r0, 1), :] += ids2[:, 0:lanes].astype(jnp.int32)
        idx_ref[e, pl.ds(r1, 1), :] += ids2[:, lanes:].astype(jnp.int32)
        gate_ref[e, pl.ds(r0, 1), :] += gts2[:, 0:lanes]
        gate_ref[e, pl.ds(r1, 1), :] += gts2[:, lanes:]

    starts = [lo_ref[b, e] // align * align for e in range(ne)]
    for e in range(ne):
        window(e, 0, starts[e])
    for e in range(ne):
        nwin = (lo_ref[b + 1, e] - starts[e] + win - 1) // win
        lax.fori_loop(1, nwin, lambda w, c, e=e: (window(e, w, starts[e]), c)[1], 0)


def _compact(lo, pos_t, gate_t, cap):
    e, n = pos_t.shape
    assert COMBINE_WINDOW == LANES_V7X
    tb = _tile(n, TOKEN_TILE)
    shape3 = (e, cap // LANES_V7X, LANES_V7X)
    grid_spec = pltpu.PrefetchScalarGridSpec(
        num_scalar_prefetch=1,
        grid=(n // tb,),
        in_specs=[pl.BlockSpec((e, tb), lambda i, lo_ref: (0, i)),
                  pl.BlockSpec((e, tb), lambda i, lo_ref: (0, i))],
        out_specs=[pl.BlockSpec(shape3, lambda i, lo_ref: (0, 0, 0)),
                   pl.BlockSpec(shape3, lambda i, lo_ref: (0, 0, 0))])
    return pl.pallas_call(
        _compact_kernel,
        grid_spec=grid_spec,
        out_shape=[jax.ShapeDtypeStruct(shape3, jnp.int32), jax.ShapeDtypeStruct(shape3, F32)],
        compiler_params=_params("arbitrary"),
        name="moe_compact",
    )(lo, pos_t, gate_t)


def _combine_kernel(lo_ref, h_ref, idx_ref, ye_hbm, *rest, final):
    (gf_ref, o_ref, buf, sem) = rest if final else (None,) + rest
    tb, d = h_ref.shape
    ne, nrow, lanes = idx_ref.shape
    cap = nrow * lanes
    win = COMBINE_WINDOW
    align = BF16_SUBLANES_V7X
    b = pl.program_id(0)
    nb = pl.num_programs(0)
    tok = b * tb + lax.broadcasted_iota(jnp.int32, (tb, 1), 0)
    lane = lax.broadcasted_iota(jnp.int32, (1, win), 1)

    def clamp(start):
        return jnp.minimum(start, cap - win)

    def first_start(blk, e):
        return clamp(lo_ref[e, blk] // align * align)

    def fetch(e, start):
        return pltpu.make_async_copy(ye_hbm.at[e, pl.ds(pl.multiple_of(start, align), win), :],
                                     buf.at[pl.ds(e * win, win)], sem.at[e])

    def select(e, start, row_lo, row_hi):
        r0 = start // lanes
        two = jnp.concatenate([idx_ref[e, pl.ds(r0, 1), :],
                               idx_ref[e, pl.ds(jnp.minimum(r0 + 1, nrow - 1), 1), :]], axis=1)
        ids = pltpu.roll(two, (2 * lanes - start % lanes) % (2 * lanes), axis=1)[:, 0:win]
        rid = start + lane
        ids = jnp.where((rid >= row_lo) & (rid < row_hi), ids, -1)
        return jnp.where(tok == ids, 1.0, 0.0).astype(BF16)

    @pl.when(b == 0)
    def _():
        for e in range(ne):
            fetch(e, first_start(0, e)).start()

    o_ref[...] = h_ref[...]
    group = 4
    total = None
    for e0 in range(0, ne, group):
        first = []
        for e in range(e0, e0 + group):
            lo = lo_ref[e, b]
            hi = lo_ref[e, b + 1]
            start0 = lo // align * align
            first.append((e, lo, hi, start0, clamp(start0)))
        for e, lo, hi, start0, start in first:
            fetch(e, start).wait()
        sel = jnp.concatenate([select(e, start, lo, hi) for e, lo, hi, start0, start in first], axis=1)
        part = jnp.dot(sel, buf[e0 * win:(e0 + group) * win, :], preferred_element_type=F32)
        total = part if total is None else total + part

        for e, lo, hi, start0, start in first:
            def more(w, carry, e=e, lo=lo, hi=hi, start0=start0):
                nominal = start0 + w * win
                st = clamp(nominal)
                cp = fetch(e, st)
                cp.start()
                cp.wait()
                o_ref[...] += jnp.dot(select(e, st, jnp.maximum(lo, nominal), hi), buf[e * win:(e + 1) * win, :],
                                      preferred_element_type=F32)
                return carry

            nwin = (hi - start0 + win - 1) // win
            lax.fori_loop(1, nwin, more, 0)

            @pl.when(b + 1 < nb)
            def _(e=e):
                fetch(e, first_start(b + 1, e)).start()

    if final:
        o_ref[...] = _rms(o_ref[...] + total, gf_ref[...])
    else:
        o_ref[...] += total


def _combine(h, lo, idx3, ye, g_final=None):
    n, d = h.shape
    e, cap, _ = ye.shape
    assert cap >= COMBINE_WINDOW and cap % LANES_V7X == 0 and e % 4 == 0 and COMBINE_WINDOW <= LANES_V7X
    tb = _tile(n, TOKEN_TILE)
    nb = n // tb
    final = g_final is not None
    grid_spec = pltpu.PrefetchScalarGridSpec(
        num_scalar_prefetch=1,
        grid=(nb,),
        in_specs=[pl.BlockSpec((tb, d), lambda i, lo_ref: (i, 0)),
                  pl.BlockSpec(idx3.shape, lambda i, lo_ref: (0, 0, 0)),
                  pl.BlockSpec(memory_space=pl.ANY)]
        + ([pl.BlockSpec((1, d), lambda i, lo_ref: (0, 0))] if final else []),
        out_specs=pl.BlockSpec((tb, d), lambda i, lo_ref: (i, 0)),
        scratch_shapes=[pltpu.VMEM((e * COMBINE_WINDOW, d), BF16), pltpu.SemaphoreType.DMA((e,))])
    return pl.pallas_call(
        functools.partial(_combine_kernel, final=final),
        grid_spec=grid_spec,
        out_shape=jax.ShapeDtypeStruct((n, d), F32),
        compiler_params=_params("arbitrary"),
        name="moe_combine",
    )(lo, h, idx3, ye, *([g_final] if final else []))


def _moe(h, g, w_router, wg, wu, wd, layer, g_final=None):
    n, d = h.shape
    cap = EC_CAPACITY * n // N_EXPERTS
    xn, aff = _router(h, g, w_router)
    thr, need = _threshold(aff.T, cap)
    pos, gate, cnt = _positions(aff, thr[:, 0][None, :], need[:, 0][None, :])
    lo = jnp.concatenate([cnt[:, 0, :], jnp.full((1, N_EXPERTS), cap, jnp.int32)], axis=0)
    idx3, gate3 = _compact(lo, pos.T, gate.T, cap)
    idx = idx3.reshape(N_EXPERTS, cap)
    nchunk = max(1, idx.size // GATHER_ROWS)
    epc = N_EXPERTS // nchunk
    xes = [xn[idx[c * epc:(c + 1) * epc]] for c in range(nchunk)]
    ye = _expert_ffn(xes, gate3, wg, wu, wd, layer)
    return _combine(h, lo.T, idx3, ye, g_final)


def _rope_group_cols(w_pe):
    half = QK_ROPE // 2
    z = jnp.zeros((w_pe.shape[0], LANES_V7X // 2 - half), w_pe.dtype)
    return jnp.concatenate([w_pe[:, :half], z, w_pe[:, half:], z], axis=1)


def _rope_tables(seq):
    half = QK_ROPE // 2
    pos = jnp.arange(seq, dtype=F32)
    inv = ROPE_THETA ** (-jnp.arange(0, QK_ROPE, 2, dtype=F32) / QK_ROPE)
    ang = pos[:, None] * inv[None, :]
    c, s = jnp.cos(ang), jnp.sin(ang)
    z = jnp.zeros((seq, LANES_V7X // 2 - half), F32)
    return jnp.concatenate([c, z, c, z], axis=1), jnp.concatenate([-s, z, s, z], axis=1)


def _prep_mla(w_in, w_qb, w_kvb):
    lat = Q_LORA + KV_LORA
    w_in_p = jnp.concatenate([w_in[:, :lat], _rope_group_cols(w_in[:, lat:])], axis=1).astype(BF16)
    hd = QK_NOPE + QK_ROPE
    cols = []
    for h in range(MLA_HEADS):
        cols.append(w_qb[:, h * hd:h * hd + QK_NOPE])
        cols.append(_rope_group_cols(w_qb[:, h * hd + QK_NOPE:(h + 1) * hd]))
    w_kv = w_kvb.reshape(KV_LORA, MLA_HEADS, QK_NOPE + V_HEAD)
    w_k = w_kv[:, :, :QK_NOPE].reshape(KV_LORA, MLA_HEADS * QK_NOPE).astype(BF16)
    w_vt = w_kv[:, :, QK_NOPE:].reshape(KV_LORA, MLA_HEADS * V_HEAD).T.astype(BF16)
    return w_in_p, jnp.concatenate(cols, axis=1).astype(BF16), w_k, w_vt


def _prep_ml_in(w):
    qd = ML_HEADS * ML_QK
    cols = []
    for base in (0, qd):
        for h in range(ML_HEADS):
            cols.append(w[:, base + h * ML_QK:base + (h + 1) * ML_QK])
            cols.append(jnp.zeros((w.shape[0], ML_QK_PAD - ML_QK), w.dtype))
    cols.append(w[:, 2 * qd:])
    return jnp.concatenate(cols, axis=1).astype(BF16)


def _trunk(x, p):
    batch, seq, d = x.shape
    n = batch * seq
    h = x.reshape(n, d)
    for i in range(DEPTH):
        j = i // N_MIXERS
        kind = i % N_MIXERS
        g_mix = p["norm_mix"][i][None, :]
        if kind == 0:
            u, gb = _conv_in(h, g_mix, p["conv_w_in"][j])
            h = _conv_out(u, gb, h, p["conv_w_dw"][j], p["conv_w_out"][j], seq)
        elif kind == 1:
            cos, sin = _rope_tables(seq)
            q, k, vt = _mla_in(h, g_mix, p["mla_w_in"][j], p["mla_q_norm"][j][None, :], p["mla_w_qb"][j],
                               p["mla_kv_norm"][j][None, :], p["mla_w_k"][j], p["mla_w_vt"][j], cos, sin, seq)
            o = _attention(q, k, vt, batch, seq)
            h = _proj_residual(o, h, p["mla_w_out"][j])
        else:
            q, k, v, og, gates = _ml_in(h, g_mix, p["ml_w_in"][j], p["ml_b_gates"][j][None, :])
            gcol = gates.reshape(n, 2, 2 * ML_HEADS).transpose(1, 0, 2)
            grow = gcol.transpose(0, 2, 1)
            hs = _mlstm(q, k, v.T, gcol, grow, batch, seq)
            h = _ml_out(hs, og, p["ml_head_norm"][j][None, :], h, p["ml_w_out"][j])
        g_final = p["norm_final"][None, :] if i == DEPTH - 1 else None
        h = _moe(h, p["norm_ffn"][i][None, :], p["router_w"][i],
                 p["exp_w_gate"], p["exp_w_up"], p["exp_w_down"], i, g_final)
    return h.reshape(batch, seq, d)


def kernel(x_prompt, x_sample, conv_w_in, conv_w_dw, conv_w_out, mla_w_in, mla_q_norm, mla_w_qb, mla_kv_norm, mla_w_kvb, mla_w_out, ml_w_in, ml_b_gates, ml_head_norm, ml_w_out, norm_mix, norm_ffn, router_w, exp_w_gate, exp_w_up, exp_w_down, norm_final):
    mla = [_prep_mla(mla_w_in[j], mla_w_qb[j], mla_w_kvb[j]) for j in range(mla_w_in.shape[0])]
    p = dict(
        conv_w_in=conv_w_in.astype(BF16), conv_w_dw=conv_w_dw, conv_w_out=conv_w_out.astype(BF16),
        mla_w_in=[m[0] for m in mla], mla_q_norm=mla_q_norm, mla_w_qb=[m[1] for m in mla],
        mla_kv_norm=mla_kv_norm, mla_w_k=[m[2] for m in mla], mla_w_vt=[m[3] for m in mla],
        mla_w_out=mla_w_out.astype(BF16),
        ml_w_in=[_prep_ml_in(ml_w_in[j]) for j in range(ml_w_in.shape[0])], ml_b_gates=ml_b_gates,
        ml_head_norm=ml_head_norm, ml_w_out=ml_w_out.astype(BF16),
        norm_mix=norm_mix, norm_ffn=norm_ffn, router_w=router_w,
        exp_w_gate=exp_w_gate, exp_w_up=exp_w_up, exp_w_down=exp_w_down, norm_final=norm_final)
    return (_trunk(x_prompt, p), _trunk(x_sample, p))
```
